```python
import jax, jax.numpy as jnp
from jax import lax
import numpy as np

D_MODEL = 2048
BATCH = 2
SEQ = 4096
DEPTH = 1

GRID_W = 64
PLE_DIM = 256
D_ATTN = D_MODEL // 2
D_CONV = D_MODEL - D_ATTN
N_HEADS = 8
HEAD_DIM = D_ATTN // N_HEADS
N_CONV_GROUPS = 8
CONV_W = 3
WIN_ROWS_MAX = 8
WIN_COLS = 16
D_FF = 5632
RMS_EPS = 1e-6
D_IN = 3 * D_ATTN + 3 * D_CONV

kernel_name = 'hybrid_natten_shortconv_macaron_block'


def _rmsnorm(x, g):
    x32 = x.astype(jnp.float32)
    y = x32 * lax.rsqrt(jnp.mean(x32 * x32, axis=-1, keepdims=True) + RMS_EPS)
    return (y * g.astype(jnp.float32)).astype(x.dtype)


def _group_rmsnorm(x, g, n_groups):
    b, s, c = x.shape
    xg = x.reshape(b, s, n_groups, c // n_groups).astype(jnp.float32)
    y = xg * lax.rsqrt(jnp.mean(xg * xg, axis=-1, keepdims=True) + RMS_EPS)
    return (y.reshape(b, s, c) * g.astype(jnp.float32)).astype(x.dtype)


def _swiglu(x, wg, wu, wd):
    return (jax.nn.silu(x @ wg) * (x @ wu)) @ wd


def _neighbourhood_attention(q, k, v, rpb):
    b, s, h, dh = q.shape
    rows = s // GRID_W
    kr = min(WIN_ROWS_MAX, rows)
    r = jnp.arange(rows)
    row_start = jnp.clip(r - kr // 2, 0, rows - kr)
    row_idx = row_start[:, None] + jnp.arange(kr)[None, :]
    c = jnp.arange(GRID_W)
    col_start = jnp.clip(c - WIN_COLS // 2, 0, GRID_W - WIN_COLS)
    in_win = (c[None, :] >= col_start[:, None]) & (c[None, :] < col_start[:, None] + WIN_COLS)

    qg = q.reshape(b, rows, GRID_W, h, dh)
    kg = k.reshape(b, rows, GRID_W, h, dh)[:, row_idx]
    vg = v.reshape(b, rows, GRID_W, h, dh)[:, row_idx]

    scores = jnp.einsum('brqhd,brkwhd->bhrqkw', qg, kg).astype(jnp.float32) * (dh ** -0.5)

    rel_r = row_idx - r[:, None]
    rel_c = jnp.clip(c[None, :] - c[:, None], -(WIN_COLS - 1), WIN_COLS - 1)
    bias = rpb[:, rel_r[:, None, :, None] + (WIN_ROWS_MAX - 1),
               rel_c[None, :, None, :] + (WIN_COLS - 1)]
    scores = scores + bias.astype(jnp.float32)[None]
    scores = jnp.where(in_win[:, None, :], scores, -1e30)
    probs = jax.nn.softmax(scores, axis=(-2, -1))
    out = jnp.einsum('bhrqkw,brkwhd->brqhd', probs.astype(v.dtype), vg)
    return out.reshape(b, s, h * dh)


def _short_conv(u, w, bias):
    s = u.shape[1]
    pad = CONV_W // 2
    up = jnp.pad(u, ((0, 0), (pad, CONV_W - 1 - pad), (0, 0)))
    y = up[:, 0:s] * w[0]
    for j in range(1, CONV_W):
        y = y + up[:, j:j + s] * w[j]
    return y + bias


def setup_inputs(seed: int = 0) -> dict:
    key = jax.random.key(seed)
    ks = jax.random.split(key, 24)
    f32 = jnp.float32

    def nrm(k, shape, scale):
        return jax.random.normal(k, shape, f32) * scale

    def gain(k, shape):
        return 1.0 + 0.02 * jax.random.normal(k, shape, f32)

    L, D = DEPTH, D_MODEL
    return {
        'x': nrm(ks[0], (BATCH, SEQ, D), 1.0),
        'p': nrm(ks[1], (DEPTH, BATCH, SEQ, PLE_DIM), 1.0),
        'ffn1_norm': gain(ks[2], (L, D)),
        'ffn1_wg': nrm(ks[3], (L, D, D_FF), D ** -0.5),
        'ffn1_wu': nrm(ks[4], (L, D, D_FF), D ** -0.5),
        'ffn1_wd': nrm(ks[5], (L, D_FF, D), D_FF ** -0.5),
        'mix_norm': gain(ks[6], (L, D)),
        'w_in': nrm(ks[7], (L, D, D_IN), D ** -0.5),
        'rpb': nrm(ks[8], (L, N_HEADS, 2 * WIN_ROWS_MAX - 1, 2 * WIN_COLS - 1), 0.1),
        'conv_w': nrm(ks[9], (L, CONV_W, D_CONV), CONV_W ** -0.5),
        'conv_b': nrm(ks[10], (L, D_CONV), 0.01),
        'attn_out_norm': gain(ks[11], (L, D_ATTN)),
        'conv_out_norm': gain(ks[12], (L, D_CONV)),
        'w_out': nrm(ks[13], (L, D_ATTN + D_CONV, D), (D_ATTN + D_CONV) ** -0.5),
        'ffn2_norm': gain(ks[14], (L, D)),
        'ffn2_wg': nrm(ks[15], (L, D, D_FF), D ** -0.5),
        'ffn2_wu': nrm(ks[16], (L, D, D_FF), D ** -0.5),
        'ffn2_wd': nrm(ks[17], (L, D_FF, D), D_FF ** -0.5),
        'ple_norm': gain(ks[18], (L, D)),
        'ple_w_gate': nrm(ks[19], (L, D, D), D ** -0.5),
        'ple_w_proj': nrm(ks[20], (L, PLE_DIM, D), PLE_DIM ** -0.5),
        'final_norm': gain(ks[21], (D,)),
    }


def reference(x, p, ffn1_norm, ffn1_wg, ffn1_wu, ffn1_wd, mix_norm, w_in, rpb, conv_w, conv_b,
              attn_out_norm, conv_out_norm, w_out, ffn2_norm, ffn2_wg, ffn2_wu, ffn2_wd,
              ple_norm, ple_w_gate, ple_w_proj, final_norm):
    b, s, _ = x.shape
    splits = [D_ATTN, 2 * D_ATTN, 3 * D_ATTN, 3 * D_ATTN + D_CONV, 3 * D_ATTN + 2 * D_CONV]
    h = x
    for i in range(DEPTH):
        h = h + 0.5 * _swiglu(_rmsnorm(h, ffn1_norm[i]), ffn1_wg[i], ffn1_wu[i], ffn1_wd[i])

        a = _rmsnorm(h, mix_norm[i])
        z = a @ w_in[i]
        q, k, v, gate_b, gate_c, u = jnp.split(z, splits, axis=-1)
        q = q.reshape(b, s, N_HEADS, HEAD_DIM)
        k = k.reshape(b, s, N_HEADS, HEAD_DIM)
        v = v.reshape(b, s, N_HEADS, HEAD_DIM)
        y_attn = _neighbourhood_attention(q, k, v, rpb[i])
        y_conv = gate_b * _short_conv(gate_c * u, conv_w[i], conv_b[i])
        mixed = jnp.concatenate([
            _group_rmsnorm(y_attn, attn_out_norm[i], N_HEADS),
            _group_rmsnorm(y_conv, conv_out_norm[i], N_CONV_GROUPS)], axis=-1)
        h = h + mixed @ w_out[i]

        h = h + 0.5 * _swiglu(_rmsnorm(h, ffn2_norm[i]), ffn2_wg[i], ffn2_wu[i], ffn2_wd[i])

        g = jax.nn.sigmoid(_rmsnorm(h, ple_norm[i]) @ ple_w_gate[i])
        h = h + g * (p[i] @ ple_w_proj[i])
    return _rmsnorm(h, final_norm)
```

```python
import functools

import numpy as np
import jax
import jax.numpy as jnp
from jax import lax
from jax.experimental import pallas as pl
from jax.experimental.pallas import tpu as pltpu

D_MODEL = 2048
GRID_W = 64
PLE_DIM = 256
D_ATTN = D_MODEL // 2
D_CONV = D_MODEL - D_ATTN
N_HEADS = 8
HEAD_DIM = D_ATTN // N_HEADS
GROUP_DIM = 128
CONV_W = 3
WIN_ROWS = 8
WIN_COLS = 16
D_FF = 5632
RMS_EPS = 1e-6
NEG = -1e30

BF16 = jnp.bfloat16
F32 = jnp.float32

VMEM_LIMIT = 56 * 1024 * 1024
FFN_TM = 1024
FFN_TF = 512
PROJ_TM = 1024
PROJ_TN = 512
CONV_TM = 1024
CONV_TC = 256
HALO = 16
Q_ROWS = 8
K_ROWS = 16
K_CHUNK_ROWS = 4
OUT_TM = 512
PLE_TM = 512


def _rms(x, g):
    ms = jnp.mean(x * x, axis=-1, keepdims=True)
    return x * lax.rsqrt(ms + RMS_EPS) * g


def _params(sem):
    return pltpu.CompilerParams(dimension_semantics=sem, vmem_limit_bytes=VMEM_LIMIT)


def _ffn_kernel(x_ref, g_ref, wg_ref, wu_ref, wd_ref, o_ref, xn_ref):
    @pl.when(pl.program_id(1) == 0)
    def _():
        x = x_ref[...]
        xn_ref[...] = _rms(x, g_ref[...]).astype(BF16)
        o_ref[...] = x

    xn = xn_ref[...]
    gate = jnp.dot(xn, wg_ref[...], preferred_element_type=F32)
    up = jnp.dot(xn, wu_ref[...], preferred_element_type=F32)
    act = (jax.nn.silu(gate) * up * 0.5).astype(BF16)
    o_ref[...] += jnp.dot(act, wd_ref[...], preferred_element_type=F32)


def _ffn(x, g, wg, wu, wd):
    t, d = x.shape
    f = wg.shape[1]
    return pl.pallas_call(
        _ffn_kernel,
        grid=(t // FFN_TM, f // FFN_TF),
        in_specs=[
            pl.BlockSpec((FFN_TM, d), lambda i, j: (i, 0), pipeline_mode=pl.Buffered(1)),
            pl.BlockSpec((1, d), lambda i, j: (0, 0)),
            pl.BlockSpec((d, FFN_TF), lambda i, j: (0, j)),
            pl.BlockSpec((d, FFN_TF), lambda i, j: (0, j)),
            pl.BlockSpec((FFN_TF, d), lambda i, j: (j, 0)),
        ],
        out_specs=pl.BlockSpec((FFN_TM, d), lambda i, j: (i, 0)),
        out_shape=jax.ShapeDtypeStruct((t, d), F32),
        scratch_shapes=[pltpu.VMEM((FFN_TM, d), BF16)],
        compiler_params=_params(("parallel", "arbitrary")),
        name="ffn",
    )(x, g, wg, wu, wd)


def _qkv_kernel(x_ref, g_ref, w_ref, o_ref, xn_ref):
    @pl.when(pl.program_id(1) == 0)
    def _():
        xn_ref[...] = _rms(x_ref[...], g_ref[...]).astype(BF16)

    o_ref[...] = jnp.dot(xn_ref[...], w_ref[...], preferred_element_type=F32).astype(BF16)


def _qkv(x, g, w_in):
    t, d = x.shape
    n = 3 * D_ATTN
    return pl.pallas_call(
        _qkv_kernel,
        grid=(t // PROJ_TM, n // PROJ_TN),
        in_specs=[
            pl.BlockSpec((PROJ_TM, d), lambda i, j: (i, 0)),
            pl.BlockSpec((1, d), lambda i, j: (0, 0)),
            pl.BlockSpec((d, PROJ_TN), lambda i, j: (0, j)),
        ],
        out_specs=pl.BlockSpec((PROJ_TM, PROJ_TN), lambda i, j: (i, j)),
        out_shape=jax.ShapeDtypeStruct((t, n), BF16),
        scratch_shapes=[pltpu.VMEM((PROJ_TM, d), BF16)],
        compiler_params=_params(("parallel", "arbitrary")),
        name="qkv",
    )(x, g, w_in)


def _conv_kernel(x_ref, xp_ref, xnx_ref, g_ref, wb_ref, wc_ref, wu_ref, cw_ref, cb_ref,
                 gn_ref, o_ref, xn_ref, cu_ref, *, seq):
    tm = x_ref.shape[0]

    @pl.when(pl.program_id(1) == 0)
    def _():
        g = g_ref[...]
        xn_ref[0:HALO, :] = _rms(xp_ref[...], g).astype(BF16)
        xn_ref[HALO:HALO + tm, :] = _rms(x_ref[...], g).astype(BF16)
        xn_ref[HALO + tm:, :] = _rms(xnx_ref[...], g).astype(BF16)

    xn = xn_ref[...]
    gate_b = jnp.dot(xn_ref[HALO:HALO + tm, :], wb_ref[...], preferred_element_type=F32)
    gate_c = jnp.dot(xn, wc_ref[...], preferred_element_type=F32)
    u = jnp.dot(xn, wu_ref[...], preferred_element_type=F32)
    cu_ref[...] = gate_c * u

    pos = (lax.broadcasted_iota(jnp.int32, (tm, 1), 0) + pl.program_id(0) * tm) % seq
    prev = jnp.where(pos == 0, 0.0, cu_ref[HALO - 1:HALO - 1 + tm, :])
    cur = cu_ref[HALO:HALO + tm, :]
    nxt = jnp.where(pos == seq - 1, 0.0, cu_ref[HALO + 1:HALO + 1 + tm, :])
    cw = cw_ref[...]
    y = prev * cw[0:1, :]
    y = y + cur * cw[1:2, :]
    y = y + nxt * cw[2:3, :]
    y = gate_b * (y + cb_ref[...])
    gn = gn_ref[...]
    for c in range(0, y.shape[1], GROUP_DIM):
        sl = slice(c, c + GROUP_DIM)
        o_ref[:, sl] = _rms(y[:, sl], gn[:, sl]).astype(BF16)


def _conv(x, g, w_in, conv_w, conv_b, gn, seq):
    t, d = x.shape
    tm, tc = CONV_TM, CONV_TC
    hb = tm // HALO
    nhb = t // HALO
    col_b = 3 * D_ATTN // tc
    col_c = col_b + D_CONV // tc
    col_u = col_c + D_CONV // tc
    return pl.pallas_call(
        functools.partial(_conv_kernel, seq=seq),
        grid=(t // tm, D_CONV // tc),
        in_specs=[
            pl.BlockSpec((tm, d), lambda i, j: (i, 0)),
            pl.BlockSpec((HALO, d), lambda i, j: (jnp.maximum(i * hb - 1, 0), 0)),
            pl.BlockSpec((HALO, d), lambda i, j: (jnp.minimum((i + 1) * hb, nhb - 1), 0)),
            pl.BlockSpec((1, d), lambda i, j: (0, 0)),
            pl.BlockSpec((d, tc), lambda i, j: (0, col_b + j)),
            pl.BlockSpec((d, tc), lambda i, j: (0, col_c + j)),
            pl.BlockSpec((d, tc), lambda i, j: (0, col_u + j)),
            pl.BlockSpec((CONV_W, tc), lambda i, j: (0, j)),
            pl.BlockSpec((1, tc), lambda i, j: (0, j)),
            pl.BlockSpec((1, tc), lambda i, j: (0, j)),
        ],
        out_specs=pl.BlockSpec((tm, tc), lambda i, j: (i, j)),
        out_shape=jax.ShapeDtypeStruct((t, D_CONV), BF16),
        scratch_shapes=[pltpu.VMEM((tm + 2 * HALO, d), BF16),
                        pltpu.VMEM((tm + 2 * HALO, tc), F32)],
        compiler_params=_params(("parallel", "arbitrary")),
        name="conv",
    )(x, x, x, g, w_in, w_in, w_in, conv_w, conv_b, gn)


def _bias_tables(rpb, rows):
    c = np.arange(GRID_W)
    col_start = np.clip(c - WIN_COLS // 2, 0, GRID_W - WIN_COLS)
    in_win = (c[None, :] >= col_start[:, None]) & (c[None, :] < col_start[:, None] + WIN_COLS)
    rel_c = np.clip(c[None, :] - c[:, None], -(WIN_COLS - 1), WIN_COLS - 1) + WIN_COLS - 1
    tabs = []
    for r0 in (0, Q_ROWS, rows - Q_ROWS):
        k0 = min(max(r0 - WIN_ROWS // 2, 0), rows - K_ROWS)
        r = r0 + np.arange(Q_ROWS)
        row_start = np.clip(r - WIN_ROWS // 2, 0, rows - WIN_ROWS)
        kr = k0 + np.arange(K_ROWS)
        valid_r = (kr[None, :] >= row_start[:, None]) & (kr[None, :] < row_start[:, None] + WIN_ROWS)
        rel_r = np.clip(kr[None, :] - r[:, None], -(WIN_ROWS - 1), WIN_ROWS - 1) + WIN_ROWS - 1
        valid = valid_r[:, None, :, None] & in_win[None, :, None, :]
        ri = np.broadcast_to(rel_r[:, None, :, None], valid.shape)
        ci = np.broadcast_to(rel_c[None, :, None, :], valid.shape)
        tab = jnp.where(valid[None], rpb[:, ri, ci], NEG)
        tabs.append(tab.reshape(N_HEADS, Q_ROWS * GRID_W, K_ROWS * GRID_W))
    return jnp.stack(tabs)


def _attn_kernel(q_ref, k0, k1, k2, k3, v0, v1, v2, v3, bias_ref, gn_ref, o_ref):
    q = q_ref[...]
    s = jnp.concatenate(
        [lax.dot_general(q, k[...], (((1,), (1,)), ((), ())), preferred_element_type=F32)
         for k in (k0, k1, k2, k3)], axis=1)
    s = s * (HEAD_DIM ** -0.5) + bias_ref[...]
    m = jnp.max(s, axis=1, keepdims=True)
    p = jnp.exp(s - m)
    l = jnp.sum(p, axis=1, keepdims=True)
    pb = p.astype(BF16)
    kc = k0.shape[0]
    o = None
    for c, v in enumerate((v0, v1, v2, v3)):
        part = jnp.dot(pb[:, c * kc:(c + 1) * kc], v[...], preferred_element_type=F32)
        o = part if o is None else o + part
    o = o / l
    o_ref[...] = _rms(o, gn_ref[...]).astype(BF16)


def _attn(qkv, bias, gn, batch, seq):
    rows = seq // GRID_W
    tq = Q_ROWS * GRID_W
    kc = K_CHUNK_ROWS * GRID_W
    n_rb = rows // Q_ROWS
    n_kc = K_ROWS // K_CHUNK_ROWS
    chunks_per_seq = seq // kc

    def q_map(h, b, rb):
        return (b * n_rb + rb, h)

    def kv_map(col0, c):
        def f(h, b, rb):
            first = jnp.clip(rb * (Q_ROWS // K_CHUNK_ROWS) - (WIN_ROWS // 2) // K_CHUNK_ROWS,
                             0, chunks_per_seq - n_kc)
            return (b * chunks_per_seq + first + c, col0 + h)
        return f

    def bias_map(h, b, rb):
        pat = jnp.where(rb == 0, 0, jnp.where(rb == n_rb - 1, 2, 1))
        return (pat, h, 0, 0)

    k_specs = [pl.BlockSpec((kc, HEAD_DIM), kv_map(N_HEADS, c)) for c in range(n_kc)]
    v_specs = [pl.BlockSpec((kc, HEAD_DIM), kv_map(2 * N_HEADS, c)) for c in range(n_kc)]
    return pl.pallas_call(
        _attn_kernel,
        grid=(N_HEADS, batch, n_rb),
        in_specs=[pl.BlockSpec((tq, HEAD_DIM), q_map)] + k_specs + v_specs + [
            pl.BlockSpec((None, None, tq, K_ROWS * GRID_W), bias_map),
            pl.BlockSpec((1, HEAD_DIM), lambda h, b, rb: (0, h)),
        ],
        out_specs=pl.BlockSpec((tq, HEAD_DIM), q_map),
        out_shape=jax.ShapeDtypeStruct((batch * seq, D_ATTN), BF16),
        compiler_params=_params(("parallel", "parallel", "arbitrary")),
        name="attn",
    )(qkv, *([qkv] * (2 * n_kc)), bias, gn)


def _outproj_kernel(x_ref, a_ref, c_ref, wa_ref, wc_ref, o_ref):
    acc = jnp.dot(a_ref[...], wa_ref[...], preferred_element_type=F32)
    acc = acc + jnp.dot(c_ref[...], wc_ref[...], preferred_element_type=F32)
    o_ref[...] = x_ref[...] + acc


def _outproj(x, a, c, w_out):
    t, d = x.shape
    tm = OUT_TM
    return pl.pallas_call(
        _outproj_kernel,
        grid=(t // tm,),
        in_specs=[
            pl.BlockSpec((tm, d), lambda i: (i, 0)),
            pl.BlockSpec((tm, D_ATTN), lambda i: (i, 0)),
            pl.BlockSpec((tm, D_CONV), lambda i: (i, 0)),
            pl.BlockSpec((D_ATTN, d), lambda i: (0, 0), pipeline_mode=pl.Buffered(1)),
            pl.BlockSpec((D_CONV, d), lambda i: (1, 0), pipeline_mode=pl.Buffered(1)),
        ],
        out_specs=pl.BlockSpec((tm, d), lambda i: (i, 0)),
        out_shape=jax.ShapeDtypeStruct((t, d), F32),
        compiler_params=_params(("parallel",)),
        name="outproj",
    )(x, a, c, w_out, w_out)


def _ple_kernel(x_ref, p_ref, g_ref, wg_ref, wp_ref, gf_ref, o_ref):
    x = x_ref[...]
    xn = _rms(x, g_ref[...]).astype(BF16)
    gate = jax.nn.sigmoid(jnp.dot(xn, wg_ref[...], preferred_element_type=F32))
    proj = jnp.dot(p_ref[...].astype(BF16), wp_ref[...], preferred_element_type=F32)
    o_ref[...] = _rms(x + gate * proj, gf_ref[...])


def _ple(x, p, g, w_gate, w_proj, gf):
    t, d = x.shape
    tm = PLE_TM
    return pl.pallas_call(
        _ple_kernel,
        grid=(t // tm,),
        in_specs=[
            pl.BlockSpec((tm, d), lambda i: (i, 0)),
            pl.BlockSpec((tm, PLE_DIM), lambda i: (i, 0)),
            pl.BlockSpec((1, d), lambda i: (0, 0)),
            pl.BlockSpec((d, d), lambda i: (0, 0), pipeline_mode=pl.Buffered(1)),
            pl.BlockSpec((PLE_DIM, d), lambda i: (0, 0), pipeline_mode=pl.Buffered(1)),
            pl.BlockSpec((1, d), lambda i: (0, 0)),
        ],
        out_specs=pl.BlockSpec((tm, d), lambda i: (i, 0)),
        out_shape=jax.ShapeDtypeStruct((t, d), F32),
        compiler_params=_params(("parallel",)),
        name="ple",
    )(x, p, g, w_gate, w_proj, gf)


def kernel(x, p, ffn1_norm, ffn1_wg, ffn1_wu, ffn1_wd, mix_norm, w_in, rpb, conv_w, conv_b,
           attn_out_norm, conv_out_norm, w_out, ffn2_norm, ffn2_wg, ffn2_wu, ffn2_wd,
           ple_norm, ple_w_gate, ple_w_proj, final_norm):
    b, s, d = x.shape
    assert ffn1_wg.shape[0] == 1
    assert d == D_MODEL and s % (Q_ROWS * GRID_W) == 0 and s % CONV_TM == 0
    t = b * s
    h = x.reshape(t, d)
    row = lambda v: v.reshape(1, -1)
    for i in range(1):
        h = _ffn(h, row(ffn1_norm[i]), ffn1_wg[i].astype(BF16), ffn1_wu[i].astype(BF16),
                 ffn1_wd[i].astype(BF16))
        w_in_b = w_in[i].astype(BF16)
        qkv = _qkv(h, row(mix_norm[i]), w_in_b)
        y_conv = _conv(h, row(mix_norm[i]), w_in_b, conv_w[i], row(conv_b[i]),
                       row(conv_out_norm[i]), s)
        y_attn = _attn(qkv, _bias_tables(rpb[i], s // GRID_W), row(attn_out_norm[i]), b, s)
        h = _outproj(h, y_attn, y_conv, w_out[i].astype(BF16))
        h = _ffn(h, row(ffn2_norm[i]), ffn2_wg[i].astype(BF16), ffn2_wu[i].astype(BF16),
                 ffn2_wd[i].astype(BF16))
        h = _ple(h, p[i].reshape(t, PLE_DIM), row(ple_norm[i]), ple_w_gate[i].astype(BF16),
                 ple_w_proj[i].astype(BF16), row(final_norm))
    return h.reshape(b, s, d)
```

```python
import functools

import numpy as np
import jax
import jax.numpy as jnp
from jax import lax
from jax.experimental import pallas as pl
from jax.experimental.pallas import tpu as pltpu

D_MODEL = 2048
GRID_W = 64
PLE_DIM = 256
D_ATTN = D_MODEL // 2
D_CONV = D_MODEL - D_ATTN
N_HEADS = 8
HEAD_DIM = D_ATTN // N_HEADS
GROUP_DIM = 128
CONV_W = 3
WIN_ROWS = 8
WIN_COLS = 16
D_FF = 5632
RMS_EPS = 1e-6
NEG = -1e30

BF16 = jnp.bfloat16
F32 = jnp.float32

VMEM_LIMIT = 56 * 1024 * 1024
FFN_TM = 1024
FFN_TF = 512
PROJ_TM = 1024
PROJ_TN = 512
CONV_TM = 1024
CONV_TC = 256
HALO = 16
Q_ROWS = 8
K_ROWS = 16
K_CHUNK_ROWS = 4
OUT_TM = 512
PLE_TM = 512


def _rms(x, g):
    ms = jnp.mean(x * x, axis=-1, keepdims=True)
    return x * lax.rsqrt(ms + RMS_EPS) * g


def _params(sem):
    return pltpu.CompilerParams(dimension_semantics=sem, vmem_limit_bytes=VMEM_LIMIT)


def _ffn_kernel(x_ref, g_ref, wg_ref, wu_ref, wd_ref, o_ref, xn_ref):
    @pl.when(pl.program_id(1) == 0)
    def _():
        x = x_ref[...]
        xn_ref[...] = _rms(x, g_ref[...]).astype(BF16)
        o_ref[...] = x

    xn = xn_ref[...]
    gate = jnp.dot(xn, wg_ref[...], preferred_element_type=F32)
    up = jnp.dot(xn, wu_ref[...], preferred_element_type=F32)
    act = (jax.nn.silu(gate) * up * 0.5).astype(BF16)
    o_ref[...] += jnp.dot(act, wd_ref[...], preferred_element_type=F32)


def _ffn(x, g, wg, wu, wd):
    t, d = x.shape
    f = wg.shape[1]
    return pl.pallas_call(
        _ffn_kernel,
        grid=(t // FFN_TM, f // FFN_TF),
        in_specs=[
            pl.BlockSpec((FFN_TM, d), lambda i, j: (i, 0), pipeline_mode=pl.Buffered(1)),
            pl.BlockSpec((1, d), lambda i, j: (0, 0)),
            pl.BlockSpec((d, FFN_TF), lambda i, j: (0, j)),
            pl.BlockSpec((d, FFN_TF), lambda i, j: (0, j)),
            pl.BlockSpec((FFN_TF, d), lambda i, j: (j, 0)),
        ],
        out_specs=pl.BlockSpec((FFN_TM, d), lambda i, j: (i, 0)),
        out_shape=jax.ShapeDtypeStruct((t, d), F32),
        scratch_shapes=[pltpu.VMEM((FFN_TM, d), BF16)],
        compiler_params=_params(("parallel", "arbitrary")),
        name="ffn",
    )(x, g, wg, wu, wd)


def _qkv_kernel(x_ref, g_ref, w_ref, o_ref, xn_ref):
    @pl.when(pl.program_id(1) == 0)
    def _():
        xn_ref[...] = _rms(x_ref[...], g_ref[...]).astype(BF16)

    o_ref[...] = jnp.dot(xn_ref[...], w_ref[...], preferred_element_type=F32).astype(BF16)


def _qkv(x, g, w_in):
    t, d = x.shape
    n = 3 * D_ATTN
    return pl.pallas_call(
        _qkv_kernel,
        grid=(t // PROJ_TM, n // PROJ_TN),
        in_specs=[
            pl.BlockSpec((PROJ_TM, d), lambda i, j: (i, 0)),
            pl.BlockSpec((1, d), lambda i, j: (0, 0)),
            pl.BlockSpec((d, PROJ_TN), lambda i, j: (0, j)),
        ],
        out_specs=pl.BlockSpec((PROJ_TM, PROJ_TN), lambda i, j: (i, j)),
        out_shape=jax.ShapeDtypeStruct((t, n), BF16),
        scratch_shapes=[pltpu.VMEM((PROJ_TM, d), BF16)],
        compiler_params=_params(("parallel", "arbitrary")),
        name="qkv",
    )(x, g, w_in)


def _conv_kernel(x_ref, xp_ref, xnx_ref, g_ref, wb_ref, wc_ref, wu_ref, cw_ref, cb_ref,
                 gn_ref, o_ref, xn_ref, cu_ref, *, seq):
    tm = x_ref.shape[0]

    @pl.when(pl.program_id(1) == 0)
    def _():
        g = g_ref[...]
        xn_ref[0:HALO, :] = _rms(xp_ref[...], g).astype(BF16)
        xn_ref[HALO:HALO + tm, :] = _rms(x_ref[...], g).astype(BF16)
        xn_ref[HALO + tm:, :] = _rms(xnx_ref[...], g).astype(BF16)

    xn = xn_ref[...]
    gate_b = jnp.dot(xn_ref[HALO:HALO + tm, :], wb_ref[...], preferred_element_type=F32)
    gate_c = jnp.dot(xn, wc_ref[...], preferred_element_type=F32)
    u = jnp.dot(xn, wu_ref[...], preferred_element_type=F32)
    cu_ref[...] = gate_c * u

    pos = (lax.broadcasted_iota(jnp.int32, (tm, 1), 0) + pl.program_id(0) * tm) % seq
    prev = jnp.where(pos == 0, 0.0, cu_ref[HALO - 1:HALO - 1 + tm, :])
    cur = cu_ref[HALO:HALO + tm, :]
    nxt = jnp.where(pos == seq - 1, 0.0, cu_ref[HALO + 1:HALO + 1 + tm, :])
    cw = cw_ref[...]
    y = prev * cw[0:1, :]
    y = y + cur * cw[1:2, :]
    y = y + nxt * cw[2:3, :]
    y = gate_b * (y + cb_ref[...])
    gn = gn_ref[...]
    for c in range(0, y.shape[1], GROUP_DIM):
        sl = slice(c, c + GROUP_DIM)
        o_ref[:, sl] = _rms(y[:, sl], gn[:, sl]).astype(BF16)


def _conv(x, g, w_in, conv_w, conv_b, gn, seq):
    t, d = x.shape
    tm, tc = CONV_TM, CONV_TC
    hb = tm // HALO
    nhb = t // HALO
    col_b = 3 * D_ATTN // tc
    col_c = col_b + D_CONV // tc
    col_u = col_c + D_CONV // tc
    return pl.pallas_call(
        functools.partial(_conv_kernel, seq=seq),
        grid=(t // tm, D_CONV // tc),
        in_specs=[
            pl.BlockSpec((tm, d), lambda i, j: (i, 0)),
            pl.BlockSpec((HALO, d), lambda i, j: (jnp.maximum(i * hb - 1, 0), 0)),
            pl.BlockSpec((HALO, d), lambda i, j: (jnp.minimum((i + 1) * hb, nhb - 1), 0)),
            pl.BlockSpec((1, d), lambda i, j: (0, 0)),
            pl.BlockSpec((d, tc), lambda i, j: (0, col_b + j)),
            pl.BlockSpec((d, tc), lambda i, j: (0, col_c + j)),
            pl.BlockSpec((d, tc), lambda i, j: (0, col_u + j)),
            pl.BlockSpec((CONV_W, tc), lambda i, j: (0, j)),
            pl.BlockSpec((1, tc), lambda i, j: (0, j)),
            pl.BlockSpec((1, tc), lambda i, j: (0, j)),
        ],
        out_specs=pl.BlockSpec((tm, tc), lambda i, j: (i, j)),
        out_shape=jax.ShapeDtypeStruct((t, D_CONV), BF16),
        scratch_shapes=[pltpu.VMEM((tm + 2 * HALO, d), BF16),
                        pltpu.VMEM((tm + 2 * HALO, tc), F32)],
        compiler_params=_params(("parallel", "arbitrary")),
        name="conv",
    )(x, x, x, g, w_in, w_in, w_in, conv_w, conv_b, gn)


def _bias_tables(rpb, rows):
    nh, nr, nc = rpb.shape
    w = GRID_W
    c = np.arange(w)
    col_start = np.clip(c - WIN_COLS // 2, 0, w - WIN_COLS)
    in_win = (c[None, :] >= col_start[:, None]) & (c[None, :] < col_start[:, None] + WIN_COLS)
    period = 2 * w
    left = w - WIN_COLS
    u = jnp.pad(rpb, ((0, 0), (0, 0), (left, period - nc - left)))
    toep = jnp.tile(u, (1, 1, w))[..., :w * (period - 1)].reshape(nh, nr, w, period - 1)[..., w - 1:]
    toep = jnp.where(in_win[None, None], toep, NEG)
    neg = jnp.full((nh, w, w), NEG, F32)
    tabs = []
    for r0 in (0, Q_ROWS, rows - Q_ROWS):
        k0 = min(max(r0 - WIN_ROWS // 2, 0), rows - K_ROWS)
        strips = []
        for i in range(Q_ROWS):
            r = r0 + i
            row_start = min(max(r - WIN_ROWS // 2, 0), rows - WIN_ROWS)
            blocks = []
            for kr in range(k0, k0 + K_ROWS):
                inside = row_start <= kr < row_start + WIN_ROWS
                blocks.append(toep[:, kr - r + WIN_ROWS - 1] if inside else neg)
            strips.append(jnp.concatenate(blocks, axis=-1))
        tabs.append(jnp.concatenate(strips, axis=1))
    return jnp.stack(tabs)


def _attn_kernel(q_ref, k0, k1, k2, k3, v0, v1, v2, v3, bias_ref, gn_ref, o_ref):
    q = q_ref[...]
    s = jnp.concatenate(
        [lax.dot_general(q, k[...], (((1,), (1,)), ((), ())), preferred_element_type=F32)
         for k in (k0, k1, k2, k3)], axis=1)
    s = s * (HEAD_DIM ** -0.5) + bias_ref[...]
    m = jnp.max(s, axis=1, keepdims=True)
    p = jnp.exp(s - m)
    l = jnp.sum(p, axis=1, keepdims=True)
    pb = p.astype(BF16)
    kc = k0.shape[0]
    o = None
    for c, v in enumerate((v0, v1, v2, v3)):
        part = jnp.dot(pb[:, c * kc:(c + 1) * kc], v[...], preferred_element_type=F32)
        o = part if o is None else o + part
    o = o / l
    o_ref[...] = _rms(o, gn_ref[...]).astype(BF16)


def _attn(qkv, bias, gn, batch, seq):
    rows = seq // GRID_W
    tq = Q_ROWS * GRID_W
    kc = K_CHUNK_ROWS * GRID_W
    n_rb = rows // Q_ROWS
    n_kc = K_ROWS // K_CHUNK_ROWS
    chunks_per_seq = seq // kc

    def q_map(h, b, rb):
        return (b * n_rb + rb, h)

    def kv_map(col0, c):
        def f(h, b, rb):
            first = jnp.clip(rb * (Q_ROWS // K_CHUNK_ROWS) - (WIN_ROWS // 2) // K_CHUNK_ROWS,
                             0, chunks_per_seq - n_kc)
            return (b * chunks_per_seq + first + c, col0 + h)
        return f

    def bias_map(h, b, rb):
        pat = jnp.where(rb == 0, 0, jnp.where(rb == n_rb - 1, 2, 1))
        return (pat, h, 0, 0)

    k_specs = [pl.BlockSpec((kc, HEAD_DIM), kv_map(N_HEADS, c)) for c in range(n_kc)]
    v_specs = [pl.BlockSpec((kc, HEAD_DIM), kv_map(2 * N_HEADS, c)) for c in range(n_kc)]
    return pl.pallas_call(
        _attn_kernel,
        grid=(N_HEADS, batch, n_rb),
        in_specs=[pl.BlockSpec((tq, HEAD_DIM), q_map)] + k_specs + v_specs + [
            pl.BlockSpec((None, None, tq, K_ROWS * GRID_W), bias_map),
            pl.BlockSpec((1, HEAD_DIM), lambda h, b, rb: (0, h)),
        ],
        out_specs=pl.BlockSpec((tq, HEAD_DIM), q_map),
        out_shape=jax.ShapeDtypeStruct((batch * seq, D_ATTN), BF16),
        compiler_params=_params(("parallel", "parallel", "arbitrary")),
        name="attn",
    )(qkv, *([qkv] * (2 * n_kc)), bias, gn)


def _outproj_kernel(x_ref, a_ref, c_ref, wa_ref, wc_ref, o_ref):
    acc = jnp.dot(a_ref[...], wa_ref[...], preferred_element_type=F32)
    acc = acc + jnp.dot(c_ref[...], wc_ref[...], preferred_element_type=F32)
    o_ref[...] = x_ref[...] + acc


def _outproj(x, a, c, w_out):
    t, d = x.shape
    tm = OUT_TM
    return pl.pallas_call(
        _outproj_kernel,
        grid=(t // tm,),
        in_specs=[
            pl.BlockSpec((tm, d), lambda i: (i, 0)),
            pl.BlockSpec((tm, D_ATTN), lambda i: (i, 0)),
            pl.BlockSpec((tm, D_CONV), lambda i: (i, 0)),
            pl.BlockSpec((D_ATTN, d), lambda i: (0, 0), pipeline_mode=pl.Buffered(1)),
            pl.BlockSpec((D_CONV, d), lambda i: (1, 0), pipeline_mode=pl.Buffered(1)),
        ],
        out_specs=pl.BlockSpec((tm, d), lambda i: (i, 0)),
        out_shape=jax.ShapeDtypeStruct((t, d), F32),
        compiler_params=_params(("parallel",)),
        name="outproj",
    )(x, a, c, w_out, w_out)


def _ple_kernel(x_ref, p_ref, g_ref, wg_ref, wp_ref, gf_ref, o_ref):
    x = x_ref[...]
    xn = _rms(x, g_ref[...]).astype(BF16)
    gate = jax.nn.sigmoid(jnp.dot(xn, wg_ref[...], preferred_element_type=F32))
    proj = jnp.dot(p_ref[...].astype(BF16), wp_ref[...], preferred_element_type=F32)
    o_ref[...] = _rms(x + gate * proj, gf_ref[...])


def _ple(x, p, g, w_gate, w_proj, gf):
    t, d = x.shape
    tm = PLE_TM
    return pl.pallas_call(
        _ple_kernel,
        grid=(t // tm,),
        in_specs=[
            pl.BlockSpec((tm, d), lambda i: (i, 0)),
            pl.BlockSpec((tm, PLE_DIM), lambda i: (i, 0)),
            pl.BlockSpec((1, d), lambda i: (0, 0)),
            pl.BlockSpec((d, d), lambda i: (0, 0), pipeline_mode=pl.Buffered(1)),
            pl.BlockSpec((PLE_DIM, d), lambda i: (0, 0), pipeline_mode=pl.Buffered(1)),
            pl.BlockSpec((1, d), lambda i: (0, 0)),
        ],
        out_specs=pl.BlockSpec((tm, d), lambda i: (i, 0)),
        out_shape=jax.ShapeDtypeStruct((t, d), F32),
        compiler_params=_params(("parallel",)),
        name="ple",
    )(x, p, g, w_gate, w_proj, gf)


def kernel(x, p, ffn1_norm, ffn1_wg, ffn1_wu, ffn1_wd, mix_norm, w_in, rpb, conv_w, conv_b,
           attn_out_norm, conv_out_norm, w_out, ffn2_norm, ffn2_wg, ffn2_wu, ffn2_wd,
           ple_norm, ple_w_gate, ple_w_proj, final_norm):
    b, s, d = x.shape
    assert ffn1_wg.shape[0] == 1
    assert d == D_MODEL and s % (Q_ROWS * GRID_W) == 0 and s % CONV_TM == 0
    t = b * s
    h = x.reshape(t, d)
    row = lambda v: v.reshape(1, -1)
    for i in range(1):
        h = _ffn(h, row(ffn1_norm[i]), ffn1_wg[i].astype(BF16), ffn1_wu[i].astype(BF16),
                 ffn1_wd[i].astype(BF16))
        w_in_b = w_in[i].astype(BF16)
        qkv = _qkv(h, row(mix_norm[i]), w_in_b)
        y_conv = _conv(h, row(mix_norm[i]), w_in_b, conv_w[i], row(conv_b[i]),
                       row(conv_out_norm[i]), s)
        y_attn = _attn(qkv, _bias_tables(rpb[i], s // GRID_W), row(attn_out_norm[i]), b, s)
        h = _outproj(h, y_attn, y_conv, w_out[i].astype(BF16))
        h = _ffn(h, row(ffn2_norm[i]), ffn2_wg[i].astype(BF16), ffn2_wu[i].astype(BF16),
                 ffn2_wd[i].astype(BF16))
        h = _ple(h, p[i].reshape(t, PLE_DIM), row(ple_norm[i]), ple_w_gate[i].astype(BF16),
                 ple_w_proj[i].astype(BF16), row(final_norm))
    return h.reshape(b, s, d)
```

```python
import functools

import numpy as np
import jax
import jax.numpy as jnp
from jax import lax
from jax.experimental import pallas as pl
from jax.experimental.pallas import tpu as pltpu

D_MODEL = 2048
GRID_W = 64
PLE_DIM = 256
D_ATTN = D_MODEL // 2
D_CONV = D_MODEL - D_ATTN
N_HEADS = 8
HEAD_DIM = D_ATTN // N_HEADS
GROUP_DIM = 128
CONV_W = 3
WIN_ROWS = 8
WIN_COLS = 16
D_FF = 5632
RMS_EPS = 1e-6
NEG = -1e30

BF16 = jnp.bfloat16
F32 = jnp.float32

VMEM_LIMIT = 56 * 1024 * 1024
FFN_TM = 1024
FFN_TF = 512
PROJ_TM = 1024
PROJ_TN = 512
CONV_TM = 1024
CONV_TC = 256
HALO = 16
Q_ROWS = 8
K_ROWS = 16
K_CHUNK_ROWS = 4
OUT_TM = 512
PLE_TM = 512


def _rms(x, g):
    ms = jnp.mean(x * x, axis=-1, keepdims=True)
    return x * lax.rsqrt(ms + RMS_EPS) * g


def _params(sem):
    return pltpu.CompilerParams(dimension_semantics=sem, vmem_limit_bytes=VMEM_LIMIT)


def _ffn_kernel(x_ref, g_ref, wg_ref, wu_ref, wd_ref, o_ref, xn_ref):
    @pl.when(pl.program_id(1) == 0)
    def _():
        x = x_ref[...]
        xn_ref[...] = _rms(x, g_ref[...]).astype(BF16)
        o_ref[...] = x

    xn = xn_ref[...]
    gate = jnp.dot(xn, wg_ref[...], preferred_element_type=F32)
    up = jnp.dot(xn, wu_ref[...], preferred_element_type=F32)
    act = (jax.nn.silu(gate) * up * 0.5).astype(BF16)
    o_ref[...] += jnp.dot(act, wd_ref[...], preferred_element_type=F32)


def _ffn(x, g, wg, wu, wd):
    t, d = x.shape
    f = wg.shape[1]
    return pl.pallas_call(
        _ffn_kernel,
        grid=(t // FFN_TM, f // FFN_TF),
        in_specs=[
            pl.BlockSpec((FFN_TM, d), lambda i, j: (i, 0), pipeline_mode=pl.Buffered(1)),
            pl.BlockSpec((1, d), lambda i, j: (0, 0)),
            pl.BlockSpec((d, FFN_TF), lambda i, j: (0, j)),
            pl.BlockSpec((d, FFN_TF), lambda i, j: (0, j)),
            pl.BlockSpec((FFN_TF, d), lambda i, j: (j, 0)),
        ],
        out_specs=pl.BlockSpec((FFN_TM, d), lambda i, j: (i, 0)),
        out_shape=jax.ShapeDtypeStruct((t, d), F32),
        scratch_shapes=[pltpu.VMEM((FFN_TM, d), BF16)],
        compiler_params=_params(("parallel", "arbitrary")),
        name="ffn",
    )(x, g, wg, wu, wd)


def _qkv_kernel(x_ref, g_ref, w_ref, o_ref, xn_ref):
    @pl.when(pl.program_id(1) == 0)
    def _():
        xn_ref[...] = _rms(x_ref[...], g_ref[...]).astype(BF16)

    o_ref[...] = jnp.dot(xn_ref[...], w_ref[...], preferred_element_type=F32).astype(BF16)


def _qkv(x, g, w_in):
    t, d = x.shape
    n = 3 * D_ATTN
    return pl.pallas_call(
        _qkv_kernel,
        grid=(t // PROJ_TM, n // PROJ_TN),
        in_specs=[
            pl.BlockSpec((PROJ_TM, d), lambda i, j: (i, 0)),
            pl.BlockSpec((1, d), lambda i, j: (0, 0)),
            pl.BlockSpec((d, PROJ_TN), lambda i, j: (0, j)),
        ],
        out_specs=pl.BlockSpec((PROJ_TM, PROJ_TN), lambda i, j: (i, j)),
        out_shape=jax.ShapeDtypeStruct((t, n), BF16),
        scratch_shapes=[pltpu.VMEM((PROJ_TM, d), BF16)],
        compiler_params=_params(("parallel", "arbitrary")),
        name="qkv",
    )(x, g, w_in)


def _conv_kernel(x_ref, xp_ref, xnx_ref, g_ref, wb_ref, wc_ref, wu_ref, cw_ref, cb_ref,
                 gn_ref, o_ref, xn_ref, cu_ref, *, seq):
    tm = x_ref.shape[0]

    @pl.when(pl.program_id(1) == 0)
    def _():
        g = g_ref[...]
        xn_ref[0:HALO, :] = _rms(xp_ref[...], g).astype(BF16)
        xn_ref[HALO:HALO + tm, :] = _rms(x_ref[...], g).astype(BF16)
        xn_ref[HALO + tm:, :] = _rms(xnx_ref[...], g).astype(BF16)

    xn = xn_ref[...]
    gate_b = jnp.dot(xn_ref[HALO:HALO + tm, :], wb_ref[...], preferred_element_type=F32)
    gate_c = jnp.dot(xn, wc_ref[...], preferred_element_type=F32)
    u = jnp.dot(xn, wu_ref[...], preferred_element_type=F32)
    cu_ref[...] = gate_c * u

    pos = (lax.broadcasted_iota(jnp.int32, (tm, 1), 0) + pl.program_id(0) * tm) % seq
    prev = jnp.where(pos == 0, 0.0, cu_ref[HALO - 1:HALO - 1 + tm, :])
    cur = cu_ref[HALO:HALO + tm, :]
    nxt = jnp.where(pos == seq - 1, 0.0, cu_ref[HALO + 1:HALO + 1 + tm, :])
    cw = cw_ref[...]
    y = prev * cw[0:1, :]
    y = y + cur * cw[1:2, :]
    y = y + nxt * cw[2:3, :]
    y = gate_b * (y + cb_ref[...])
    gn = gn_ref[...]
    for c in range(0, y.shape[1], GROUP_DIM):
        sl = slice(c, c + GROUP_DIM)
        o_ref[:, sl] = _rms(y[:, sl], gn[:, sl]).astype(BF16)


def _conv(x, g, w_in, conv_w, conv_b, gn, seq):
    t, d = x.shape
    tm, tc = CONV_TM, CONV_TC
    hb = tm // HALO
    nhb = t // HALO
    col_b = 3 * D_ATTN // tc
    col_c = col_b + D_CONV // tc
    col_u = col_c + D_CONV // tc
    return pl.pallas_call(
        functools.partial(_conv_kernel, seq=seq),
        grid=(t // tm, D_CONV // tc),
        in_specs=[
            pl.BlockSpec((tm, d), lambda i, j: (i, 0)),
            pl.BlockSpec((HALO, d), lambda i, j: (jnp.maximum(i * hb - 1, 0), 0)),
            pl.BlockSpec((HALO, d), lambda i, j: (jnp.minimum((i + 1) * hb, nhb - 1), 0)),
            pl.BlockSpec((1, d), lambda i, j: (0, 0)),
            pl.BlockSpec((d, tc), lambda i, j: (0, col_b + j)),
            pl.BlockSpec((d, tc), lambda i, j: (0, col_c + j)),
            pl.BlockSpec((d, tc), lambda i, j: (0, col_u + j)),
            pl.BlockSpec((CONV_W, tc), lambda i, j: (0, j)),
            pl.BlockSpec((1, tc), lambda i, j: (0, j)),
            pl.BlockSpec((1, tc), lambda i, j: (0, j)),
        ],
        out_specs=pl.BlockSpec((tm, tc), lambda i, j: (i, j)),
        out_shape=jax.ShapeDtypeStruct((t, D_CONV), BF16),
        scratch_shapes=[pltpu.VMEM((tm + 2 * HALO, d), BF16),
                        pltpu.VMEM((tm + 2 * HALO, tc), F32)],
        compiler_params=_params(("parallel", "arbitrary")),
        name="conv",
    )(x, x, x, g, w_in, w_in, w_in, conv_w, conv_b, gn)


N_REL = 2 * WIN_ROWS - 1
PAIR_FULL = 0
PAIR_LEFT_NEG = N_REL - 1
PAIR_RIGHT_NEG = 2 * N_REL - 1
PAIR_NEG = 3 * N_REL - 1
N_PAIR_BLOCKS = 3 * N_REL
N_ROW_PATTERNS = 3
N_PAIRS = K_ROWS // 2


def _pair_blocks(rpb):
    nh, nr, nc = rpb.shape
    w = GRID_W
    c = np.arange(w)
    col_start = np.clip(c - WIN_COLS // 2, 0, w - WIN_COLS)
    in_win = (c[None, :] >= col_start[:, None]) & (c[None, :] < col_start[:, None] + WIN_COLS)
    period = 2 * w
    left = w - WIN_COLS
    u = jnp.pad(rpb, ((0, 0), (0, 0), (left, period - nc - left)))
    toep = jnp.tile(u, (1, 1, w))[..., :w * (period - 1)].reshape(nh, nr, w, period - 1)[..., w - 1:]
    toep = jnp.where(in_win[None, None], toep, NEG)
    neg = jnp.full((nh, nr, w, w), NEG, F32)
    return jnp.concatenate([
        jnp.concatenate([toep[:, :-1], toep[:, 1:]], axis=-1),
        jnp.concatenate([neg, toep], axis=-1),
        jnp.concatenate([toep, neg], axis=-1),
        jnp.concatenate([neg[:, :1], neg[:, :1]], axis=-1)], axis=1)


def _pair_index_table(rows):
    tab = np.zeros((N_ROW_PATTERNS, Q_ROWS, N_PAIRS), np.int32)
    for pat, r0 in enumerate((0, Q_ROWS, rows - Q_ROWS)):
        k0 = min(max(r0 - WIN_ROWS // 2, 0), rows - K_ROWS)
        for i in range(Q_ROWS):
            r = r0 + i
            row_start = min(max(r - WIN_ROWS // 2, 0), rows - WIN_ROWS)
            for m in range(N_PAIRS):
                kr = k0 + 2 * m
                in0 = row_start <= kr < row_start + WIN_ROWS
                in1 = row_start <= kr + 1 < row_start + WIN_ROWS
                a = kr - r + WIN_ROWS - 1
                if in0 and in1:
                    tab[pat, i, m] = PAIR_FULL + a
                elif in1:
                    tab[pat, i, m] = PAIR_LEFT_NEG + a + 1
                elif in0:
                    tab[pat, i, m] = PAIR_RIGHT_NEG + a
                else:
                    tab[pat, i, m] = PAIR_NEG
    return jnp.asarray(tab.reshape(-1))


def _attn_kernel(tab_ref, q_ref, k0, k1, k2, k3, v0, v1, v2, v3, pb_ref, gn_ref, o_ref):
    rb = pl.program_id(1)
    pat = jnp.where(rb == 0, 0, jnp.where(rb == pl.num_programs(1) - 1, 2, 1))
    idx = [[tab_ref[(pat * Q_ROWS + i) * N_PAIRS + m] for m in range(N_PAIRS)]
           for i in range(Q_ROWS)]
    kc = k0.shape[0]
    for h in range(N_HEADS):
        hs = slice(h * HEAD_DIM, (h + 1) * HEAD_DIM)
        q = q_ref[:, hs]
        s = jnp.concatenate(
            [lax.dot_general(q, k[:, hs], (((1,), (1,)), ((), ())), preferred_element_type=F32)
             for k in (k0, k1, k2, k3)], axis=1)
        bias = jnp.concatenate(
            [jnp.concatenate([pb_ref[h, idx[i][m]] for m in range(N_PAIRS)], axis=1)
             for i in range(Q_ROWS)], axis=0)
        s = s * (HEAD_DIM ** -0.5) + bias
        m = jnp.max(s, axis=1, keepdims=True)
        p = jnp.exp(s - m)
        l = jnp.sum(p, axis=1, keepdims=True)
        pb = p.astype(BF16)
        o = None
        for c, v in enumerate((v0, v1, v2, v3)):
            part = jnp.dot(pb[:, c * kc:(c + 1) * kc], v[:, hs], preferred_element_type=F32)
            o = part if o is None else o + part
        o = o / l
        o_ref[:, hs] = _rms(o, gn_ref[:, hs]).astype(BF16)


def _attn(qkv, pair_blocks, gn, batch, seq):
    rows = seq // GRID_W
    tq = Q_ROWS * GRID_W
    kc = K_CHUNK_ROWS * GRID_W
    n_rb = rows // Q_ROWS
    n_kc = K_ROWS // K_CHUNK_ROWS
    chunks_per_seq = seq // kc

    def q_map(b, rb, tab):
        return (b * n_rb + rb, 0)

    def kv_map(col, c):
        def f(b, rb, tab):
            first = jnp.clip(rb * (Q_ROWS // K_CHUNK_ROWS) - (WIN_ROWS // 2) // K_CHUNK_ROWS,
                             0, chunks_per_seq - n_kc)
            return (b * chunks_per_seq + first + c, col)
        return f

    k_specs = [pl.BlockSpec((kc, D_ATTN), kv_map(1, c)) for c in range(n_kc)]
    v_specs = [pl.BlockSpec((kc, D_ATTN), kv_map(2, c)) for c in range(n_kc)]
    grid_spec = pltpu.PrefetchScalarGridSpec(
        num_scalar_prefetch=1,
        grid=(batch, n_rb),
        in_specs=[pl.BlockSpec((tq, D_ATTN), q_map)] + k_specs + v_specs + [
            pl.BlockSpec(pair_blocks.shape, lambda b, rb, tab: (0, 0, 0, 0),
                         pipeline_mode=pl.Buffered(1)),
            pl.BlockSpec((1, D_ATTN), lambda b, rb, tab: (0, 0)),
        ],
        out_specs=pl.BlockSpec((tq, D_ATTN), q_map),
    )
    return pl.pallas_call(
        _attn_kernel,
        grid_spec=grid_spec,
        out_shape=jax.ShapeDtypeStruct((batch * seq, D_ATTN), BF16),
        compiler_params=_params(("parallel", "arbitrary")),
        name="attn",
    )(_pair_index_table(rows), qkv, *([qkv] * (2 * n_kc)), pair_blocks, gn)


def _outproj_kernel(x_ref, a_ref, c_ref, wa_ref, wc_ref, o_ref):
    acc = jnp.dot(a_ref[...], wa_ref[...], preferred_element_type=F32)
    acc = acc + jnp.dot(c_ref[...], wc_ref[...], preferred_element_type=F32)
    o_ref[...] = x_ref[...] + acc


def _outproj(x, a, c, w_out):
    t, d = x.shape
    tm = OUT_TM
    return pl.pallas_call(
        _outproj_kernel,
        grid=(t // tm,),
        in_specs=[
            pl.BlockSpec((tm, d), lambda i: (i, 0)),
            pl.BlockSpec((tm, D_ATTN), lambda i: (i, 0)),
            pl.BlockSpec((tm, D_CONV), lambda i: (i, 0)),
            pl.BlockSpec((D_ATTN, d), lambda i: (0, 0), pipeline_mode=pl.Buffered(1)),
            pl.BlockSpec((D_CONV, d), lambda i: (1, 0), pipeline_mode=pl.Buffered(1)),
        ],
        out_specs=pl.BlockSpec((tm, d), lambda i: (i, 0)),
        out_shape=jax.ShapeDtypeStruct((t, d), F32),
        compiler_params=_params(("parallel",)),
        name="outproj",
    )(x, a, c, w_out, w_out)


def _ple_kernel(x_ref, p_ref, g_ref, wg_ref, wp_ref, gf_ref, o_ref):
    x = x_ref[...]
    xn = _rms(x, g_ref[...]).astype(BF16)
    gate = jax.nn.sigmoid(jnp.dot(xn, wg_ref[...], preferred_element_type=F32))
    proj = jnp.dot(p_ref[...].astype(BF16), wp_ref[...], preferred_element_type=F32)
    o_ref[...] = _rms(x + gate * proj, gf_ref[...])


def _ple(x, p, g, w_gate, w_proj, gf):
    t, d = x.shape
    tm = PLE_TM
    return pl.pallas_call(
        _ple_kernel,
        grid=(t // tm,),
        in_specs=[
            pl.BlockSpec((tm, d), lambda i: (i, 0)),
            pl.BlockSpec((tm, PLE_DIM), lambda i: (i, 0)),
            pl.BlockSpec((1, d), lambda i: (0, 0)),
            pl.BlockSpec((d, d), lambda i: (0, 0), pipeline_mode=pl.Buffered(1)),
            pl.BlockSpec((PLE_DIM, d), lambda i: (0, 0), pipeline_mode=pl.Buffered(1)),
            pl.BlockSpec((1, d), lambda i: (0, 0)),
        ],
        out_specs=pl.BlockSpec((tm, d), lambda i: (i, 0)),
        out_shape=jax.ShapeDtypeStruct((t, d), F32),
        compiler_params=_params(("parallel",)),
        name="ple",
    )(x, p, g, w_gate, w_proj, gf)


def kernel(x, p, ffn1_norm, ffn1_wg, ffn1_wu, ffn1_wd, mix_norm, w_in, rpb, conv_w, conv_b,
           attn_out_norm, conv_out_norm, w_out, ffn2_norm, ffn2_wg, ffn2_wu, ffn2_wd,
           ple_norm, ple_w_gate, ple_w_proj, final_norm):
    b, s, d = x.shape
    assert ffn1_wg.shape[0] == 1
    assert d == D_MODEL and s % (Q_ROWS * GRID_W) == 0 and s % CONV_TM == 0
    t = b * s
    h = x.reshape(t, d)
    row = lambda v: v.reshape(1, -1)
    for i in range(1):
        h = _ffn(h, row(ffn1_norm[i]), ffn1_wg[i].astype(BF16), ffn1_wu[i].astype(BF16),
                 ffn1_wd[i].astype(BF16))
        w_in_b = w_in[i].astype(BF16)
        qkv = _qkv(h, row(mix_norm[i]), w_in_b)
        y_conv = _conv(h, row(mix_norm[i]), w_in_b, conv_w[i], row(conv_b[i]),
                       row(conv_out_norm[i]), s)
        y_attn = _attn(qkv, _pair_blocks(rpb[i]), row(attn_out_norm[i]), b, s)
        h = _outproj(h, y_attn, y_conv, w_out[i].astype(BF16))
        h = _ffn(h, row(ffn2_norm[i]), ffn2_wg[i].astype(BF16), ffn2_wu[i].astype(BF16),
                 ffn2_wd[i].astype(BF16))
        h = _ple(h, p[i].reshape(t, PLE_DIM), row(ple_norm[i]), ple_w_gate[i].astype(BF16),
                 ple_w_proj[i].astype(BF16), row(final_norm))
    return h.reshape(b, s, d)
```

```python
import functools

import numpy as np
import jax
import jax.numpy as jnp
from jax import lax
from jax.experimental import pallas as pl
from jax.experimental.pallas import tpu as pltpu

D_MODEL = 2048
GRID_W = 64
PLE_DIM = 256
D_ATTN = D_MODEL // 2
D_CONV = D_MODEL - D_ATTN
N_HEADS = 8
HEAD_DIM = D_ATTN // N_HEADS
GROUP_DIM = 128
CONV_W = 3
WIN_ROWS = 8
WIN_COLS = 16
D_FF = 5632
RMS_EPS = 1e-6
NEG = -1e30

BF16 = jnp.bfloat16
F32 = jnp.float32

VMEM_LIMIT = 56 * 1024 * 1024
FFN_TM = 1024
FFN_TF = 512
PROJ_TM = 1024
PROJ_TN = 512
CONV_TC = 256
HALO = 16
Q_ROWS = 8
K_ROWS = 16
K_CHUNK_ROWS = 4
OUT_TM = 512
PLE_TM = 512


def _rms(x, g):
    ms = jnp.mean(x * x, axis=-1, keepdims=True)
    return x * lax.rsqrt(ms + RMS_EPS) * g


def _params(sem):
    return pltpu.CompilerParams(dimension_semantics=sem, vmem_limit_bytes=VMEM_LIMIT)


def _ffn_kernel(x_ref, g_ref, wg_ref, wu_ref, wd_ref, o_ref, xn_ref):
    @pl.when(pl.program_id(1) == 0)
    def _():
        x = x_ref[...]
        xn_ref[...] = _rms(x, g_ref[...]).astype(BF16)
        o_ref[...] = x

    xn = xn_ref[...]
    gate = jnp.dot(xn, wg_ref[...], preferred_element_type=F32)
    up = jnp.dot(xn, wu_ref[...], preferred_element_type=F32)
    act = (jax.nn.silu(gate) * up * 0.5).astype(BF16)
    o_ref[...] += jnp.dot(act, wd_ref[...], preferred_element_type=F32)


def _ffn(x, g, wg, wu, wd):
    t, d = x.shape
    f = wg.shape[1]
    return pl.pallas_call(
        _ffn_kernel,
        grid=(t // FFN_TM, f // FFN_TF),
        in_specs=[
            pl.BlockSpec((FFN_TM, d), lambda i, j: (i, 0), pipeline_mode=pl.Buffered(1)),
            pl.BlockSpec((1, d), lambda i, j: (0, 0)),
            pl.BlockSpec((d, FFN_TF), lambda i, j: (0, j)),
            pl.BlockSpec((d, FFN_TF), lambda i, j: (0, j)),
            pl.BlockSpec((FFN_TF, d), lambda i, j: (j, 0)),
        ],
        out_specs=pl.BlockSpec((FFN_TM, d), lambda i, j: (i, 0)),
        out_shape=jax.ShapeDtypeStruct((t, d), F32),
        scratch_shapes=[pltpu.VMEM((FFN_TM, d), BF16)],
        compiler_params=_params(("parallel", "arbitrary")),
        name="ffn",
    )(x, g, wg, wu, wd)


N_QKV_STEPS = 3 * D_ATTN // PROJ_TN
N_CONV_STEPS = D_CONV // CONV_TC


def _inproj_kernel(x_ref, xp_ref, xnx_ref, g_ref, wq_ref, wb_ref, wc_ref, wu_ref, cw_ref,
                   cb_ref, gn_ref, qkv_ref, conv_ref, xn_ref, cu_ref, *, seq):
    tm = x_ref.shape[0]
    j = pl.program_id(1)

    @pl.when(j == 0)
    def _():
        g = g_ref[...]
        xn_ref[0:HALO, :] = _rms(xp_ref[...], g).astype(BF16)
        xn_ref[HALO:HALO + tm, :] = _rms(x_ref[...], g).astype(BF16)
        xn_ref[HALO + tm:, :] = _rms(xnx_ref[...], g).astype(BF16)

    @pl.when(j < N_QKV_STEPS)
    def _():
        qkv_ref[...] = jnp.dot(xn_ref[HALO:HALO + tm, :], wq_ref[...].astype(BF16),
                               preferred_element_type=F32).astype(BF16)

    @pl.when(j >= N_QKV_STEPS)
    def _():
        xn = xn_ref[...]
        gate_b = jnp.dot(xn_ref[HALO:HALO + tm, :], wb_ref[...].astype(BF16),
                         preferred_element_type=F32)
        gate_c = jnp.dot(xn, wc_ref[...].astype(BF16), preferred_element_type=F32)
        u = jnp.dot(xn, wu_ref[...].astype(BF16), preferred_element_type=F32)
        cu_ref[...] = gate_c * u

        pos = (lax.broadcasted_iota(jnp.int32, (tm, 1), 0) + pl.program_id(0) * tm) % seq
        prev = jnp.where(pos == 0, 0.0, cu_ref[HALO - 1:HALO - 1 + tm, :])
        cur = cu_ref[HALO:HALO + tm, :]
        nxt = jnp.where(pos == seq - 1, 0.0, cu_ref[HALO + 1:HALO + 1 + tm, :])
        cw = cw_ref[...]
        y = prev * cw[0:1, :]
        y = y + cur * cw[1:2, :]
        y = y + nxt * cw[2:3, :]
        y = gate_b * (y + cb_ref[...])
        gn = gn_ref[...]
        for c in range(0, y.shape[1], GROUP_DIM):
            sl = slice(c, c + GROUP_DIM)
            conv_ref[:, sl] = _rms(y[:, sl], gn[:, sl]).astype(BF16)


def _inproj(x, g, w_in, conv_w, conv_b, gn, seq):
    t, d = x.shape
    tm, tn, tc = PROJ_TM, PROJ_TN, CONV_TC
    hb = tm // HALO
    nhb = t // HALO
    col_b = 3 * D_ATTN // tc
    col_c = col_b + D_CONV // tc
    col_u = col_c + D_CONV // tc
    qstep = lambda j: jnp.minimum(j, N_QKV_STEPS - 1)
    cstep = lambda j: jnp.maximum(j - N_QKV_STEPS, 0)
    return pl.pallas_call(
        functools.partial(_inproj_kernel, seq=seq),
        grid=(t // tm, N_QKV_STEPS + N_CONV_STEPS),
        in_specs=[
            pl.BlockSpec((tm, d), lambda i, j: (i, 0)),
            pl.BlockSpec((HALO, d), lambda i, j: (jnp.maximum(i * hb - 1, 0), 0)),
            pl.BlockSpec((HALO, d), lambda i, j: (jnp.minimum((i + 1) * hb, nhb - 1), 0)),
            pl.BlockSpec((1, d), lambda i, j: (0, 0)),
            pl.BlockSpec((d, tn), lambda i, j: (0, qstep(j))),
            pl.BlockSpec((d, tc), lambda i, j: (0, col_b + cstep(j))),
            pl.BlockSpec((d, tc), lambda i, j: (0, col_c + cstep(j))),
            pl.BlockSpec((d, tc), lambda i, j: (0, col_u + cstep(j))),
            pl.BlockSpec((CONV_W, tc), lambda i, j: (0, cstep(j))),
            pl.BlockSpec((1, tc), lambda i, j: (0, cstep(j))),
            pl.BlockSpec((1, tc), lambda i, j: (0, cstep(j))),
        ],
        out_specs=[pl.BlockSpec((tm, tn), lambda i, j: (i, qstep(j))),
                   pl.BlockSpec((tm, tc), lambda i, j: (i, cstep(j)))],
        out_shape=[jax.ShapeDtypeStruct((t, 3 * D_ATTN), BF16),
                   jax.ShapeDtypeStruct((t, D_CONV), BF16)],
        scratch_shapes=[pltpu.VMEM((tm + 2 * HALO, d), BF16),
                        pltpu.VMEM((tm + 2 * HALO, tc), F32)],
        compiler_params=_params(("parallel", "arbitrary")),
        name="inproj",
    )(x, x, x, g, w_in, w_in, w_in, w_in, conv_w, conv_b, gn)


N_REL = 2 * WIN_ROWS - 1
PAIR_FULL = 0
PAIR_LEFT_NEG = N_REL - 1
PAIR_RIGHT_NEG = 2 * N_REL - 1
PAIR_NEG = 3 * N_REL - 1
N_PAIR_BLOCKS = 3 * N_REL
N_ROW_PATTERNS = 3
N_PAIRS = K_ROWS // 2


def _pair_blocks(rpb):
    nh, nr, nc = rpb.shape
    w = GRID_W
    c = np.arange(w)
    col_start = np.clip(c - WIN_COLS // 2, 0, w - WIN_COLS)
    in_win = (c[None, :] >= col_start[:, None]) & (c[None, :] < col_start[:, None] + WIN_COLS)
    period = 2 * w
    left = w - WIN_COLS
    u = jnp.pad(rpb, ((0, 0), (0, 0), (left, period - nc - left)))
    toep = jnp.tile(u, (1, 1, w))[..., :w * (period - 1)].reshape(nh, nr, w, period - 1)[..., w - 1:]
    toep = jnp.where(in_win[None, None], toep, NEG)
    neg = jnp.full((nh, nr, w, w), NEG, F32)
    return jnp.concatenate([
        jnp.concatenate([toep[:, :-1], toep[:, 1:]], axis=-1),
        jnp.concatenate([neg, toep], axis=-1),
        jnp.concatenate([toep, neg], axis=-1),
        jnp.concatenate([neg[:, :1], neg[:, :1]], axis=-1)], axis=1)


def _pair_index_table(rows):
    tab = np.zeros((N_ROW_PATTERNS, Q_ROWS, N_PAIRS), np.int32)
    for pat, r0 in enumerate((0, Q_ROWS, rows - Q_ROWS)):
        k0 = min(max(r0 - WIN_ROWS // 2, 0), rows - K_ROWS)
        for i in range(Q_ROWS):
            r = r0 + i
            row_start = min(max(r - WIN_ROWS // 2, 0), rows - WIN_ROWS)
            for m in range(N_PAIRS):
                kr = k0 + 2 * m
                in0 = row_start <= kr < row_start + WIN_ROWS
                in1 = row_start <= kr + 1 < row_start + WIN_ROWS
                a = kr - r + WIN_ROWS - 1
                if in0 and in1:
                    tab[pat, i, m] = PAIR_FULL + a
                elif in1:
                    tab[pat, i, m] = PAIR_LEFT_NEG + a + 1
                elif in0:
                    tab[pat, i, m] = PAIR_RIGHT_NEG + a
                else:
                    tab[pat, i, m] = PAIR_NEG
    return jnp.asarray(tab.reshape(-1))


def _attn_kernel(tab_ref, q_ref, k0, k1, k2, k3, v0, v1, v2, v3, pb_ref, gn_ref, o_ref):
    rb = pl.program_id(1)
    pat = jnp.where(rb == 0, 0, jnp.where(rb == pl.num_programs(1) - 1, 2, 1))
    idx = [[tab_ref[(pat * Q_ROWS + i) * N_PAIRS + m] for m in range(N_PAIRS)]
           for i in range(Q_ROWS)]
    kc = k0.shape[0]
    for h in range(N_HEADS):
        hs = slice(h * HEAD_DIM, (h + 1) * HEAD_DIM)
        q = q_ref[:, hs]
        s = jnp.concatenate(
            [lax.dot_general(q, k[:, hs], (((1,), (1,)), ((), ())), preferred_element_type=F32)
             for k in (k0, k1, k2, k3)], axis=1)
        bias = jnp.concatenate(
            [jnp.concatenate([pb_ref[h, idx[i][m]] for m in range(N_PAIRS)], axis=1)
             for i in range(Q_ROWS)], axis=0)
        s = s * (HEAD_DIM ** -0.5) + bias
        m = jnp.max(s, axis=1, keepdims=True)
        p = jnp.exp(s - m)
        l = jnp.sum(p, axis=1, keepdims=True)
        pb = p.astype(BF16)
        o = None
        for c, v in enumerate((v0, v1, v2, v3)):
            part = jnp.dot(pb[:, c * kc:(c + 1) * kc], v[:, hs], preferred_element_type=F32)
            o = part if o is None else o + part
        o = o / l
        o_ref[:, hs] = _rms(o, gn_ref[:, hs]).astype(BF16)


def _attn(qkv, pair_blocks, gn, batch, seq):
    rows = seq // GRID_W
    tq = Q_ROWS * GRID_W
    kc = K_CHUNK_ROWS * GRID_W
    n_rb = rows // Q_ROWS
    n_kc = K_ROWS // K_CHUNK_ROWS
    chunks_per_seq = seq // kc

    def q_map(b, rb, tab):
        return (b * n_rb + rb, 0)

    def kv_map(col, c):
        def f(b, rb, tab):
            first = jnp.clip(rb * (Q_ROWS // K_CHUNK_ROWS) - (WIN_ROWS // 2) // K_CHUNK_ROWS,
                             0, chunks_per_seq - n_kc)
            return (b * chunks_per_seq + first + c, col)
        return f

    k_specs = [pl.BlockSpec((kc, D_ATTN), kv_map(1, c)) for c in range(n_kc)]
    v_specs = [pl.BlockSpec((kc, D_ATTN), kv_map(2, c)) for c in range(n_kc)]
    grid_spec = pltpu.PrefetchScalarGridSpec(
        num_scalar_prefetch=1,
        grid=(batch, n_rb),
        in_specs=[pl.BlockSpec((tq, D_ATTN), q_map)] + k_specs + v_specs + [
            pl.BlockSpec(pair_blocks.shape, lambda b, rb, tab: (0, 0, 0, 0),
                         pipeline_mode=pl.Buffered(1)),
            pl.BlockSpec((1, D_ATTN), lambda b, rb, tab: (0, 0)),
        ],
        out_specs=pl.BlockSpec((tq, D_ATTN), q_map),
    )
    return pl.pallas_call(
        _attn_kernel,
        grid_spec=grid_spec,
        out_shape=jax.ShapeDtypeStruct((batch * seq, D_ATTN), BF16),
        compiler_params=_params(("parallel", "arbitrary")),
        name="attn",
    )(_pair_index_table(rows), qkv, *([qkv] * (2 * n_kc)), pair_blocks, gn)


def _outproj_kernel(x_ref, a_ref, c_ref, wa_ref, wc_ref, o_ref):
    acc = jnp.dot(a_ref[...], wa_ref[...].astype(BF16), preferred_element_type=F32)
    acc = acc + jnp.dot(c_ref[...], wc_ref[...].astype(BF16), preferred_element_type=F32)
    o_ref[...] = x_ref[...] + acc


def _outproj(x, a, c, w_out):
    t, d = x.shape
    tm = OUT_TM
    return pl.pallas_call(
        _outproj_kernel,
        grid=(t // tm,),
        in_specs=[
            pl.BlockSpec((tm, d), lambda i: (i, 0)),
            pl.BlockSpec((tm, D_ATTN), lambda i: (i, 0)),
            pl.BlockSpec((tm, D_CONV), lambda i: (i, 0)),
            pl.BlockSpec((D_ATTN, d), lambda i: (0, 0), pipeline_mode=pl.Buffered(1)),
            pl.BlockSpec((D_CONV, d), lambda i: (1, 0), pipeline_mode=pl.Buffered(1)),
        ],
        out_specs=pl.BlockSpec((tm, d), lambda i: (i, 0)),
        out_shape=jax.ShapeDtypeStruct((t, d), F32),
        compiler_params=_params(("parallel",)),
        name="outproj",
    )(x, a, c, w_out, w_out)


def _ple_kernel(x_ref, p_ref, g_ref, wg_ref, wp_ref, gf_ref, o_ref):
    x = x_ref[...]
    xn = _rms(x, g_ref[...]).astype(BF16)
    gate = jax.nn.sigmoid(jnp.dot(xn, wg_ref[...].astype(BF16), preferred_element_type=F32))
    proj = jnp.dot(p_ref[...].astype(BF16), wp_ref[...].astype(BF16),
                   preferred_element_type=F32)
    o_ref[...] = _rms(x + gate * proj, gf_ref[...])


def _ple(x, p, g, w_gate, w_proj, gf):
    t, d = x.shape
    tm = PLE_TM
    return pl.pallas_call(
        _ple_kernel,
        grid=(t // tm,),
        in_specs=[
            pl.BlockSpec((tm, d), lambda i: (i, 0)),
            pl.BlockSpec((tm, PLE_DIM), lambda i: (i, 0)),
            pl.BlockSpec((1, d), lambda i: (0, 0)),
            pl.BlockSpec((d, d), lambda i: (0, 0), pipeline_mode=pl.Buffered(1)),
            pl.BlockSpec((PLE_DIM, d), lambda i: (0, 0), pipeline_mode=pl.Buffered(1)),
            pl.BlockSpec((1, d), lambda i: (0, 0)),
        ],
        out_specs=pl.BlockSpec((tm, d), lambda i: (i, 0)),
        out_shape=jax.ShapeDtypeStruct((t, d), F32),
        compiler_params=_params(("parallel",)),
        name="ple",
    )(x, p, g, w_gate, w_proj, gf)


def kernel(x, p, ffn1_norm, ffn1_wg, ffn1_wu, ffn1_wd, mix_norm, w_in, rpb, conv_w, conv_b,
           attn_out_norm, conv_out_norm, w_out, ffn2_norm, ffn2_wg, ffn2_wu, ffn2_wd,
           ple_norm, ple_w_gate, ple_w_proj, final_norm):
    b, s, d = x.shape
    assert ffn1_wg.shape[0] == 1
    assert d == D_MODEL and s % (Q_ROWS * GRID_W) == 0 and s % PROJ_TM == 0
    t = b * s
    h = x.reshape(t, d)
    row = lambda v: v.reshape(1, -1)
    for i in range(1):
        h = _ffn(h, row(ffn1_norm[i]), ffn1_wg[i].astype(BF16), ffn1_wu[i].astype(BF16),
                 ffn1_wd[i].astype(BF16))
        qkv, y_conv = _inproj(h, row(mix_norm[i]), w_in[i], conv_w[i], row(conv_b[i]),
                              row(conv_out_norm[i]), s)
        y_attn = _attn(qkv, _pair_blocks(rpb[i]), row(attn_out_norm[i]), b, s)
        h = _outproj(h, y_attn, y_conv, w_out[i])
        h = _ffn(h, row(ffn2_norm[i]), ffn2_wg[i].astype(BF16), ffn2_wu[i].astype(BF16),
                 ffn2_wd[i].astype(BF16))
        h = _ple(h, p[i].reshape(t, PLE_DIM), row(ple_norm[i]), ple_w_gate[i], ple_w_proj[i],
                 row(final_norm))
    return h.reshape(b, s, d)
```

```python
import functools

import numpy as np
import jax
import jax.numpy as jnp
from jax import lax
from jax.experimental import pallas as pl
from jax.experimental.pallas import tpu as pltpu

D_MODEL = 2048
GRID_W = 64
PLE_DIM = 256
D_ATTN = D_MODEL // 2
D_CONV = D_MODEL - D_ATTN
N_HEADS = 8
HEAD_DIM = D_ATTN // N_HEADS
GROUP_DIM = 128
CONV_W = 3
WIN_ROWS = 8
WIN_COLS = 16
D_FF = 5632
RMS_EPS = 1e-6
NEG = -1e30

BF16 = jnp.bfloat16
F32 = jnp.float32

VMEM_LIMIT = 56 * 1024 * 1024
FFN_TM = 1024
FFN_TF = 256
PROJ_TM = 1024
PROJ_TN = 512
CONV_TC = 256
HALO = 16
Q_ROWS = 8
K_ROWS = 16
K_CHUNK_ROWS = 4
OUT_TM = 512
PLE_TM = 512


def _rms(x, g):
    ms = jnp.mean(x * x, axis=-1, keepdims=True)
    return x * lax.rsqrt(ms + RMS_EPS) * g


def _params(sem):
    return pltpu.CompilerParams(dimension_semantics=sem, vmem_limit_bytes=VMEM_LIMIT)


def _ffn_kernel(x_ref, g_ref, wg_ref, wu_ref, wd_ref, o_ref, xn_ref):
    @pl.when(pl.program_id(1) == 0)
    def _():
        x = x_ref[...]
        xn_ref[...] = _rms(x, g_ref[...]).astype(BF16)
        o_ref[...] = x

    xn = xn_ref[...]
    gate = jnp.dot(xn, wg_ref[...].astype(BF16), preferred_element_type=F32)
    up = jnp.dot(xn, wu_ref[...].astype(BF16), preferred_element_type=F32)
    act = (jax.nn.silu(gate) * up * 0.5).astype(BF16)
    o_ref[...] += jnp.dot(act, wd_ref[...].astype(BF16), preferred_element_type=F32)


def _ffn(x, g, wg, wu, wd):
    t, d = x.shape
    f = wg.shape[1]
    return pl.pallas_call(
        _ffn_kernel,
        grid=(t // FFN_TM, f // FFN_TF),
        in_specs=[
            pl.BlockSpec((FFN_TM, d), lambda i, j: (i, 0), pipeline_mode=pl.Buffered(1)),
            pl.BlockSpec((1, d), lambda i, j: (0, 0)),
            pl.BlockSpec((d, FFN_TF), lambda i, j: (0, j)),
            pl.BlockSpec((d, FFN_TF), lambda i, j: (0, j)),
            pl.BlockSpec((FFN_TF, d), lambda i, j: (j, 0)),
        ],
        out_specs=pl.BlockSpec((FFN_TM, d), lambda i, j: (i, 0)),
        out_shape=jax.ShapeDtypeStruct((t, d), F32),
        scratch_shapes=[pltpu.VMEM((FFN_TM, d), BF16)],
        compiler_params=_params(("parallel", "arbitrary")),
        name="ffn",
    )(x, g, wg, wu, wd)


N_QKV_STEPS = 3 * D_ATTN // PROJ_TN
N_CONV_STEPS = D_CONV // CONV_TC


def _inproj_kernel(x_ref, xp_ref, xnx_ref, g_ref, wq_ref, wb_ref, wc_ref, wu_ref, cw_ref,
                   cb_ref, gn_ref, qkv_ref, conv_ref, xn_ref, cu_ref, *, seq):
    tm = x_ref.shape[0]
    j = pl.program_id(1)

    @pl.when(j == 0)
    def _():
        g = g_ref[...]
        xn_ref[0:HALO, :] = _rms(xp_ref[...], g).astype(BF16)
        xn_ref[HALO:HALO + tm, :] = _rms(x_ref[...], g).astype(BF16)
        xn_ref[HALO + tm:, :] = _rms(xnx_ref[...], g).astype(BF16)

    @pl.when(j < N_QKV_STEPS)
    def _():
        qkv_ref[...] = jnp.dot(xn_ref[HALO:HALO + tm, :], wq_ref[...].astype(BF16),
                               preferred_element_type=F32).astype(BF16)

    @pl.when(j >= N_QKV_STEPS)
    def _():
        xn = xn_ref[...]
        gate_b = jnp.dot(xn_ref[HALO:HALO + tm, :], wb_ref[...].astype(BF16),
                         preferred_element_type=F32)
        gate_c = jnp.dot(xn, wc_ref[...].astype(BF16), preferred_element_type=F32)
        u = jnp.dot(xn, wu_ref[...].astype(BF16), preferred_element_type=F32)
        cu_ref[...] = gate_c * u

        pos = (lax.broadcasted_iota(jnp.int32, (tm, 1), 0) + pl.program_id(0) * tm) % seq
        prev = jnp.where(pos == 0, 0.0, cu_ref[HALO - 1:HALO - 1 + tm, :])
        cur = cu_ref[HALO:HALO + tm, :]
        nxt = jnp.where(pos == seq - 1, 0.0, cu_ref[HALO + 1:HALO + 1 + tm, :])
        cw = cw_ref[...]
        y = prev * cw[0:1, :]
        y = y + cur * cw[1:2, :]
        y = y + nxt * cw[2:3, :]
        y = gate_b * (y + cb_ref[...])
        gn = gn_ref[...]
        for c in range(0, y.shape[1], GROUP_DIM):
            sl = slice(c, c + GROUP_DIM)
            conv_ref[:, sl] = _rms(y[:, sl], gn[:, sl]).astype(BF16)


def _inproj(x, g, w_in, conv_w, conv_b, gn, seq):
    t, d = x.shape
    tm, tn, tc = PROJ_TM, PROJ_TN, CONV_TC
    hb = tm // HALO
    nhb = t // HALO
    col_b = 3 * D_ATTN // tc
    col_c = col_b + D_CONV // tc
    col_u = col_c + D_CONV // tc
    qstep = lambda j: jnp.minimum(j, N_QKV_STEPS - 1)
    cstep = lambda j: jnp.maximum(j - N_QKV_STEPS, 0)
    return pl.pallas_call(
        functools.partial(_inproj_kernel, seq=seq),
        grid=(t // tm, N_QKV_STEPS + N_CONV_STEPS),
        in_specs=[
            pl.BlockSpec((tm, d), lambda i, j: (i, 0)),
            pl.BlockSpec((HALO, d), lambda i, j: (jnp.maximum(i * hb - 1, 0), 0)),
            pl.BlockSpec((HALO, d), lambda i, j: (jnp.minimum((i + 1) * hb, nhb - 1), 0)),
            pl.BlockSpec((1, d), lambda i, j: (0, 0)),
            pl.BlockSpec((d, tn), lambda i, j: (0, qstep(j))),
            pl.BlockSpec((d, tc), lambda i, j: (0, col_b + cstep(j))),
            pl.BlockSpec((d, tc), lambda i, j: (0, col_c + cstep(j))),
            pl.BlockSpec((d, tc), lambda i, j: (0, col_u + cstep(j))),
            pl.BlockSpec((CONV_W, tc), lambda i, j: (0, cstep(j))),
            pl.BlockSpec((1, tc), lambda i, j: (0, cstep(j))),
            pl.BlockSpec((1, tc), lambda i, j: (0, cstep(j))),
        ],
        out_specs=[pl.BlockSpec((tm, tn), lambda i, j: (i, qstep(j))),
                   pl.BlockSpec((tm, tc), lambda i, j: (i, cstep(j)))],
        out_shape=[jax.ShapeDtypeStruct((t, 3 * D_ATTN), BF16),
                   jax.ShapeDtypeStruct((t, D_CONV), BF16)],
        scratch_shapes=[pltpu.VMEM((tm + 2 * HALO, d), BF16),
                        pltpu.VMEM((tm + 2 * HALO, tc), F32)],
        compiler_params=_params(("parallel", "arbitrary")),
        name="inproj",
    )(x, x, x, g, w_in, w_in, w_in, w_in, conv_w, conv_b, gn)


N_REL = 2 * WIN_ROWS - 1
PAIR_FULL = 0
PAIR_LEFT_NEG = N_REL - 1
PAIR_RIGHT_NEG = 2 * N_REL - 1
PAIR_NEG = 3 * N_REL - 1
N_PAIR_BLOCKS = 3 * N_REL
N_ROW_PATTERNS = 3
N_PAIRS = K_ROWS // 2


def _pair_blocks(rpb):
    nh, nr, nc = rpb.shape
    w = GRID_W
    c = np.arange(w)
    col_start = np.clip(c - WIN_COLS // 2, 0, w - WIN_COLS)
    in_win = (c[None, :] >= col_start[:, None]) & (c[None, :] < col_start[:, None] + WIN_COLS)
    period = 2 * w
    left = w - WIN_COLS
    u = jnp.pad(rpb, ((0, 0), (0, 0), (left, period - nc - left)))
    toep = jnp.tile(u, (1, 1, w))[..., :w * (period - 1)].reshape(nh, nr, w, period - 1)[..., w - 1:]
    toep = jnp.where(in_win[None, None], toep, NEG)
    neg = jnp.full((nh, nr, w, w), NEG, F32)
    return jnp.concatenate([
        jnp.concatenate([toep[:, :-1], toep[:, 1:]], axis=-1),
        jnp.concatenate([neg, toep], axis=-1),
        jnp.concatenate([toep, neg], axis=-1),
        jnp.concatenate([neg[:, :1], neg[:, :1]], axis=-1)], axis=1)


def _pair_index_table(rows):
    tab = np.zeros((N_ROW_PATTERNS, Q_ROWS, N_PAIRS), np.int32)
    for pat, r0 in enumerate((0, Q_ROWS, rows - Q_ROWS)):
        k0 = min(max(r0 - WIN_ROWS // 2, 0), rows - K_ROWS)
        for i in range(Q_ROWS):
            r = r0 + i
            row_start = min(max(r - WIN_ROWS // 2, 0), rows - WIN_ROWS)
            for m in range(N_PAIRS):
                kr = k0 + 2 * m
                in0 = row_start <= kr < row_start + WIN_ROWS
                in1 = row_start <= kr + 1 < row_start + WIN_ROWS
                a = kr - r + WIN_ROWS - 1
                if in0 and in1:
                    tab[pat, i, m] = PAIR_FULL + a
                elif in1:
                    tab[pat, i, m] = PAIR_LEFT_NEG + a + 1
                elif in0:
                    tab[pat, i, m] = PAIR_RIGHT_NEG + a
                else:
                    tab[pat, i, m] = PAIR_NEG
    return jnp.asarray(tab.reshape(-1))


def _attn_kernel(tab_ref, q_ref, k0, k1, k2, k3, v0, v1, v2, v3, pb_ref, gn_ref, o_ref):
    rb = pl.program_id(1)
    pat = jnp.where(rb == 0, 0, jnp.where(rb == pl.num_programs(1) - 1, 2, 1))
    idx = [[tab_ref[(pat * Q_ROWS + i) * N_PAIRS + m] for m in range(N_PAIRS)]
           for i in range(Q_ROWS)]
    kc = k0.shape[0]
    for h in range(N_HEADS):
        hs = slice(h * HEAD_DIM, (h + 1) * HEAD_DIM)
        q = q_ref[:, hs]
        s = jnp.concatenate(
            [lax.dot_general(q, k[:, hs], (((1,), (1,)), ((), ())), preferred_element_type=F32)
             for k in (k0, k1, k2, k3)], axis=1)
        bias = jnp.concatenate(
            [jnp.concatenate([pb_ref[h, idx[i][m]] for m in range(N_PAIRS)], axis=1)
             for i in range(Q_ROWS)], axis=0)
        s = s * (HEAD_DIM ** -0.5) + bias
        m = jnp.max(s, axis=1, keepdims=True)
        p = jnp.exp(s - m)
        l = jnp.sum(p, axis=1, keepdims=True)
        pb = p.astype(BF16)
        o = None
        for c, v in enumerate((v0, v1, v2, v3)):
            part = jnp.dot(pb[:, c * kc:(c + 1) * kc], v[:, hs], preferred_element_type=F32)
            o = part if o is None else o + part
        o = o / l
        o_ref[:, hs] = _rms(o, gn_ref[:, hs]).astype(BF16)


def _attn(qkv, pair_blocks, gn, batch, seq):
    rows = seq // GRID_W
    tq = Q_ROWS * GRID_W
    kc = K_CHUNK_ROWS * GRID_W
    n_rb = rows // Q_ROWS
    n_kc = K_ROWS // K_CHUNK_ROWS
    chunks_per_seq = seq // kc

    def q_map(b, rb, tab):
        return (b * n_rb + rb, 0)

    def kv_map(col, c):
        def f(b, rb, tab):
            first = jnp.clip(rb * (Q_ROWS // K_CHUNK_ROWS) - (WIN_ROWS // 2) // K_CHUNK_ROWS,
                             0, chunks_per_seq - n_kc)
            return (b * chunks_per_seq + first + c, col)
        return f

    k_specs = [pl.BlockSpec((kc, D_ATTN), kv_map(1, c)) for c in range(n_kc)]
    v_specs = [pl.BlockSpec((kc, D_ATTN), kv_map(2, c)) for c in range(n_kc)]
    grid_spec = pltpu.PrefetchScalarGridSpec(
        num_scalar_prefetch=1,
        grid=(batch, n_rb),
        in_specs=[pl.BlockSpec((tq, D_ATTN), q_map)] + k_specs + v_specs + [
            pl.BlockSpec(pair_blocks.shape, lambda b, rb, tab: (0, 0, 0, 0),
                         pipeline_mode=pl.Buffered(1)),
            pl.BlockSpec((1, D_ATTN), lambda b, rb, tab: (0, 0)),
        ],
        out_specs=pl.BlockSpec((tq, D_ATTN), q_map),
    )
    return pl.pallas_call(
        _attn_kernel,
        grid_spec=grid_spec,
        out_shape=jax.ShapeDtypeStruct((batch * seq, D_ATTN), BF16),
        compiler_params=_params(("parallel", "arbitrary")),
        name="attn",
    )(_pair_index_table(rows), qkv, *([qkv] * (2 * n_kc)), pair_blocks, gn)


def _outproj_kernel(x_ref, a_ref, c_ref, wa_ref, wc_ref, o_ref):
    acc = jnp.dot(a_ref[...], wa_ref[...].astype(BF16), preferred_element_type=F32)
    acc = acc + jnp.dot(c_ref[...], wc_ref[...].astype(BF16), preferred_element_type=F32)
    o_ref[...] = x_ref[...] + acc


def _outproj(x, a, c, w_out):
    t, d = x.shape
    tm = OUT_TM
    return pl.pallas_call(
        _outproj_kernel,
        grid=(t // tm,),
        in_specs=[
            pl.BlockSpec((tm, d), lambda i: (i, 0)),
            pl.BlockSpec((tm, D_ATTN), lambda i: (i, 0)),
            pl.BlockSpec((tm, D_CONV), lambda i: (i, 0)),
            pl.BlockSpec((D_ATTN, d), lambda i: (0, 0), pipeline_mode=pl.Buffered(1)),
            pl.BlockSpec((D_CONV, d), lambda i: (1, 0), pipeline_mode=pl.Buffered(1)),
        ],
        out_specs=pl.BlockSpec((tm, d), lambda i: (i, 0)),
        out_shape=jax.ShapeDtypeStruct((t, d), F32),
        compiler_params=_params(("parallel",)),
        name="outproj",
    )(x, a, c, w_out, w_out)


def _ple_kernel(x_ref, p_ref, g_ref, wg_ref, wp_ref, gf_ref, o_ref):
    x = x_ref[...]
    xn = _rms(x, g_ref[...]).astype(BF16)
    gate = jax.nn.sigmoid(jnp.dot(xn, wg_ref[...].astype(BF16), preferred_element_type=F32))
    proj = jnp.dot(p_ref[...].astype(BF16), wp_ref[...].astype(BF16),
                   preferred_element_type=F32)
    o_ref[...] = _rms(x + gate * proj, gf_ref[...])


def _ple(x, p, g, w_gate, w_proj, gf):
    t, d = x.shape
    tm = PLE_TM
    return pl.pallas_call(
        _ple_kernel,
        grid=(t // tm,),
        in_specs=[
            pl.BlockSpec((tm, d), lambda i: (i, 0)),
            pl.BlockSpec((tm, PLE_DIM), lambda i: (i, 0)),
            pl.BlockSpec((1, d), lambda i: (0, 0)),
            pl.BlockSpec((d, d), lambda i: (0, 0), pipeline_mode=pl.Buffered(1)),
            pl.BlockSpec((PLE_DIM, d), lambda i: (0, 0), pipeline_mode=pl.Buffered(1)),
            pl.BlockSpec((1, d), lambda i: (0, 0)),
        ],
        out_specs=pl.BlockSpec((tm, d), lambda i: (i, 0)),
        out_shape=jax.ShapeDtypeStruct((t, d), F32),
        compiler_params=_params(("parallel",)),
        name="ple",
    )(x, p, g, w_gate, w_proj, gf)


def kernel(x, p, ffn1_norm, ffn1_wg, ffn1_wu, ffn1_wd, mix_norm, w_in, rpb, conv_w, conv_b,
           attn_out_norm, conv_out_norm, w_out, ffn2_norm, ffn2_wg, ffn2_wu, ffn2_wd,
           ple_norm, ple_w_gate, ple_w_proj, final_norm):
    b, s, d = x.shape
    assert ffn1_wg.shape[0] == 1
    assert d == D_MODEL and s % (Q_ROWS * GRID_W) == 0 and s % PROJ_TM == 0
    t = b * s
    h = x.reshape(t, d)
    row = lambda v: v.reshape(1, -1)
    for i in range(1):
        h = _ffn(h, row(ffn1_norm[i]), ffn1_wg[i], ffn1_wu[i], ffn1_wd[i])
        qkv, y_conv = _inproj(h, row(mix_norm[i]), w_in[i], conv_w[i], row(conv_b[i]),
                              row(conv_out_norm[i]), s)
        y_attn = _attn(qkv, _pair_blocks(rpb[i]), row(attn_out_norm[i]), b, s)
        h = _outproj(h, y_attn, y_conv, w_out[i])
        h = _ffn(h, row(ffn2_norm[i]), ffn2_wg[i], ffn2_wu[i], ffn2_wd[i])
        h = _ple(h, p[i].reshape(t, PLE_DIM), row(ple_norm[i]), ple_w_gate[i], ple_w_proj[i],
                 row(final_norm))
    return h.reshape(b, s, d)
```

```python
import functools

import numpy as np
import jax
import jax.numpy as jnp
from jax import lax
from jax.experimental import pallas as pl
from jax.experimental.pallas import tpu as pltpu

D_MODEL = 2048
GRID_W = 64
PLE_DIM = 256
D_ATTN = D_MODEL // 2
D_CONV = D_MODEL - D_ATTN
N_HEADS = 8
HEAD_DIM = D_ATTN // N_HEADS
GROUP_DIM = 128
CONV_W = 3
WIN_ROWS = 8
WIN_COLS = 16
D_FF = 5632
RMS_EPS = 1e-6
NEG = -1e30

BF16 = jnp.bfloat16
F32 = jnp.float32

VMEM_LIMIT = 56 * 1024 * 1024
FFN_TM = 1024
FFN_TF = 256
PROJ_TM = 1024
PROJ_TN = 512
CONV_TC = 256
HALO = 16
Q_ROWS = 8
K_ROWS = 16
K_CHUNK_ROWS = 4
OUT_TM = 512
PLE_TM = 512


def _rms(x, g):
    ms = jnp.mean(x * x, axis=-1, keepdims=True)
    return x * lax.rsqrt(ms + RMS_EPS) * g


def _params(sem):
    return pltpu.CompilerParams(dimension_semantics=sem, vmem_limit_bytes=VMEM_LIMIT)


def _ffn_kernel(x_ref, g_ref, wg_ref, wu_ref, wd_ref, o_ref, xn_ref):
    @pl.when(pl.program_id(1) == 0)
    def _():
        x = x_ref[...]
        xn_ref[...] = _rms(x, g_ref[...]).astype(BF16)
        o_ref[...] = x

    xn = xn_ref[...]
    gate = jnp.dot(xn, wg_ref[...].astype(BF16), preferred_element_type=F32)
    up = jnp.dot(xn, wu_ref[...].astype(BF16), preferred_element_type=F32)
    act = (jax.nn.silu(gate) * up * 0.5).astype(BF16)
    o_ref[...] += jnp.dot(act, wd_ref[...].astype(BF16), preferred_element_type=F32)


def _ffn(x, g, wg, wu, wd):
    t, d = x.shape
    f = wg.shape[1]
    return pl.pallas_call(
        _ffn_kernel,
        grid=(t // FFN_TM, f // FFN_TF),
        in_specs=[
            pl.BlockSpec((FFN_TM, d), lambda i, j: (i, 0)),
            pl.BlockSpec((1, d), lambda i, j: (0, 0)),
            pl.BlockSpec((d, FFN_TF), lambda i, j: (0, j)),
            pl.BlockSpec((d, FFN_TF), lambda i, j: (0, j)),
            pl.BlockSpec((FFN_TF, d), lambda i, j: (j, 0)),
        ],
        out_specs=pl.BlockSpec((FFN_TM, d), lambda i, j: (i, 0)),
        out_shape=jax.ShapeDtypeStruct((t, d), F32),
        scratch_shapes=[pltpu.VMEM((FFN_TM, d), BF16)],
        compiler_params=_params(("parallel", "arbitrary")),
        name="ffn",
    )(x, g, wg, wu, wd)


N_QKV_STEPS = 3 * D_ATTN // PROJ_TN
N_CONV_STEPS = D_CONV // CONV_TC


def _inproj_kernel(x_ref, xp_ref, xnx_ref, g_ref, wq_ref, wb_ref, wc_ref, wu_ref, cw_ref,
                   cb_ref, gn_ref, qkv_ref, conv_ref, xn_ref, cu_ref, *, seq):
    tm = x_ref.shape[0]
    j = pl.program_id(1)

    @pl.when(j == 0)
    def _():
        g = g_ref[...]
        xn_ref[0:HALO, :] = _rms(xp_ref[...], g).astype(BF16)
        xn_ref[HALO:HALO + tm, :] = _rms(x_ref[...], g).astype(BF16)
        xn_ref[HALO + tm:, :] = _rms(xnx_ref[...], g).astype(BF16)

    @pl.when(j < N_QKV_STEPS)
    def _():
        qkv_ref[...] = jnp.dot(xn_ref[HALO:HALO + tm, :], wq_ref[...].astype(BF16),
                               preferred_element_type=F32).astype(BF16)

    @pl.when(j >= N_QKV_STEPS)
    def _():
        xn = xn_ref[...]
        gate_b = jnp.dot(xn_ref[HALO:HALO + tm, :], wb_ref[...].astype(BF16),
                         preferred_element_type=F32)
        gate_c = jnp.dot(xn, wc_ref[...].astype(BF16), preferred_element_type=F32)
        u = jnp.dot(xn, wu_ref[...].astype(BF16), preferred_element_type=F32)
        cu_ref[...] = gate_c * u

        pos = (lax.broadcasted_iota(jnp.int32, (tm, 1), 0) + pl.program_id(0) * tm) % seq
        prev = jnp.where(pos == 0, 0.0, cu_ref[HALO - 1:HALO - 1 + tm, :])
        cur = cu_ref[HALO:HALO + tm, :]
        nxt = jnp.where(pos == seq - 1, 0.0, cu_ref[HALO + 1:HALO + 1 + tm, :])
        cw = cw_ref[...]
        y = prev * cw[0:1, :]
        y = y + cur * cw[1:2, :]
        y = y + nxt * cw[2:3, :]
        y = gate_b * (y + cb_ref[...])
        gn = gn_ref[...]
        for c in range(0, y.shape[1], GROUP_DIM):
            sl = slice(c, c + GROUP_DIM)
            conv_ref[:, sl] = _rms(y[:, sl], gn[:, sl]).astype(BF16)


def _inproj(x, g, w_in, conv_w, conv_b, gn, seq):
    t, d = x.shape
    tm, tn, tc = PROJ_TM, PROJ_TN, CONV_TC
    hb = tm // HALO
    nhb = t // HALO
    col_b = 3 * D_ATTN // tc
    col_c = col_b + D_CONV // tc
    col_u = col_c + D_CONV // tc
    qstep = lambda j: jnp.minimum(j, N_QKV_STEPS - 1)
    cstep = lambda j: jnp.maximum(j - N_QKV_STEPS, 0)
    return pl.pallas_call(
        functools.partial(_inproj_kernel, seq=seq),
        grid=(t // tm, N_QKV_STEPS + N_CONV_STEPS),
        in_specs=[
            pl.BlockSpec((tm, d), lambda i, j: (i, 0)),
            pl.BlockSpec((HALO, d), lambda i, j: (jnp.maximum(i * hb - 1, 0), 0)),
            pl.BlockSpec((HALO, d), lambda i, j: (jnp.minimum((i + 1) * hb, nhb - 1), 0)),
            pl.BlockSpec((1, d), lambda i, j: (0, 0)),
            pl.BlockSpec((d, tn), lambda i, j: (0, qstep(j))),
            pl.BlockSpec((d, tc), lambda i, j: (0, col_b + cstep(j))),
            pl.BlockSpec((d, tc), lambda i, j: (0, col_c + cstep(j))),
            pl.BlockSpec((d, tc), lambda i, j: (0, col_u + cstep(j))),
            pl.BlockSpec((CONV_W, tc), lambda i, j: (0, cstep(j))),
            pl.BlockSpec((1, tc), lambda i, j: (0, cstep(j))),
            pl.BlockSpec((1, tc), lambda i, j: (0, cstep(j))),
        ],
        out_specs=[pl.BlockSpec((tm, tn), lambda i, j: (i, qstep(j))),
                   pl.BlockSpec((tm, tc), lambda i, j: (i, cstep(j)))],
        out_shape=[jax.ShapeDtypeStruct((t, 3 * D_ATTN), BF16),
                   jax.ShapeDtypeStruct((t, D_CONV), BF16)],
        scratch_shapes=[pltpu.VMEM((tm + 2 * HALO, d), BF16),
                        pltpu.VMEM((tm + 2 * HALO, tc), F32)],
        compiler_params=_params(("parallel", "arbitrary")),
        name="inproj",
    )(x, x, x, g, w_in, w_in, w_in, w_in, conv_w, conv_b, gn)


N_REL = 2 * WIN_ROWS - 1
PAIR_FULL = 0
PAIR_LEFT_NEG = N_REL - 1
PAIR_RIGHT_NEG = 2 * N_REL - 1
PAIR_NEG = 3 * N_REL - 1
N_PAIR_BLOCKS = 3 * N_REL
N_ROW_PATTERNS = 3
N_PAIRS = K_ROWS // 2


def _pair_blocks(rpb):
    nh, nr, nc = rpb.shape
    w = GRID_W
    c = np.arange(w)
    col_start = np.clip(c - WIN_COLS // 2, 0, w - WIN_COLS)
    in_win = (c[None, :] >= col_start[:, None]) & (c[None, :] < col_start[:, None] + WIN_COLS)
    period = 2 * w
    left = w - WIN_COLS
    u = jnp.pad(rpb, ((0, 0), (0, 0), (left, period - nc - left)))
    toep = jnp.tile(u, (1, 1, w))[..., :w * (period - 1)].reshape(nh, nr, w, period - 1)[..., w - 1:]
    toep = jnp.where(in_win[None, None], toep, NEG)
    neg = jnp.full((nh, nr, w, w), NEG, F32)
    return jnp.concatenate([
        jnp.concatenate([toep[:, :-1], toep[:, 1:]], axis=-1),
        jnp.concatenate([neg, toep], axis=-1),
        jnp.concatenate([toep, neg], axis=-1),
        jnp.concatenate([neg[:, :1], neg[:, :1]], axis=-1)], axis=1)


def _pair_index_table(rows):
    tab = np.zeros((N_ROW_PATTERNS, Q_ROWS, N_PAIRS), np.int32)
    for pat, r0 in enumerate((0, Q_ROWS, rows - Q_ROWS)):
        k0 = min(max(r0 - WIN_ROWS // 2, 0), rows - K_ROWS)
        for i in range(Q_ROWS):
            r = r0 + i
            row_start = min(max(r - WIN_ROWS // 2, 0), rows - WIN_ROWS)
            for m in range(N_PAIRS):
                kr = k0 + 2 * m
                in0 = row_start <= kr < row_start + WIN_ROWS
                in1 = row_start <= kr + 1 < row_start + WIN_ROWS
                a = kr - r + WIN_ROWS - 1
                if in0 and in1:
                    tab[pat, i, m] = PAIR_FULL + a
                elif in1:
                    tab[pat, i, m] = PAIR_LEFT_NEG + a + 1
                elif in0:
                    tab[pat, i, m] = PAIR_RIGHT_NEG + a
                else:
                    tab[pat, i, m] = PAIR_NEG
    return jnp.asarray(tab.reshape(-1))


def _attn_kernel(tab_ref, q_ref, k0, k1, k2, k3, v0, v1, v2, v3, pb_ref, gn_ref, o_ref):
    rb = pl.program_id(1)
    pat = jnp.where(rb == 0, 0, jnp.where(rb == pl.num_programs(1) - 1, 2, 1))
    idx = [[tab_ref[(pat * Q_ROWS + i) * N_PAIRS + m] for m in range(N_PAIRS)]
           for i in range(Q_ROWS)]
    kc = k0.shape[0]
    for h in range(N_HEADS):
        hs = slice(h * HEAD_DIM, (h + 1) * HEAD_DIM)
        q = q_ref[:, hs]
        s = jnp.concatenate(
            [lax.dot_general(q, k[:, hs], (((1,), (1,)), ((), ())), preferred_element_type=F32)
             for k in (k0, k1, k2, k3)], axis=1)
        bias = jnp.concatenate(
            [jnp.concatenate([pb_ref[h, idx[i][m]] for m in range(N_PAIRS)], axis=1)
             for i in range(Q_ROWS)], axis=0)
        s = s * (HEAD_DIM ** -0.5) + bias
        m = jnp.max(s, axis=1, keepdims=True)
        p = jnp.exp(s - m)
        l = jnp.sum(p, axis=1, keepdims=True)
        pb = p.astype(BF16)
        o = None
        for c, v in enumerate((v0, v1, v2, v3)):
            part = jnp.dot(pb[:, c * kc:(c + 1) * kc], v[:, hs], preferred_element_type=F32)
            o = part if o is None else o + part
        o = o / l
        o_ref[:, hs] = _rms(o, gn_ref[:, hs]).astype(BF16)


def _attn(qkv, pair_blocks, gn, batch, seq):
    rows = seq // GRID_W
    tq = Q_ROWS * GRID_W
    kc = K_CHUNK_ROWS * GRID_W
    n_rb = rows // Q_ROWS
    n_kc = K_ROWS // K_CHUNK_ROWS
    chunks_per_seq = seq // kc

    def q_map(b, rb, tab):
        return (b * n_rb + rb, 0)

    def kv_map(col, c):
        def f(b, rb, tab):
            first = jnp.clip(rb * (Q_ROWS // K_CHUNK_ROWS) - (WIN_ROWS // 2) // K_CHUNK_ROWS,
                             0, chunks_per_seq - n_kc)
            return (b * chunks_per_seq + first + c, col)
        return f

    k_specs = [pl.BlockSpec((kc, D_ATTN), kv_map(1, c)) for c in range(n_kc)]
    v_specs = [pl.BlockSpec((kc, D_ATTN), kv_map(2, c)) for c in range(n_kc)]
    grid_spec = pltpu.PrefetchScalarGridSpec(
        num_scalar_prefetch=1,
        grid=(batch, n_rb),
        in_specs=[pl.BlockSpec((tq, D_ATTN), q_map)] + k_specs + v_specs + [
            pl.BlockSpec(pair_blocks.shape, lambda b, rb, tab: (0, 0, 0, 0),
                         pipeline_mode=pl.Buffered(1)),
            pl.BlockSpec((1, D_ATTN), lambda b, rb, tab: (0, 0)),
        ],
        out_specs=pl.BlockSpec((tq, D_ATTN), q_map),
    )
    return pl.pallas_call(
        _attn_kernel,
        grid_spec=grid_spec,
        out_shape=jax.ShapeDtypeStruct((batch * seq, D_ATTN), BF16),
        compiler_params=_params(("parallel", "arbitrary")),
        name="attn",
    )(_pair_index_table(rows), qkv, *([qkv] * (2 * n_kc)), pair_blocks, gn)


def _outproj_kernel(x_ref, a_ref, c_ref, wa_ref, wc_ref, o_ref):
    acc = jnp.dot(a_ref[...], wa_ref[...].astype(BF16), preferred_element_type=F32)
    acc = acc + jnp.dot(c_ref[...], wc_ref[...].astype(BF16), preferred_element_type=F32)
    o_ref[...] = x_ref[...] + acc


def _outproj(x, a, c, w_out):
    t, d = x.shape
    tm = OUT_TM
    return pl.pallas_call(
        _outproj_kernel,
        grid=(t // tm,),
        in_specs=[
            pl.BlockSpec((tm, d), lambda i: (i, 0)),
            pl.BlockSpec((tm, D_ATTN), lambda i: (i, 0)),
            pl.BlockSpec((tm, D_CONV), lambda i: (i, 0)),
            pl.BlockSpec((D_ATTN, d), lambda i: (0, 0), pipeline_mode=pl.Buffered(1)),
            pl.BlockSpec((D_CONV, d), lambda i: (1, 0), pipeline_mode=pl.Buffered(1)),
        ],
        out_specs=pl.BlockSpec((tm, d), lambda i: (i, 0)),
        out_shape=jax.ShapeDtypeStruct((t, d), F32),
        compiler_params=_params(("parallel",)),
        name="outproj",
    )(x, a, c, w_out, w_out)


def _ple_kernel(x_ref, p_ref, g_ref, wg_ref, wp_ref, gf_ref, o_ref):
    x = x_ref[...]
    xn = _rms(x, g_ref[...]).astype(BF16)
    gate = jax.nn.sigmoid(jnp.dot(xn, wg_ref[...].astype(BF16), preferred_element_type=F32))
    proj = jnp.dot(p_ref[...].astype(BF16), wp_ref[...].astype(BF16),
                   preferred_element_type=F32)
    o_ref[...] = _rms(x + gate * proj, gf_ref[...])


def _ple(x, p, g, w_gate, w_proj, gf):
    t, d = x.shape
    tm = PLE_TM
    return pl.pallas_call(
        _ple_kernel,
        grid=(t // tm,),
        in_specs=[
            pl.BlockSpec((tm, d), lambda i: (i, 0)),
            pl.BlockSpec((tm, PLE_DIM), lambda i: (i, 0)),
            pl.BlockSpec((1, d), lambda i: (0, 0)),
            pl.BlockSpec((d, d), lambda i: (0, 0), pipeline_mode=pl.Buffered(1)),
            pl.BlockSpec((PLE_DIM, d), lambda i: (0, 0), pipeline_mode=pl.Buffered(1)),
            pl.BlockSpec((1, d), lambda i: (0, 0)),
        ],
        out_specs=pl.BlockSpec((tm, d), lambda i: (i, 0)),
        out_shape=jax.ShapeDtypeStruct((t, d), F32),
        compiler_params=_params(("parallel",)),
        name="ple",
    )(x, p, g, w_gate, w_proj, gf)


def kernel(x, p, ffn1_norm, ffn1_wg, ffn1_wu, ffn1_wd, mix_norm, w_in, rpb, conv_w, conv_b,
           attn_out_norm, conv_out_norm, w_out, ffn2_norm, ffn2_wg, ffn2_wu, ffn2_wd,
           ple_norm, ple_w_gate, ple_w_proj, final_norm):
    b, s, d = x.shape
    assert ffn1_wg.shape[0] == 1
    assert d == D_MODEL and s % (Q_ROWS * GRID_W) == 0 and s % PROJ_TM == 0
    t = b * s
    h = x.reshape(t, d)
    row = lambda v: v.reshape(1, -1)
    for i in range(1):
        h = _ffn(h, row(ffn1_norm[i]), ffn1_wg[i], ffn1_wu[i], ffn1_wd[i])
        qkv, y_conv = _inproj(h, row(mix_norm[i]), w_in[i], conv_w[i], row(conv_b[i]),
                              row(conv_out_norm[i]), s)
        y_attn = _attn(qkv, _pair_blocks(rpb[i]), row(attn_out_norm[i]), b, s)
        h = _outproj(h, y_attn, y_conv, w_out[i])
        h = _ffn(h, row(ffn2_norm[i]), ffn2_wg[i], ffn2_wu[i], ffn2_wd[i])
        h = _ple(h, p[i].reshape(t, PLE_DIM), row(ple_norm[i]), ple_w_gate[i], ple_w_proj[i],
                 row(final_norm))
    return h.reshape(b, s, d)
```

```python
import functools

import numpy as np
import jax
import jax.numpy as jnp
from jax import lax
from jax.experimental import pallas as pl
from jax.experimental.pallas import tpu as pltpu

D_MODEL = 2048
GRID_W = 64
PLE_DIM = 256
D_ATTN = D_MODEL // 2
D_CONV = D_MODEL - D_ATTN
N_HEADS = 8
HEAD_DIM = D_ATTN // N_HEADS
GROUP_DIM = 128
CONV_W = 3
WIN_ROWS = 8
WIN_COLS = 16
D_FF = 5632
RMS_EPS = 1e-6
NEG = -1e30

BF16 = jnp.bfloat16
F32 = jnp.float32

VMEM_LIMIT = 56 * 1024 * 1024
FFN_VMEM_LIMIT = 60 * 1024 * 1024
FFN_TM = 1024
FFN_TF = 512
PROJ_TM = 1024
PROJ_TN = 512
CONV_TC = 256
HALO = 16
Q_ROWS = 8
K_ROWS = 16
K_CHUNK_ROWS = 4
OUT_TM = 512
PLE_TM = 512


def _rms(x, g):
    ms = jnp.mean(x * x, axis=-1, keepdims=True)
    return x * lax.rsqrt(ms + RMS_EPS) * g


def _params(sem, vmem_limit=VMEM_LIMIT):
    return pltpu.CompilerParams(dimension_semantics=sem, vmem_limit_bytes=vmem_limit)


def _ffn_kernel(x_hbm, g_ref, wg_ref, wu_ref, wd_ref, o_ref, xbuf, xn_ref, sem):
    i = pl.program_id(0)
    j = pl.program_id(1)
    tm = xbuf.shape[0]

    def x_copy(tile):
        rows = pl.ds(pl.multiple_of(tile * tm, tm), tm)
        return pltpu.make_async_copy(x_hbm.at[rows, :], xbuf, sem)

    @pl.when(j == 0)
    def _():
        @pl.when(i == 0)
        def _():
            x_copy(0).start()

        x_copy(i).wait()
        x = xbuf[...]
        xn_ref[...] = _rms(x, g_ref[...]).astype(BF16)
        o_ref[...] = x

    @pl.when((j == 1) & (i + 1 < pl.num_programs(0)))
    def _():
        x_copy(i + 1).start()

    xn = xn_ref[...]
    gate = jnp.dot(xn, wg_ref[...].astype(BF16), preferred_element_type=F32)
    up = jnp.dot(xn, wu_ref[...].astype(BF16), preferred_element_type=F32)
    act = (jax.nn.silu(gate) * up * 0.5).astype(BF16)
    o_ref[...] += jnp.dot(act, wd_ref[...].astype(BF16), preferred_element_type=F32)


def _ffn(x, g, wg, wu, wd):
    t, d = x.shape
    f = wg.shape[1]
    return pl.pallas_call(
        _ffn_kernel,
        grid=(t // FFN_TM, f // FFN_TF),
        in_specs=[
            pl.BlockSpec(memory_space=pl.ANY),
            pl.BlockSpec((1, d), lambda i, j: (0, 0)),
            pl.BlockSpec((d, FFN_TF), lambda i, j: (0, j)),
            pl.BlockSpec((d, FFN_TF), lambda i, j: (0, j)),
            pl.BlockSpec((FFN_TF, d), lambda i, j: (j, 0)),
        ],
        out_specs=pl.BlockSpec((FFN_TM, d), lambda i, j: (i, 0)),
        out_shape=jax.ShapeDtypeStruct((t, d), F32),
        scratch_shapes=[pltpu.VMEM((FFN_TM, d), F32),
                        pltpu.VMEM((FFN_TM, d), BF16),
                        pltpu.SemaphoreType.DMA(())],
        compiler_params=_params(("arbitrary", "arbitrary"), FFN_VMEM_LIMIT),
        name="ffn",
    )(x, g, wg, wu, wd)


N_QKV_STEPS = 3 * D_ATTN // PROJ_TN
N_CONV_STEPS = D_CONV // CONV_TC


def _inproj_kernel(x_ref, xp_ref, xnx_ref, g_ref, wq_ref, wb_ref, wc_ref, wu_ref, cw_ref,
                   cb_ref, gn_ref, qkv_ref, conv_ref, xn_ref, cu_ref, *, seq):
    tm = x_ref.shape[0]
    j = pl.program_id(1)

    @pl.when(j == 0)
    def _():
        g = g_ref[...]
        xn_ref[0:HALO, :] = _rms(xp_ref[...], g).astype(BF16)
        xn_ref[HALO:HALO + tm, :] = _rms(x_ref[...], g).astype(BF16)
        xn_ref[HALO + tm:, :] = _rms(xnx_ref[...], g).astype(BF16)

    @pl.when(j < N_QKV_STEPS)
    def _():
        qkv_ref[...] = jnp.dot(xn_ref[HALO:HALO + tm, :], wq_ref[...].astype(BF16),
                               preferred_element_type=F32).astype(BF16)

    @pl.when(j >= N_QKV_STEPS)
    def _():
        xn = xn_ref[...]
        gate_b = jnp.dot(xn_ref[HALO:HALO + tm, :], wb_ref[...].astype(BF16),
                         preferred_element_type=F32)
        gate_c = jnp.dot(xn, wc_ref[...].astype(BF16), preferred_element_type=F32)
        u = jnp.dot(xn, wu_ref[...].astype(BF16), preferred_element_type=F32)
        cu_ref[...] = gate_c * u

        pos = (lax.broadcasted_iota(jnp.int32, (tm, 1), 0) + pl.program_id(0) * tm) % seq
        prev = jnp.where(pos == 0, 0.0, cu_ref[HALO - 1:HALO - 1 + tm, :])
        cur = cu_ref[HALO:HALO + tm, :]
        nxt = jnp.where(pos == seq - 1, 0.0, cu_ref[HALO + 1:HALO + 1 + tm, :])
        cw = cw_ref[...]
        y = prev * cw[0:1, :]
        y = y + cur * cw[1:2, :]
        y = y + nxt * cw[2:3, :]
        y = gate_b * (y + cb_ref[...])
        gn = gn_ref[...]
        for c in range(0, y.shape[1], GROUP_DIM):
            sl = slice(c, c + GROUP_DIM)
            conv_ref[:, sl] = _rms(y[:, sl], gn[:, sl]).astype(BF16)


def _inproj(x, g, w_in, conv_w, conv_b, gn, seq):
    t, d = x.shape
    tm, tn, tc = PROJ_TM, PROJ_TN, CONV_TC
    hb = tm // HALO
    nhb = t // HALO
    col_b = 3 * D_ATTN // tc
    col_c = col_b + D_CONV // tc
    col_u = col_c + D_CONV // tc
    qstep = lambda j: jnp.minimum(j, N_QKV_STEPS - 1)
    cstep = lambda j: jnp.maximum(j - N_QKV_STEPS, 0)
    return pl.pallas_call(
        functools.partial(_inproj_kernel, seq=seq),
        grid=(t // tm, N_QKV_STEPS + N_CONV_STEPS),
        in_specs=[
            pl.BlockSpec((tm, d), lambda i, j: (i, 0)),
            pl.BlockSpec((HALO, d), lambda i, j: (jnp.maximum(i * hb - 1, 0), 0)),
            pl.BlockSpec((HALO, d), lambda i, j: (jnp.minimum((i + 1) * hb, nhb - 1), 0)),
            pl.BlockSpec((1, d), lambda i, j: (0, 0)),
            pl.BlockSpec((d, tn), lambda i, j: (0, qstep(j))),
            pl.BlockSpec((d, tc), lambda i, j: (0, col_b + cstep(j))),
            pl.BlockSpec((d, tc), lambda i, j: (0, col_c + cstep(j))),
            pl.BlockSpec((d, tc), lambda i, j: (0, col_u + cstep(j))),
            pl.BlockSpec((CONV_W, tc), lambda i, j: (0, cstep(j))),
            pl.BlockSpec((1, tc), lambda i, j: (0, cstep(j))),
            pl.BlockSpec((1, tc), lambda i, j: (0, cstep(j))),
        ],
        out_specs=[pl.BlockSpec((tm, tn), lambda i, j: (i, qstep(j))),
                   pl.BlockSpec((tm, tc), lambda i, j: (i, cstep(j)))],
        out_shape=[jax.ShapeDtypeStruct((t, 3 * D_ATTN), BF16),
                   jax.ShapeDtypeStruct((t, D_CONV), BF16)],
        scratch_shapes=[pltpu.VMEM((tm + 2 * HALO, d), BF16),
                        pltpu.VMEM((tm + 2 * HALO, tc), F32)],
        compiler_params=_params(("parallel", "arbitrary")),
        name="inproj",
    )(x, x, x, g, w_in, w_in, w_in, w_in, conv_w, conv_b, gn)


N_REL = 2 * WIN_ROWS - 1
PAIR_FULL = 0
PAIR_LEFT_NEG = N_REL - 1
PAIR_RIGHT_NEG = 2 * N_REL - 1
PAIR_NEG = 3 * N_REL - 1
N_PAIR_BLOCKS = 3 * N_REL
N_ROW_PATTERNS = 3
N_PAIRS = K_ROWS // 2


def _pair_blocks(rpb):
    nh, nr, nc = rpb.shape
    w = GRID_W
    c = np.arange(w)
    col_start = np.clip(c - WIN_COLS // 2, 0, w - WIN_COLS)
    in_win = (c[None, :] >= col_start[:, None]) & (c[None, :] < col_start[:, None] + WIN_COLS)
    period = 2 * w
    left = w - WIN_COLS
    u = jnp.pad(rpb, ((0, 0), (0, 0), (left, period - nc - left)))
    toep = jnp.tile(u, (1, 1, w))[..., :w * (period - 1)].reshape(nh, nr, w, period - 1)[..., w - 1:]
    toep = jnp.where(in_win[None, None], toep, NEG)
    neg = jnp.full((nh, nr, w, w), NEG, F32)
    return jnp.concatenate([
        jnp.concatenate([toep[:, :-1], toep[:, 1:]], axis=-1),
        jnp.concatenate([neg, toep], axis=-1),
        jnp.concatenate([toep, neg], axis=-1),
        jnp.concatenate([neg[:, :1], neg[:, :1]], axis=-1)], axis=1)


def _pair_index_table(rows):
    tab = np.zeros((N_ROW_PATTERNS, Q_ROWS, N_PAIRS), np.int32)
    for pat, r0 in enumerate((0, Q_ROWS, rows - Q_ROWS)):
        k0 = min(max(r0 - WIN_ROWS // 2, 0), rows - K_ROWS)
        for i in range(Q_ROWS):
            r = r0 + i
            row_start = min(max(r - WIN_ROWS // 2, 0), rows - WIN_ROWS)
            for m in range(N_PAIRS):
                kr = k0 + 2 * m
                in0 = row_start <= kr < row_start + WIN_ROWS
                in1 = row_start <= kr + 1 < row_start + WIN_ROWS
                a = kr - r + WIN_ROWS - 1
                if in0 and in1:
                    tab[pat, i, m] = PAIR_FULL + a
                elif in1:
                    tab[pat, i, m] = PAIR_LEFT_NEG + a + 1
                elif in0:
                    tab[pat, i, m] = PAIR_RIGHT_NEG + a
                else:
                    tab[pat, i, m] = PAIR_NEG
    return jnp.asarray(tab.reshape(-1))


def _attn_kernel(tab_ref, q_ref, k0, k1, k2, k3, v0, v1, v2, v3, pb_ref, gn_ref, o_ref):
    rb = pl.program_id(1)
    pat = jnp.where(rb == 0, 0, jnp.where(rb == pl.num_programs(1) - 1, 2, 1))
    idx = [[tab_ref[(pat * Q_ROWS + i) * N_PAIRS + m] for m in range(N_PAIRS)]
           for i in range(Q_ROWS)]
    kc = k0.shape[0]
    for h in range(N_HEADS):
        hs = slice(h * HEAD_DIM, (h + 1) * HEAD_DIM)
        q = q_ref[:, hs]
        s = jnp.concatenate(
            [lax.dot_general(q, k[:, hs], (((1,), (1,)), ((), ())), preferred_element_type=F32)
             for k in (k0, k1, k2, k3)], axis=1)
        bias = jnp.concatenate(
            [jnp.concatenate([pb_ref[h, idx[i][m]] for m in range(N_PAIRS)], axis=1)
             for i in range(Q_ROWS)], axis=0)
        s = s * (HEAD_DIM ** -0.5) + bias
        m = jnp.max(s, axis=1, keepdims=True)
        p = jnp.exp(s - m)
        l = jnp.sum(p, axis=1, keepdims=True)
        pb = p.astype(BF16)
        o = None
        for c, v in enumerate((v0, v1, v2, v3)):
            part = jnp.dot(pb[:, c * kc:(c + 1) * kc], v[:, hs], preferred_element_type=F32)
            o = part if o is None else o + part
        o = o / l
        o_ref[:, hs] = _rms(o, gn_ref[:, hs]).astype(BF16)


def _attn(qkv, pair_blocks, gn, batch, seq):
    rows = seq // GRID_W
    tq = Q_ROWS * GRID_W
    kc = K_CHUNK_ROWS * GRID_W
    n_rb = rows // Q_ROWS
    n_kc = K_ROWS // K_CHUNK_ROWS
    chunks_per_seq = seq // kc

    def q_map(b, rb, tab):
        return (b * n_rb + rb, 0)

    def kv_map(col, c):
        def f(b, rb, tab):
            first = jnp.clip(rb * (Q_ROWS // K_CHUNK_ROWS) - (WIN_ROWS // 2) // K_CHUNK_ROWS,
                             0, chunks_per_seq - n_kc)
            return (b * chunks_per_seq + first + c, col)
        return f

    k_specs = [pl.BlockSpec((kc, D_ATTN), kv_map(1, c)) for c in range(n_kc)]
    v_specs = [pl.BlockSpec((kc, D_ATTN), kv_map(2, c)) for c in range(n_kc)]
    grid_spec = pltpu.PrefetchScalarGridSpec(
        num_scalar_prefetch=1,
        grid=(batch, n_rb),
        in_specs=[pl.BlockSpec((tq, D_ATTN), q_map)] + k_specs + v_specs + [
            pl.BlockSpec(pair_blocks.shape, lambda b, rb, tab: (0, 0, 0, 0),
                         pipeline_mode=pl.Buffered(1)),
            pl.BlockSpec((1, D_ATTN), lambda b, rb, tab: (0, 0)),
        ],
        out_specs=pl.BlockSpec((tq, D_ATTN), q_map),
    )
    return pl.pallas_call(
        _attn_kernel,
        grid_spec=grid_spec,
        out_shape=jax.ShapeDtypeStruct((batch * seq, D_ATTN), BF16),
        compiler_params=_params(("parallel", "arbitrary")),
        name="attn",
    )(_pair_index_table(rows), qkv, *([qkv] * (2 * n_kc)), pair_blocks, gn)


def _outproj_kernel(x_ref, a_ref, c_ref, wa_ref, wc_ref, o_ref):
    acc = jnp.dot(a_ref[...], wa_ref[...].astype(BF16), preferred_element_type=F32)
    acc = acc + jnp.dot(c_ref[...], wc_ref[...].astype(BF16), preferred_element_type=F32)
    o_ref[...] = x_ref[...] + acc


def _outproj(x, a, c, w_out):
    t, d = x.shape
    tm = OUT_TM
    return pl.pallas_call(
        _outproj_kernel,
        grid=(t // tm,),
        in_specs=[
            pl.BlockSpec((tm, d), lambda i: (i, 0)),
            pl.BlockSpec((tm, D_ATTN), lambda i: (i, 0)),
            pl.BlockSpec((tm, D_CONV), lambda i: (i, 0)),
            pl.BlockSpec((D_ATTN, d), lambda i: (0, 0), pipeline_mode=pl.Buffered(1)),
            pl.BlockSpec((D_CONV, d), lambda i: (1, 0), pipeline_mode=pl.Buffered(1)),
        ],
        out_specs=pl.BlockSpec((tm, d), lambda i: (i, 0)),
        out_shape=jax.ShapeDtypeStruct((t, d), F32),
        compiler_params=_params(("parallel",)),
        name="outproj",
    )(x, a, c, w_out, w_out)


def _ple_kernel(x_ref, p_ref, g_ref, wg_ref, wp_ref, gf_ref, o_ref):
    x = x_ref[...]
    xn = _rms(x, g_ref[...]).astype(BF16)
    gate = jax.nn.sigmoid(jnp.dot(xn, wg_ref[...].astype(BF16), preferred_element_type=F32))
    proj = jnp.dot(p_ref[...].astype(BF16), wp_ref[...].astype(BF16),
                   preferred_element_type=F32)
    o_ref[...] = _rms(x + gate * proj, gf_ref[...])


def _ple(x, p, g, w_gate, w_proj, gf):
    t, d = x.shape
    tm = PLE_TM
    return pl.pallas_call(
        _ple_kernel,
        grid=(t // tm,),
        in_specs=[
            pl.BlockSpec((tm, d), lambda i: (i, 0)),
            pl.BlockSpec((tm, PLE_DIM), lambda i: (i, 0)),
            pl.BlockSpec((1, d), lambda i: (0, 0)),
            pl.BlockSpec((d, d), lambda i: (0, 0), pipeline_mode=pl.Buffered(1)),
            pl.BlockSpec((PLE_DIM, d), lambda i: (0, 0), pipeline_mode=pl.Buffered(1)),
            pl.BlockSpec((1, d), lambda i: (0, 0)),
        ],
        out_specs=pl.BlockSpec((tm, d), lambda i: (i, 0)),
        out_shape=jax.ShapeDtypeStruct((t, d), F32),
        compiler_params=_params(("parallel",)),
        name="ple",
    )(x, p, g, w_gate, w_proj, gf)


def kernel(x, p, ffn1_norm, ffn1_wg, ffn1_wu, ffn1_wd, mix_norm, w_in, rpb, conv_w, conv_b,
           attn_out_norm, conv_out_norm, w_out, ffn2_norm, ffn2_wg, ffn2_wu, ffn2_wd,
           ple_norm, ple_w_gate, ple_w_proj, final_norm):
    b, s, d = x.shape
    assert ffn1_wg.shape[0] == 1
    assert d == D_MODEL and s % (Q_ROWS * GRID_W) == 0 and s % PROJ_TM == 0
    t = b * s
    h = x.reshape(t, d)
    row = lambda v: v.reshape(1, -1)
    for i in range(1):
        h = _ffn(h, row(ffn1_norm[i]), ffn1_wg[i], ffn1_wu[i], ffn1_wd[i])
        qkv, y_conv = _inproj(h, row(mix_norm[i]), w_in[i], conv_w[i], row(conv_b[i]),
                              row(conv_out_norm[i]), s)
        y_attn = _attn(qkv, _pair_blocks(rpb[i]), row(attn_out_norm[i]), b, s)
        h = _outproj(h, y_attn, y_conv, w_out[i])
        h = _ffn(h, row(ffn2_norm[i]), ffn2_wg[i], ffn2_wu[i], ffn2_wd[i])
        h = _ple(h, p[i].reshape(t, PLE_DIM), row(ple_norm[i]), ple_w_gate[i], ple_w_proj[i],
                 row(final_norm))
    return h.reshape(b, s, d)
```

```python
import functools

import numpy as np
import jax
import jax.numpy as jnp
from jax import lax
from jax.experimental import pallas as pl
from jax.experimental.pallas import tpu as pltpu

D_MODEL = 2048
GRID_W = 64
PLE_DIM = 256
D_ATTN = D_MODEL // 2
D_CONV = D_MODEL - D_ATTN
N_HEADS = 8
HEAD_DIM = D_ATTN // N_HEADS
GROUP_DIM = 128
CONV_W = 3
WIN_ROWS = 8
WIN_COLS = 16
D_FF = 5632
RMS_EPS = 1e-6
NEG = -1e30
LOG2E = 1.4426950408889634
Q_SCALE = HEAD_DIM ** -0.5 * LOG2E

BF16 = jnp.bfloat16
F32 = jnp.float32

VMEM_LIMIT = 56 * 1024 * 1024
FFN_VMEM_LIMIT = 60 * 1024 * 1024
FFN_TM = 1024
FFN_TF = 512
PROJ_TM = 1024
PROJ_TN = 512
CONV_TC = 256
CONV_SUBTILES = 2
HALO = 16
Q_ROWS = 8
K_ROWS = 16
K_CHUNK_ROWS = 4
OUT_TM = 512
PLE_TM = 512


def _rms(x, g):
    ms = jnp.mean(x * x, axis=-1, keepdims=True)
    return x * lax.rsqrt(ms + RMS_EPS) * g


def _params(sem, vmem_limit=VMEM_LIMIT):
    return pltpu.CompilerParams(dimension_semantics=sem, vmem_limit_bytes=vmem_limit)


def _ffn_kernel(x_hbm, g_ref, wg_ref, wu_ref, wd_ref, o_ref, xbuf, xn_ref, sem):
    i = pl.program_id(0)
    j = pl.program_id(1)
    tm = xbuf.shape[0]

    def x_copy(tile):
        rows = pl.ds(pl.multiple_of(tile * tm, tm), tm)
        return pltpu.make_async_copy(x_hbm.at[rows, :], xbuf, sem)

    @pl.when(j == 0)
    def _():
        @pl.when(i == 0)
        def _():
            x_copy(0).start()

        x_copy(i).wait()
        x = xbuf[...]
        xn_ref[...] = _rms(x, g_ref[...]).astype(BF16)
        o_ref[...] = x

    @pl.when((j == 1) & (i + 1 < pl.num_programs(0)))
    def _():
        x_copy(i + 1).start()

    xn = xn_ref[...]
    gate = jnp.dot(xn, wg_ref[...].astype(BF16), preferred_element_type=F32)
    up = jnp.dot(xn, wu_ref[...].astype(BF16), preferred_element_type=F32)
    act = (jax.nn.silu(gate) * up * 0.5).astype(BF16)
    o_ref[...] += jnp.dot(act, wd_ref[...].astype(BF16), preferred_element_type=F32)


def _ffn(x, g, wg, wu, wd):
    t, d = x.shape
    f = wg.shape[1]
    return pl.pallas_call(
        _ffn_kernel,
        grid=(t // FFN_TM, f // FFN_TF),
        in_specs=[
            pl.BlockSpec(memory_space=pl.ANY),
            pl.BlockSpec((1, d), lambda i, j: (0, 0)),
            pl.BlockSpec((d, FFN_TF), lambda i, j: (0, j)),
            pl.BlockSpec((d, FFN_TF), lambda i, j: (0, j)),
            pl.BlockSpec((FFN_TF, d), lambda i, j: (j, 0)),
        ],
        out_specs=pl.BlockSpec((FFN_TM, d), lambda i, j: (i, 0)),
        out_shape=jax.ShapeDtypeStruct((t, d), F32),
        scratch_shapes=[pltpu.VMEM((FFN_TM, d), F32),
                        pltpu.VMEM((FFN_TM, d), BF16),
                        pltpu.SemaphoreType.DMA(())],
        compiler_params=_params(("arbitrary", "arbitrary"), FFN_VMEM_LIMIT),
        name="ffn",
    )(x, g, wg, wu, wd)


N_QKV_STEPS = 3 * D_ATTN // PROJ_TN
N_CONV_STEPS = D_CONV // CONV_TC


def _inproj_kernel(x_ref, xp_ref, xnx_ref, g_ref, wq_ref, wb_ref, wc_ref, wu_ref, cw_ref,
                   cb_ref, gn_ref, qkv_ref, conv_ref, xn_ref, cu_ref, *, seq):
    tm = x_ref.shape[0]
    j = pl.program_id(1)

    @pl.when(j == 0)
    def _():
        g = g_ref[...]
        xn_ref[0:HALO, :] = _rms(xp_ref[...], g).astype(BF16)
        xn_ref[HALO:HALO + tm, :] = _rms(x_ref[...], g).astype(BF16)
        xn_ref[HALO + tm:, :] = _rms(xnx_ref[...], g).astype(BF16)

    @pl.when(j < N_QKV_STEPS)
    def _():
        acc = jnp.dot(xn_ref[HALO:HALO + tm, :], wq_ref[...].astype(BF16),
                      preferred_element_type=F32)
        col_scale = jnp.where(j < D_ATTN // PROJ_TN, Q_SCALE, 1.0)
        qkv_ref[...] = (acc * col_scale).astype(BF16)

    @pl.when(j >= N_QKV_STEPS)
    def _():
        wb = wb_ref[...].astype(BF16)
        wc = wc_ref[...].astype(BF16)
        wu = wu_ref[...].astype(BF16)
        cw = cw_ref[...]
        cb = cb_ref[...]
        gn = gn_ref[...]
        ts = tm // CONV_SUBTILES
        for sub in range(CONV_SUBTILES):
            r0 = sub * ts
            xs = xn_ref[r0:r0 + ts + 2 * HALO, :]
            gate_b = jnp.dot(xn_ref[HALO + r0:HALO + r0 + ts, :], wb, preferred_element_type=F32)
            gate_c = jnp.dot(xs, wc, preferred_element_type=F32)
            u = jnp.dot(xs, wu, preferred_element_type=F32)
            cu_ref[sub] = gate_c * u

            pos = (lax.broadcasted_iota(jnp.int32, (ts, 1), 0) + pl.program_id(0) * tm + r0) % seq
            prev = jnp.where(pos == 0, 0.0, cu_ref[sub, HALO - 1:HALO - 1 + ts, :])
            cur = cu_ref[sub, HALO:HALO + ts, :]
            nxt = jnp.where(pos == seq - 1, 0.0, cu_ref[sub, HALO + 1:HALO + 1 + ts, :])
            y = prev * cw[0:1, :]
            y = y + cur * cw[1:2, :]
            y = y + nxt * cw[2:3, :]
            y = gate_b * (y + cb)
            for c in range(0, y.shape[1], GROUP_DIM):
                sl = slice(c, c + GROUP_DIM)
                conv_ref[r0:r0 + ts, sl] = _rms(y[:, sl], gn[:, sl]).astype(BF16)


def _inproj(x, g, w_in, conv_w, conv_b, gn, seq):
    t, d = x.shape
    tm, tn, tc = PROJ_TM, PROJ_TN, CONV_TC
    hb = tm // HALO
    nhb = t // HALO
    col_b = 3 * D_ATTN // tc
    col_c = col_b + D_CONV // tc
    col_u = col_c + D_CONV // tc
    qstep = lambda j: jnp.minimum(j, N_QKV_STEPS - 1)
    cstep = lambda j: jnp.maximum(j - N_QKV_STEPS, 0)
    return pl.pallas_call(
        functools.partial(_inproj_kernel, seq=seq),
        grid=(t // tm, N_QKV_STEPS + N_CONV_STEPS),
        in_specs=[
            pl.BlockSpec((tm, d), lambda i, j: (i, 0)),
            pl.BlockSpec((HALO, d), lambda i, j: (jnp.maximum(i * hb - 1, 0), 0)),
            pl.BlockSpec((HALO, d), lambda i, j: (jnp.minimum((i + 1) * hb, nhb - 1), 0)),
            pl.BlockSpec((1, d), lambda i, j: (0, 0)),
            pl.BlockSpec((d, tn), lambda i, j: (0, qstep(j))),
            pl.BlockSpec((d, tc), lambda i, j: (0, col_b + cstep(j))),
            pl.BlockSpec((d, tc), lambda i, j: (0, col_c + cstep(j))),
            pl.BlockSpec((d, tc), lambda i, j: (0, col_u + cstep(j))),
            pl.BlockSpec((CONV_W, tc), lambda i, j: (0, cstep(j))),
            pl.BlockSpec((1, tc), lambda i, j: (0, cstep(j))),
            pl.BlockSpec((1, tc), lambda i, j: (0, cstep(j))),
        ],
        out_specs=[pl.BlockSpec((tm, tn), lambda i, j: (i, qstep(j))),
                   pl.BlockSpec((tm, tc), lambda i, j: (i, cstep(j)))],
        out_shape=[jax.ShapeDtypeStruct((t, 3 * D_ATTN), BF16),
                   jax.ShapeDtypeStruct((t, D_CONV), BF16)],
        scratch_shapes=[pltpu.VMEM((tm + 2 * HALO, d), BF16),
                        pltpu.VMEM((CONV_SUBTILES, tm // CONV_SUBTILES + 2 * HALO, tc), F32)],
        compiler_params=_params(("parallel", "arbitrary")),
        name="inproj",
    )(x, x, x, g, w_in, w_in, w_in, w_in, conv_w, conv_b, gn)


N_REL = 2 * WIN_ROWS - 1
PAIR_FULL = 0
PAIR_LEFT_NEG = N_REL - 1
PAIR_RIGHT_NEG = 2 * N_REL - 1
PAIR_NEG = 3 * N_REL - 1
N_PAIR_BLOCKS = 3 * N_REL
N_ROW_PATTERNS = 3
N_PAIRS = K_ROWS // 2


def _pair_blocks(rpb):
    nh, nr, nc = rpb.shape
    w = GRID_W
    c = np.arange(w)
    col_start = np.clip(c - WIN_COLS // 2, 0, w - WIN_COLS)
    in_win = (c[None, :] >= col_start[:, None]) & (c[None, :] < col_start[:, None] + WIN_COLS)
    period = 2 * w
    left = w - WIN_COLS
    u = jnp.pad(rpb, ((0, 0), (0, 0), (left, period - nc - left)))
    toep = jnp.tile(u, (1, 1, w))[..., :w * (period - 1)].reshape(nh, nr, w, period - 1)[..., w - 1:]
    toep = jnp.where(in_win[None, None], toep * LOG2E, NEG)
    neg = jnp.full((nh, nr, w, w), NEG, F32)
    return jnp.concatenate([
        jnp.concatenate([toep[:, :-1], toep[:, 1:]], axis=-1),
        jnp.concatenate([neg, toep], axis=-1),
        jnp.concatenate([toep, neg], axis=-1),
        jnp.concatenate([neg[:, :1], neg[:, :1]], axis=-1)], axis=1)


def _pair_index_table(rows):
    tab = np.zeros((N_ROW_PATTERNS, Q_ROWS, N_PAIRS), np.int32)
    for pat, r0 in enumerate((0, Q_ROWS, rows - Q_ROWS)):
        k0 = min(max(r0 - WIN_ROWS // 2, 0), rows - K_ROWS)
        for i in range(Q_ROWS):
            r = r0 + i
            row_start = min(max(r - WIN_ROWS // 2, 0), rows - WIN_ROWS)
            for m in range(N_PAIRS):
                kr = k0 + 2 * m
                in0 = row_start <= kr < row_start + WIN_ROWS
                in1 = row_start <= kr + 1 < row_start + WIN_ROWS
                a = kr - r + WIN_ROWS - 1
                if in0 and in1:
                    tab[pat, i, m] = PAIR_FULL + a
                elif in1:
                    tab[pat, i, m] = PAIR_LEFT_NEG + a + 1
                elif in0:
                    tab[pat, i, m] = PAIR_RIGHT_NEG + a
                else:
                    tab[pat, i, m] = PAIR_NEG
    return jnp.asarray(tab.reshape(-1))


def _attn_kernel(tab_ref, q_ref, k0, k1, k2, k3, v0, v1, v2, v3, pb_ref, gn_ref, o_ref):
    rb = pl.program_id(1)
    pat = jnp.where(rb == 0, 0, jnp.where(rb == pl.num_programs(1) - 1, 2, 1))
    idx = [[tab_ref[(pat * Q_ROWS + i) * N_PAIRS + m] for m in range(N_PAIRS)]
           for i in range(Q_ROWS)]
    kc = k0.shape[0]
    for h in range(N_HEADS):
        hs = slice(h * HEAD_DIM, (h + 1) * HEAD_DIM)
        q = q_ref[:, hs]
        s = jnp.concatenate(
            [lax.dot_general(q, k[:, hs], (((1,), (1,)), ((), ())), preferred_element_type=F32)
             for k in (k0, k1, k2, k3)], axis=1)
        bias = jnp.concatenate(
            [jnp.concatenate([pb_ref[h, idx[i][m]] for m in range(N_PAIRS)], axis=1)
             for i in range(Q_ROWS)], axis=0)
        s = s + bias
        m = jnp.max(s, axis=1, keepdims=True)
        p = jnp.exp2(s - m)
        l = jnp.sum(p, axis=1, keepdims=True)
        pb = p.astype(BF16)
        o = None
        for c, v in enumerate((v0, v1, v2, v3)):
            part = jnp.dot(pb[:, c * kc:(c + 1) * kc], v[:, hs], preferred_element_type=F32)
            o = part if o is None else o + part
        o = o / l
        o_ref[:, hs] = _rms(o, gn_ref[:, hs]).astype(BF16)


def _attn(qkv, pair_blocks, gn, batch, seq):
    rows = seq // GRID_W
    tq = Q_ROWS * GRID_W
    kc = K_CHUNK_ROWS * GRID_W
    n_rb = rows // Q_ROWS
    n_kc = K_ROWS // K_CHUNK_ROWS
    chunks_per_seq = seq // kc

    def q_map(b, rb, tab):
        return (b * n_rb + rb, 0)

    def kv_map(col, c):
        def f(b, rb, tab):
            first = jnp.clip(rb * (Q_ROWS // K_CHUNK_ROWS) - (WIN_ROWS // 2) // K_CHUNK_ROWS,
                             0, chunks_per_seq - n_kc)
            return (b * chunks_per_seq + first + c, col)
        return f

    k_specs = [pl.BlockSpec((kc, D_ATTN), kv_map(1, c)) for c in range(n_kc)]
    v_specs = [pl.BlockSpec((kc, D_ATTN), kv_map(2, c)) for c in range(n_kc)]
    grid_spec = pltpu.PrefetchScalarGridSpec(
        num_scalar_prefetch=1,
        grid=(batch, n_rb),
        in_specs=[pl.BlockSpec((tq, D_ATTN), q_map)] + k_specs + v_specs + [
            pl.BlockSpec(pair_blocks.shape, lambda b, rb, tab: (0, 0, 0, 0),
                         pipeline_mode=pl.Buffered(1)),
            pl.BlockSpec((1, D_ATTN), lambda b, rb, tab: (0, 0)),
        ],
        out_specs=pl.BlockSpec((tq, D_ATTN), q_map),
    )
    return pl.pallas_call(
        _attn_kernel,
        grid_spec=grid_spec,
        out_shape=jax.ShapeDtypeStruct((batch * seq, D_ATTN), BF16),
        compiler_params=_params(("parallel", "arbitrary")),
        name="attn",
    )(_pair_index_table(rows), qkv, *([qkv] * (2 * n_kc)), pair_blocks, gn)


def _outproj_kernel(x_ref, a_ref, c_ref, wa_ref, wc_ref, o_ref):
    acc = jnp.dot(a_ref[...], wa_ref[...].astype(BF16), preferred_element_type=F32)
    acc = acc + jnp.dot(c_ref[...], wc_ref[...].astype(BF16), preferred_element_type=F32)
    o_ref[...] = x_ref[...] + acc


def _outproj(x, a, c, w_out):
    t, d = x.shape
    tm = OUT_TM
    return pl.pallas_call(
        _outproj_kernel,
        grid=(t // tm,),
        in_specs=[
            pl.BlockSpec((tm, d), lambda i: (i, 0)),
            pl.BlockSpec((tm, D_ATTN), lambda i: (i, 0)),
            pl.BlockSpec((tm, D_CONV), lambda i: (i, 0)),
            pl.BlockSpec((D_ATTN, d), lambda i: (0, 0), pipeline_mode=pl.Buffered(1)),
            pl.BlockSpec((D_CONV, d), lambda i: (1, 0), pipeline_mode=pl.Buffered(1)),
        ],
        out_specs=pl.BlockSpec((tm, d), lambda i: (i, 0)),
        out_shape=jax.ShapeDtypeStruct((t, d), F32),
        compiler_params=_params(("parallel",)),
        name="outproj",
    )(x, a, c, w_out, w_out)


def _ple_kernel(x_ref, p_ref, g_ref, wg_ref, wp_ref, gf_ref, o_ref):
    x = x_ref[...]
    xn = _rms(x, g_ref[...]).astype(BF16)
    gate = jax.nn.sigmoid(jnp.dot(xn, wg_ref[...].astype(BF16), preferred_element_type=F32))
    proj = jnp.dot(p_ref[...].astype(BF16), wp_ref[...].astype(BF16),
                   preferred_element_type=F32)
    o_ref[...] = _rms(x + gate * proj, gf_ref[...])


def _ple(x, p, g, w_gate, w_proj, gf):
    t, d = x.shape
    tm = PLE_TM
    return pl.pallas_call(
        _ple_kernel,
        grid=(t // tm,),
        in_specs=[
            pl.BlockSpec((tm, d), lambda i: (i, 0)),
            pl.BlockSpec((tm, PLE_DIM), lambda i: (i, 0)),
            pl.BlockSpec((1, d), lambda i: (0, 0)),
            pl.BlockSpec((d, d), lambda i: (0, 0), pipeline_mode=pl.Buffered(1)),
            pl.BlockSpec((PLE_DIM, d), lambda i: (0, 0), pipeline_mode=pl.Buffered(1)),
            pl.BlockSpec((1, d), lambda i: (0, 0)),
        ],
        out_specs=pl.BlockSpec((tm, d), lambda i: (i, 0)),
        out_shape=jax.ShapeDtypeStruct((t, d), F32),
        compiler_params=_params(("parallel",)),
        name="ple",
    )(x, p, g, w_gate, w_proj, gf)


def kernel(x, p, ffn1_norm, ffn1_wg, ffn1_wu, ffn1_wd, mix_norm, w_in, rpb, conv_w, conv_b,
           attn_out_norm, conv_out_norm, w_out, ffn2_norm, ffn2_wg, ffn2_wu, ffn2_wd,
           ple_norm, ple_w_gate, ple_w_proj, final_norm):
    b, s, d = x.shape
    assert ffn1_wg.shape[0] == 1
    assert d == D_MODEL and s % (Q_ROWS * GRID_W) == 0 and s % PROJ_TM == 0
    t = b * s
    h = x.reshape(t, d)
    row = lambda v: v.reshape(1, -1)
    for i in range(1):
        h = _ffn(h, row(ffn1_norm[i]), ffn1_wg[i], ffn1_wu[i], ffn1_wd[i])
        qkv, y_conv = _inproj(h, row(mix_norm[i]), w_in[i], conv_w[i], row(conv_b[i]),
                              row(conv_out_norm[i]), s)
        y_attn = _attn(qkv, _pair_blocks(rpb[i]), row(attn_out_norm[i]), b, s)
        h = _outproj(h, y_attn, y_conv, w_out[i])
        h = _ffn(h, row(ffn2_norm[i]), ffn2_wg[i], ffn2_wu[i], ffn2_wd[i])
        h = _ple(h, p[i].reshape(t, PLE_DIM), row(ple_norm[i]), ple_w_gate[i], ple_w_proj[i],
                 row(final_norm))
    return h.reshape(b, s, d)
```

```python
import functools

import numpy as np
import jax
import jax.numpy as jnp
from jax import lax
from jax.experimental import pallas as pl
from jax.experimental.pallas import tpu as pltpu

D_MODEL = 2048
GRID_W = 64
PLE_DIM = 256
D_ATTN = D_MODEL // 2
D_CONV = D_MODEL - D_ATTN
N_HEADS = 8
HEAD_DIM = D_ATTN // N_HEADS
GROUP_DIM = 128
CONV_W = 3
WIN_ROWS = 8
WIN_COLS = 16
D_FF = 5632
RMS_EPS = 1e-6
NEG = -1e30
LOG2E = 1.4426950408889634
Q_SCALE = HEAD_DIM ** -0.5 * LOG2E

BF16 = jnp.bfloat16
F32 = jnp.float32

VMEM_LIMIT = 56 * 1024 * 1024
FFN_VMEM_LIMIT = 60 * 1024 * 1024
FFN_TM = 1024
FFN_TF = 512
PROJ_TM = 1024
PROJ_TN = 512
CONV_TC = 256
CONV_SUBTILES = 2
HALO = 16
Q_ROWS = 8
K_ROWS = 16
K_CHUNK_ROWS = 4
OUT_TM = 512
PLE_TM = 512


def _rms(x, g):
    ms = jnp.mean(x * x, axis=-1, keepdims=True)
    return x * lax.rsqrt(ms + RMS_EPS) * g


def _params(sem, vmem_limit=VMEM_LIMIT):
    return pltpu.CompilerParams(dimension_semantics=sem, vmem_limit_bytes=vmem_limit)


def _ffn_kernel(x_hbm, g_ref, wg_ref, wu_ref, wd_ref, o_ref, xbuf, xn_ref, sem):
    i = pl.program_id(0)
    j = pl.program_id(1)
    tm = xbuf.shape[0]

    def x_copy(tile):
        rows = pl.ds(pl.multiple_of(tile * tm, tm), tm)
        return pltpu.make_async_copy(x_hbm.at[rows, :], xbuf, sem)

    @pl.when(j == 0)
    def _():
        @pl.when(i == 0)
        def _():
            x_copy(0).start()

        x_copy(i).wait()
        x = xbuf[...]
        xn_ref[...] = _rms(x, g_ref[...]).astype(BF16)
        o_ref[...] = x

    @pl.when((j == 1) & (i + 1 < pl.num_programs(0)))
    def _():
        x_copy(i + 1).start()

    xn = xn_ref[...]
    gate = jnp.dot(xn, wg_ref[...].astype(BF16), preferred_element_type=F32)
    up = jnp.dot(xn, wu_ref[...].astype(BF16), preferred_element_type=F32)
    act = (jax.nn.silu(gate) * up * 0.5).astype(BF16)
    o_ref[...] += jnp.dot(act, wd_ref[...].astype(BF16), preferred_element_type=F32)


def _ffn(x, g, wg, wu, wd):
    t, d = x.shape
    f = wg.shape[1]
    return pl.pallas_call(
        _ffn_kernel,
        grid=(t // FFN_TM, f // FFN_TF),
        in_specs=[
            pl.BlockSpec(memory_space=pl.ANY),
            pl.BlockSpec((1, d), lambda i, j: (0, 0)),
            pl.BlockSpec((d, FFN_TF), lambda i, j: (0, j)),
            pl.BlockSpec((d, FFN_TF), lambda i, j: (0, j)),
            pl.BlockSpec((FFN_TF, d), lambda i, j: (j, 0)),
        ],
        out_specs=pl.BlockSpec((FFN_TM, d), lambda i, j: (i, 0)),
        out_shape=jax.ShapeDtypeStruct((t, d), F32),
        scratch_shapes=[pltpu.VMEM((FFN_TM, d), F32),
                        pltpu.VMEM((FFN_TM, d), BF16),
                        pltpu.SemaphoreType.DMA(())],
        compiler_params=_params(("arbitrary", "arbitrary"), FFN_VMEM_LIMIT),
        name="ffn",
    )(x, g, wg, wu, wd)


N_QKV_STEPS = 3 * D_ATTN // PROJ_TN
N_CONV_STEPS = D_CONV // CONV_TC


def _inproj_kernel(x_ref, xp_ref, xnx_ref, g_ref, wq_ref, wb_ref, wc_ref, wu_ref, cw_ref,
                   cb_ref, gn_ref, qkv_ref, conv_ref, xn_ref, cu_ref, *, seq):
    tm = x_ref.shape[0]
    j = pl.program_id(1)

    @pl.when(j == 0)
    def _():
        g = g_ref[...]
        xn_ref[0:HALO, :] = _rms(xp_ref[...], g).astype(BF16)
        xn_ref[HALO:HALO + tm, :] = _rms(x_ref[...], g).astype(BF16)
        xn_ref[HALO + tm:, :] = _rms(xnx_ref[...], g).astype(BF16)

    @pl.when(j < N_QKV_STEPS)
    def _():
        acc = jnp.dot(xn_ref[HALO:HALO + tm, :], wq_ref[...].astype(BF16),
                      preferred_element_type=F32)
        col_scale = jnp.where(j < D_ATTN // PROJ_TN, Q_SCALE, 1.0)
        qkv_ref[...] = (acc * col_scale).astype(BF16)

    @pl.when(j >= N_QKV_STEPS)
    def _():
        wb = wb_ref[...].astype(BF16)
        wc = wc_ref[...].astype(BF16)
        wu = wu_ref[...].astype(BF16)
        cw = cw_ref[...]
        cb = cb_ref[...]
        gn = gn_ref[...]
        ts = tm // CONV_SUBTILES
        for sub in range(CONV_SUBTILES):
            r0 = sub * ts
            xs = xn_ref[r0:r0 + ts + 2 * HALO, :]
            gate_b = jnp.dot(xn_ref[HALO + r0:HALO + r0 + ts, :], wb, preferred_element_type=F32)
            gate_c = jnp.dot(xs, wc, preferred_element_type=F32)
            u = jnp.dot(xs, wu, preferred_element_type=F32)
            cu_ref[sub] = gate_c * u

            pos = (lax.broadcasted_iota(jnp.int32, (ts, 1), 0) + pl.program_id(0) * tm + r0) % seq
            prev = jnp.where(pos == 0, 0.0, cu_ref[sub, HALO - 1:HALO - 1 + ts, :])
            cur = cu_ref[sub, HALO:HALO + ts, :]
            nxt = jnp.where(pos == seq - 1, 0.0, cu_ref[sub, HALO + 1:HALO + 1 + ts, :])
            y = prev * cw[0:1, :]
            y = y + cur * cw[1:2, :]
            y = y + nxt * cw[2:3, :]
            y = gate_b * (y + cb)
            for c in range(0, y.shape[1], GROUP_DIM):
                sl = slice(c, c + GROUP_DIM)
                conv_ref[r0:r0 + ts, sl] = _rms(y[:, sl], gn[:, sl]).astype(BF16)


def _inproj(x, g, w_in, conv_w, conv_b, gn, seq):
    t, d = x.shape
    tm, tn, tc = PROJ_TM, PROJ_TN, CONV_TC
    hb = tm // HALO
    nhb = t // HALO
    col_b = 3 * D_ATTN // tc
    col_c = col_b + D_CONV // tc
    col_u = col_c + D_CONV // tc
    qstep = lambda j: jnp.minimum(j, N_QKV_STEPS - 1)
    cstep = lambda j: jnp.maximum(j - N_QKV_STEPS, 0)
    return pl.pallas_call(
        functools.partial(_inproj_kernel, seq=seq),
        grid=(t // tm, N_QKV_STEPS + N_CONV_STEPS),
        in_specs=[
            pl.BlockSpec((tm, d), lambda i, j: (i, 0)),
            pl.BlockSpec((HALO, d), lambda i, j: (jnp.maximum(i * hb - 1, 0), 0)),
            pl.BlockSpec((HALO, d), lambda i, j: (jnp.minimum((i + 1) * hb, nhb - 1), 0)),
            pl.BlockSpec((1, d), lambda i, j: (0, 0)),
            pl.BlockSpec((d, tn), lambda i, j: (0, qstep(j))),
            pl.BlockSpec((d, tc), lambda i, j: (0, col_b + cstep(j))),
            pl.BlockSpec((d, tc), lambda i, j: (0, col_c + cstep(j))),
            pl.BlockSpec((d, tc), lambda i, j: (0, col_u + cstep(j))),
            pl.BlockSpec((CONV_W, tc), lambda i, j: (0, cstep(j))),
            pl.BlockSpec((1, tc), lambda i, j: (0, cstep(j))),
            pl.BlockSpec((1, tc), lambda i, j: (0, cstep(j))),
        ],
        out_specs=[pl.BlockSpec((tm, tn), lambda i, j: (i, qstep(j))),
                   pl.BlockSpec((tm, tc), lambda i, j: (i, cstep(j)))],
        out_shape=[jax.ShapeDtypeStruct((t, 3 * D_ATTN), BF16),
                   jax.ShapeDtypeStruct((t, D_CONV), BF16)],
        scratch_shapes=[pltpu.VMEM((tm + 2 * HALO, d), BF16),
                        pltpu.VMEM((CONV_SUBTILES, tm // CONV_SUBTILES + 2 * HALO, tc), F32)],
        compiler_params=_params(("parallel", "arbitrary")),
        name="inproj",
    )(x, x, x, g, w_in, w_in, w_in, w_in, conv_w, conv_b, gn)


N_REL = 2 * WIN_ROWS - 1
PAIR_FULL = 0
PAIR_LEFT_NEG = N_REL - 1
PAIR_RIGHT_NEG = 2 * N_REL - 1
PAIR_NEG = 3 * N_REL - 1
N_PAIR_BLOCKS = 3 * N_REL
N_ROW_PATTERNS = 3
N_PAIRS = K_ROWS // 2


def _pair_blocks(rpb):
    nh, nr, nc = rpb.shape
    w = GRID_W
    c = np.arange(w)
    col_start = np.clip(c - WIN_COLS // 2, 0, w - WIN_COLS)
    in_win = (c[None, :] >= col_start[:, None]) & (c[None, :] < col_start[:, None] + WIN_COLS)
    period = 2 * w
    left = w - WIN_COLS
    u = jnp.pad(rpb, ((0, 0), (0, 0), (left, period - nc - left)))
    toep = jnp.tile(u, (1, 1, w))[..., :w * (period - 1)].reshape(nh, nr, w, period - 1)[..., w - 1:]
    toep = jnp.where(in_win[None, None], toep * LOG2E, NEG)
    neg = jnp.full((nh, nr, w, w), NEG, F32)
    return jnp.concatenate([
        jnp.concatenate([toep[:, :-1], toep[:, 1:]], axis=-1),
        jnp.concatenate([neg, toep], axis=-1),
        jnp.concatenate([toep, neg], axis=-1),
        jnp.concatenate([neg[:, :1], neg[:, :1]], axis=-1)], axis=1)


def _pair_index_table(rows):
    tab = np.zeros((N_ROW_PATTERNS, Q_ROWS, N_PAIRS), np.int32)
    for pat, r0 in enumerate((0, Q_ROWS, rows - Q_ROWS)):
        k0 = min(max(r0 - WIN_ROWS // 2, 0), rows - K_ROWS)
        for i in range(Q_ROWS):
            r = r0 + i
            row_start = min(max(r - WIN_ROWS // 2, 0), rows - WIN_ROWS)
            for m in range(N_PAIRS):
                kr = k0 + 2 * m
                in0 = row_start <= kr < row_start + WIN_ROWS
                in1 = row_start <= kr + 1 < row_start + WIN_ROWS
                a = kr - r + WIN_ROWS - 1
                if in0 and in1:
                    tab[pat, i, m] = PAIR_FULL + a
                elif in1:
                    tab[pat, i, m] = PAIR_LEFT_NEG + a + 1
                elif in0:
                    tab[pat, i, m] = PAIR_RIGHT_NEG + a
                else:
                    tab[pat, i, m] = PAIR_NEG
    return tab


def _chunk_spans(tab):
    pairs_per_chunk = K_CHUNK_ROWS // 2
    spans = []
    for pat in range(tab.shape[0]):
        row = []
        for c in range(K_ROWS // K_CHUNK_ROWS):
            blk = tab[pat, :, c * pairs_per_chunk:(c + 1) * pairs_per_chunk]
            hit = np.nonzero((blk != PAIR_NEG).any(axis=1))[0]
            if hit.size == 0:
                row.append((0, 0))
            else:
                assert hit[-1] - hit[0] + 1 == hit.size
                row.append((int(hit[0]), int(hit[-1]) + 1))
        spans.append(row)
    return spans


def _attn_head(h, tab, spans, q_ref, ks, vs, pb_ref, gn_ref, o_ref):
    hs = slice(h * HEAD_DIM, (h + 1) * HEAD_DIM)
    pairs_per_chunk = K_CHUNK_ROWS // 2
    live = [c for c, (first, last) in enumerate(spans) if last > first]
    w = GRID_W
    scores = {}
    m = [None] * Q_ROWS
    for c in live:
        first, last = spans[c]
        s = lax.dot_general(q_ref[first * w:last * w, hs], ks[c][:, hs],
                            (((1,), (1,)), ((), ())), preferred_element_type=F32)
        for i in range(first, last):
            bias = jnp.concatenate([pb_ref[h, int(tab[i, c * pairs_per_chunk + k])]
                                    for k in range(pairs_per_chunk)], axis=1)
            si = s[(i - first) * w:(i - first + 1) * w] + bias
            scores[c, i] = si
            mi = jnp.max(si, axis=1, keepdims=True)
            m[i] = mi if m[i] is None else jnp.maximum(m[i], mi)
    l = [None] * Q_ROWS
    o = [None] * Q_ROWS
    for c in live:
        first, last = spans[c]
        ps = []
        for i in range(first, last):
            p = jnp.exp2(scores[c, i] - m[i])
            li = jnp.sum(p, axis=1, keepdims=True)
            l[i] = li if l[i] is None else l[i] + li
            ps.append(p.astype(BF16))
        oc = jnp.dot(jnp.concatenate(ps, axis=0), vs[c][:, hs], preferred_element_type=F32)
        for i in range(first, last):
            oi = oc[(i - first) * w:(i - first + 1) * w]
            o[i] = oi if o[i] is None else o[i] + oi
    gn = gn_ref[:, hs]
    for i in range(Q_ROWS):
        o_ref[i * w:(i + 1) * w, hs] = _rms(o[i] / l[i], gn).astype(BF16)


def _attn_kernel(q_ref, k0, k1, k2, k3, v0, v1, v2, v3, pb_ref, gn_ref, o_ref, *, tab, spans):
    rb = pl.program_id(1)
    pat = jnp.where(rb == 0, 0, jnp.where(rb == pl.num_programs(1) - 1, 2, 1))
    for pat_id in range(N_ROW_PATTERNS):
        @pl.when(pat == pat_id)
        def _(pat_id=pat_id):
            for h in range(N_HEADS):
                _attn_head(h, tab[pat_id], spans[pat_id], q_ref, (k0, k1, k2, k3),
                           (v0, v1, v2, v3), pb_ref, gn_ref, o_ref)


def _attn(qkv, pair_blocks, gn, batch, seq):
    rows = seq // GRID_W
    tq = Q_ROWS * GRID_W
    kc = K_CHUNK_ROWS * GRID_W
    n_rb = rows // Q_ROWS
    n_kc = K_ROWS // K_CHUNK_ROWS
    chunks_per_seq = seq // kc
    tab = _pair_index_table(rows)

    def q_map(b, rb):
        return (b * n_rb + rb, 0)

    def kv_map(col, c):
        def f(b, rb):
            first = jnp.clip(rb * (Q_ROWS // K_CHUNK_ROWS) - (WIN_ROWS // 2) // K_CHUNK_ROWS,
                             0, chunks_per_seq - n_kc)
            return (b * chunks_per_seq + first + c, col)
        return f

    k_specs = [pl.BlockSpec((kc, D_ATTN), kv_map(1, c)) for c in range(n_kc)]
    v_specs = [pl.BlockSpec((kc, D_ATTN), kv_map(2, c)) for c in range(n_kc)]
    return pl.pallas_call(
        functools.partial(_attn_kernel, tab=tab, spans=_chunk_spans(tab)),
        grid=(batch, n_rb),
        in_specs=[pl.BlockSpec((tq, D_ATTN), q_map)] + k_specs + v_specs + [
            pl.BlockSpec(pair_blocks.shape, lambda b, rb: (0, 0, 0, 0),
                         pipeline_mode=pl.Buffered(1)),
            pl.BlockSpec((1, D_ATTN), lambda b, rb: (0, 0)),
        ],
        out_specs=pl.BlockSpec((tq, D_ATTN), q_map),
        out_shape=jax.ShapeDtypeStruct((batch * seq, D_ATTN), BF16),
        compiler_params=_params(("parallel", "arbitrary")),
        name="attn",
    )(qkv, *([qkv] * (2 * n_kc)), pair_blocks, gn)


def _outproj_kernel(x_ref, a_ref, c_ref, wa_ref, wc_ref, o_ref):
    acc = jnp.dot(a_ref[...], wa_ref[...].astype(BF16), preferred_element_type=F32)
    acc = acc + jnp.dot(c_ref[...], wc_ref[...].astype(BF16), preferred_element_type=F32)
    o_ref[...] = x_ref[...] + acc


def _outproj(x, a, c, w_out):
    t, d = x.shape
    tm = OUT_TM
    return pl.pallas_call(
        _outproj_kernel,
        grid=(t // tm,),
        in_specs=[
            pl.BlockSpec((tm, d), lambda i: (i, 0)),
            pl.BlockSpec((tm, D_ATTN), lambda i: (i, 0)),
            pl.BlockSpec((tm, D_CONV), lambda i: (i, 0)),
            pl.BlockSpec((D_ATTN, d), lambda i: (0, 0), pipeline_mode=pl.Buffered(1)),
            pl.BlockSpec((D_CONV, d), lambda i: (1, 0), pipeline_mode=pl.Buffered(1)),
        ],
        out_specs=pl.BlockSpec((tm, d), lambda i: (i, 0)),
        out_shape=jax.ShapeDtypeStruct((t, d), F32),
        compiler_params=_params(("parallel",)),
        name="outproj",
    )(x, a, c, w_out, w_out)


def _ple_kernel(x_ref, p_ref, g_ref, wg_ref, wp_ref, gf_ref, o_ref):
    x = x_ref[...]
    xn = _rms(x, g_ref[...]).astype(BF16)
    gate = jax.nn.sigmoid(jnp.dot(xn, wg_ref[...].astype(BF16), preferred_element_type=F32))
    proj = jnp.dot(p_ref[...].astype(BF16), wp_ref[...].astype(BF16),
                   preferred_element_type=F32)
    o_ref[...] = _rms(x + gate * proj, gf_ref[...])


def _ple(x, p, g, w_gate, w_proj, gf):
    t, d = x.shape
    tm = PLE_TM
    return pl.pallas_call(
        _ple_kernel,
        grid=(t // tm,),
        in_specs=[
            pl.BlockSpec((tm, d), lambda i: (i, 0)),
            pl.BlockSpec((tm, PLE_DIM), lambda i: (i, 0)),
            pl.BlockSpec((1, d), lambda i: (0, 0)),
            pl.BlockSpec((d, d), lambda i: (0, 0), pipeline_mode=pl.Buffered(1)),
            pl.BlockSpec((PLE_DIM, d), lambda i: (0, 0), pipeline_mode=pl.Buffered(1)),
            pl.BlockSpec((1, d), lambda i: (0, 0)),
        ],
        out_specs=pl.BlockSpec((tm, d), lambda i: (i, 0)),
        out_shape=jax.ShapeDtypeStruct((t, d), F32),
        compiler_params=_params(("parallel",)),
        name="ple",
    )(x, p, g, w_gate, w_proj, gf)


def kernel(x, p, ffn1_norm, ffn1_wg, ffn1_wu, ffn1_wd, mix_norm, w_in, rpb, conv_w, conv_b,
           attn_out_norm, conv_out_norm, w_out, ffn2_norm, ffn2_wg, ffn2_wu, ffn2_wd,
           ple_norm, ple_w_gate, ple_w_proj, final_norm):
    b, s, d = x.shape
    assert ffn1_wg.shape[0] == 1
    assert d == D_MODEL and s % (Q_ROWS * GRID_W) == 0 and s % PROJ_TM == 0
    t = b * s
    h = x.reshape(t, d)
    row = lambda v: v.reshape(1, -1)
    for i in range(1):
        h = _ffn(h, row(ffn1_norm[i]), ffn1_wg[i], ffn1_wu[i], ffn1_wd[i])
        qkv, y_conv = _inproj(h, row(mix_norm[i]), w_in[i], conv_w[i], row(conv_b[i]),
                              row(conv_out_norm[i]), s)
        y_attn = _attn(qkv, _pair_blocks(rpb[i]), row(attn_out_norm[i]), b, s)
        h = _outproj(h, y_attn, y_conv, w_out[i])
        h = _ffn(h, row(ffn2_norm[i]), ffn2_wg[i], ffn2_wu[i], ffn2_wd[i])
        h = _ple(h, p[i].reshape(t, PLE_DIM), row(ple_norm[i]), ple_w_gate[i], ple_w_proj[i],
                 row(final_norm))
    return h.reshape(b, s, d)
```

```python
import functools

import numpy as np
import jax
import jax.numpy as jnp
from jax import lax
from jax.experimental import pallas as pl
from jax.experimental.pallas import tpu as pltpu

D_MODEL = 2048
GRID_W = 64
PLE_DIM = 256
D_ATTN = D_MODEL // 2
D_CONV = D_MODEL - D_ATTN
N_HEADS = 8
HEAD_DIM = D_ATTN // N_HEADS
GROUP_DIM = 128
CONV_W = 3
WIN_ROWS = 8
WIN_COLS = 16
RMS_EPS = 1e-6
NEG = -1e30
LOG2E = 1.4426950408889634
Q_SCALE = HEAD_DIM ** -0.5 * LOG2E

BF16 = jnp.bfloat16
F32 = jnp.float32

VMEM_LIMIT = 56 * 1024 * 1024
BIG_VMEM_LIMIT = 60 * 1024 * 1024
FFN_TM = 1024
FFN_TF = 512
PROJ_TM = 1024
PROJ_TN = 1024
CONV_TC = 256
CONV_SUBTILES = 2
HALO = 16
Q_ROWS = 8
K_ROWS = 16
K_CHUNK_ROWS = 4
OUT_TM = 512
PLE_TM = 512


def _rms(x, g):
    ms = jnp.mean(x * x, axis=-1, keepdims=True)
    return x * lax.rsqrt(ms + RMS_EPS) * g


def _params(sem, vmem_limit=VMEM_LIMIT):
    return pltpu.CompilerParams(dimension_semantics=sem, vmem_limit_bytes=vmem_limit)


def _ffn_kernel(x_hbm, g_ref, wg_ref, wu_ref, wd_ref, o_ref, xbuf, xn_ref, sem):
    i = pl.program_id(0)
    j = pl.program_id(1)
    tm = xbuf.shape[0]

    def x_copy(tile):
        rows = pl.ds(pl.multiple_of(tile * tm, tm), tm)
        return pltpu.make_async_copy(x_hbm.at[rows, :], xbuf, sem)

    @pl.when(j == 0)
    def _():
        @pl.when(i == 0)
        def _():
            x_copy(0).start()

        x_copy(i).wait()
        x = xbuf[...]
        xn_ref[...] = _rms(x, g_ref[...]).astype(BF16)
        o_ref[...] = x

    @pl.when((j == 1) & (i + 1 < pl.num_programs(0)))
    def _():
        x_copy(i + 1).start()

    xn = xn_ref[...]
    gate = jnp.dot(xn, wg_ref[...].astype(BF16), preferred_element_type=F32)
    up = jnp.dot(xn, wu_ref[...].astype(BF16), preferred_element_type=F32)
    act = (jax.nn.silu(gate) * up * 0.5).astype(BF16)
    o_ref[...] += jnp.dot(act, wd_ref[...].astype(BF16), preferred_element_type=F32)


def _ffn(x, g, wg, wu, wd):
    t, d = x.shape
    f = wg.shape[1]
    return pl.pallas_call(
        _ffn_kernel,
        grid=(t // FFN_TM, f // FFN_TF),
        in_specs=[
            pl.BlockSpec(memory_space=pl.ANY),
            pl.BlockSpec((1, d), lambda i, j: (0, 0)),
            pl.BlockSpec((d, FFN_TF), lambda i, j: (0, j)),
            pl.BlockSpec((d, FFN_TF), lambda i, j: (0, j)),
            pl.BlockSpec((FFN_TF, d), lambda i, j: (j, 0)),
        ],
        out_specs=pl.BlockSpec((FFN_TM, d), lambda i, j: (i, 0)),
        out_shape=jax.ShapeDtypeStruct((t, d), F32),
        scratch_shapes=[pltpu.VMEM((FFN_TM, d), F32),
                        pltpu.VMEM((FFN_TM, d), BF16),
                        pltpu.SemaphoreType.DMA(())],
        compiler_params=_params(("arbitrary", "arbitrary"), BIG_VMEM_LIMIT),
        name="ffn",
    )(x, g, wg, wu, wd)


N_QKV_STEPS = 3 * D_ATTN // PROJ_TN
N_CONV_STEPS = D_CONV // CONV_TC


def _inproj_kernel(x_ref, xp_ref, xnx_ref, g_ref, wq_ref, wb_ref, wc_ref, wu_ref, cw_ref,
                   cb_ref, gn_ref, qkv_ref, conv_ref, xn_ref, cu_ref, *, seq):
    tm = x_ref.shape[0]
    j = pl.program_id(1)

    @pl.when(j == 0)
    def _():
        g = g_ref[...]
        xn_ref[0:HALO, :] = _rms(xp_ref[...], g).astype(BF16)
        xn_ref[HALO:HALO + tm, :] = _rms(x_ref[...], g).astype(BF16)
        xn_ref[HALO + tm:, :] = _rms(xnx_ref[...], g).astype(BF16)

    @pl.when(j < N_QKV_STEPS)
    def _():
        acc = jnp.dot(xn_ref[HALO:HALO + tm, :], wq_ref[...].astype(BF16),
                      preferred_element_type=F32)
        col_scale = jnp.where(j < D_ATTN // PROJ_TN, Q_SCALE, 1.0)
        qkv_ref[...] = (acc * col_scale).astype(BF16)

    @pl.when(j >= N_QKV_STEPS)
    def _():
        wb = wb_ref[...].astype(BF16)
        wc = wc_ref[...].astype(BF16)
        wu = wu_ref[...].astype(BF16)
        cw = cw_ref[...]
        cb = cb_ref[...]
        gn = gn_ref[...]
        ts = tm // CONV_SUBTILES
        for sub in range(CONV_SUBTILES):
            r0 = sub * ts
            xs = xn_ref[r0:r0 + ts + 2 * HALO, :]
            gate_b = jnp.dot(xn_ref[HALO + r0:HALO + r0 + ts, :], wb, preferred_element_type=F32)
            gate_c = jnp.dot(xs, wc, preferred_element_type=F32)
            u = jnp.dot(xs, wu, preferred_element_type=F32)
            cu_ref[sub] = gate_c * u

            pos = (lax.broadcasted_iota(jnp.int32, (ts, 1), 0) + pl.program_id(0) * tm + r0) % seq
            prev = jnp.where(pos == 0, 0.0, cu_ref[sub, HALO - 1:HALO - 1 + ts, :])
            cur = cu_ref[sub, HALO:HALO + ts, :]
            nxt = jnp.where(pos == seq - 1, 0.0, cu_ref[sub, HALO + 1:HALO + 1 + ts, :])
            y = prev * cw[0:1, :]
            y = y + cur * cw[1:2, :]
            y = y + nxt * cw[2:3, :]
            y = gate_b * (y + cb)
            for c in range(0, y.shape[1], GROUP_DIM):
                sl = slice(c, c + GROUP_DIM)
                conv_ref[r0:r0 + ts, sl] = _rms(y[:, sl], gn[:, sl]).astype(BF16)


def _inproj(x, g, w_in, conv_w, conv_b, gn, seq):
    t, d = x.shape
    tm, tn, tc = PROJ_TM, PROJ_TN, CONV_TC
    hb = tm // HALO
    nhb = t // HALO
    col_b = 3 * D_ATTN // tc
    col_c = col_b + D_CONV // tc
    col_u = col_c + D_CONV // tc
    qstep = lambda j: jnp.minimum(j, N_QKV_STEPS - 1)
    cstep = lambda j: jnp.maximum(j - N_QKV_STEPS, 0)
    return pl.pallas_call(
        functools.partial(_inproj_kernel, seq=seq),
        grid=(t // tm, N_QKV_STEPS + N_CONV_STEPS),
        in_specs=[
            pl.BlockSpec((tm, d), lambda i, j: (i, 0)),
            pl.BlockSpec((HALO, d), lambda i, j: (jnp.maximum(i * hb - 1, 0), 0)),
            pl.BlockSpec((HALO, d), lambda i, j: (jnp.minimum((i + 1) * hb, nhb - 1), 0)),
            pl.BlockSpec((1, d), lambda i, j: (0, 0)),
            pl.BlockSpec((d, tn), lambda i, j: (0, qstep(j))),
            pl.BlockSpec((d, tc), lambda i, j: (0, col_b + cstep(j))),
            pl.BlockSpec((d, tc), lambda i, j: (0, col_c + cstep(j))),
            pl.BlockSpec((d, tc), lambda i, j: (0, col_u + cstep(j))),
            pl.BlockSpec((CONV_W, tc), lambda i, j: (0, cstep(j))),
            pl.BlockSpec((1, tc), lambda i, j: (0, cstep(j))),
            pl.BlockSpec((1, tc), lambda i, j: (0, cstep(j))),
        ],
        out_specs=[pl.BlockSpec((tm, tn), lambda i, j: (i, qstep(j))),
                   pl.BlockSpec((tm, tc), lambda i, j: (i, cstep(j)))],
        out_shape=[jax.ShapeDtypeStruct((t, 3 * D_ATTN), BF16),
                   jax.ShapeDtypeStruct((t, D_CONV), BF16)],
        scratch_shapes=[pltpu.VMEM((tm + 2 * HALO, d), BF16),
                        pltpu.VMEM((CONV_SUBTILES, tm // CONV_SUBTILES + 2 * HALO, tc), F32)],
        compiler_params=_params(("parallel", "arbitrary"), BIG_VMEM_LIMIT),
        name="inproj",
    )(x, x, x, g, w_in, w_in, w_in, w_in, conv_w, conv_b, gn)


N_REL = 2 * WIN_ROWS - 1
PAIR_FULL = 0
PAIR_LEFT_NEG = N_REL - 1
PAIR_RIGHT_NEG = 2 * N_REL - 1
PAIR_NEG = 3 * N_REL - 1
N_PAIR_BLOCKS = 3 * N_REL
N_ROW_PATTERNS = 3
N_PAIRS = K_ROWS // 2


def _pair_blocks(rpb):
    nh, nr, nc = rpb.shape
    w = GRID_W
    c = np.arange(w)
    col_start = np.clip(c - WIN_COLS // 2, 0, w - WIN_COLS)
    in_win = (c[None, :] >= col_start[:, None]) & (c[None, :] < col_start[:, None] + WIN_COLS)
    period = 2 * w
    left = w - WIN_COLS
    u = jnp.pad(rpb, ((0, 0), (0, 0), (left, period - nc - left)))
    toep = jnp.tile(u, (1, 1, w))[..., :w * (period - 1)].reshape(nh, nr, w, period - 1)[..., w - 1:]
    toep = jnp.where(in_win[None, None], toep * LOG2E, NEG)
    neg = jnp.full((nh, nr, w, w), NEG, F32)
    return jnp.concatenate([
        jnp.concatenate([toep[:, :-1], toep[:, 1:]], axis=-1),
        jnp.concatenate([neg, toep], axis=-1),
        jnp.concatenate([toep, neg], axis=-1),
        jnp.concatenate([neg[:, :1], neg[:, :1]], axis=-1)], axis=1)


def _pair_index_table(rows):
    tab = np.zeros((N_ROW_PATTERNS, Q_ROWS, N_PAIRS), np.int32)
    for pat, r0 in enumerate((0, Q_ROWS, rows - Q_ROWS)):
        k0 = min(max(r0 - WIN_ROWS // 2, 0), rows - K_ROWS)
        for i in range(Q_ROWS):
            r = r0 + i
            row_start = min(max(r - WIN_ROWS // 2, 0), rows - WIN_ROWS)
            for m in range(N_PAIRS):
                kr = k0 + 2 * m
                in0 = row_start <= kr < row_start + WIN_ROWS
                in1 = row_start <= kr + 1 < row_start + WIN_ROWS
                a = kr - r + WIN_ROWS - 1
                if in0 and in1:
                    tab[pat, i, m] = PAIR_FULL + a
                elif in1:
                    tab[pat, i, m] = PAIR_LEFT_NEG + a + 1
                elif in0:
                    tab[pat, i, m] = PAIR_RIGHT_NEG + a
                else:
                    tab[pat, i, m] = PAIR_NEG
    return tab


def _chunk_spans(tab):
    pairs_per_chunk = K_CHUNK_ROWS // 2
    spans = []
    for pat in range(tab.shape[0]):
        row = []
        for c in range(K_ROWS // K_CHUNK_ROWS):
            blk = tab[pat, :, c * pairs_per_chunk:(c + 1) * pairs_per_chunk]
            hit = np.nonzero((blk != PAIR_NEG).any(axis=1))[0]
            if hit.size == 0:
                row.append((0, 0))
            else:
                assert hit[-1] - hit[0] + 1 == hit.size
                row.append((int(hit[0]), int(hit[-1]) + 1))
        spans.append(row)
    return spans


def _attn_head(h, tab, spans, q_ref, ks, vs, pb_ref, gn_ref, o_ref):
    hs = slice(h * HEAD_DIM, (h + 1) * HEAD_DIM)
    pairs_per_chunk = K_CHUNK_ROWS // 2
    live = [c for c, (first, last) in enumerate(spans) if last > first]
    w = GRID_W
    scores = {}
    m = [None] * Q_ROWS
    for c in live:
        first, last = spans[c]
        s = lax.dot_general(q_ref[first * w:last * w, hs], ks[c][:, hs],
                            (((1,), (1,)), ((), ())), preferred_element_type=F32)
        for i in range(first, last):
            bias = jnp.concatenate([pb_ref[h, int(tab[i, c * pairs_per_chunk + k])]
                                    for k in range(pairs_per_chunk)], axis=1)
            si = s[(i - first) * w:(i - first + 1) * w] + bias
            scores[c, i] = si
            mi = jnp.max(si, axis=1, keepdims=True)
            m[i] = mi if m[i] is None else jnp.maximum(m[i], mi)
    l = [None] * Q_ROWS
    o = [None] * Q_ROWS
    for c in live:
        first, last = spans[c]
        ps = []
        for i in range(first, last):
            p = jnp.exp2(scores[c, i] - m[i])
            li = jnp.sum(p, axis=1, keepdims=True)
            l[i] = li if l[i] is None else l[i] + li
            ps.append(p.astype(BF16))
        oc = jnp.dot(jnp.concatenate(ps, axis=0), vs[c][:, hs], preferred_element_type=F32)
        for i in range(first, last):
            oi = oc[(i - first) * w:(i - first + 1) * w]
            o[i] = oi if o[i] is None else o[i] + oi
    gn = gn_ref[:, hs]
    for i in range(Q_ROWS):
        o_ref[i * w:(i + 1) * w, hs] = _rms(o[i] / l[i], gn).astype(BF16)


def _attn_kernel(q_ref, k0, k1, k2, k3, v0, v1, v2, v3, pb_ref, gn_ref, o_ref, *, tab, spans):
    rb = pl.program_id(1)
    pat = jnp.where(rb == 0, 0, jnp.where(rb == pl.num_programs(1) - 1, 2, 1))
    for pat_id in range(N_ROW_PATTERNS):
        @pl.when(pat == pat_id)
        def _(pat_id=pat_id):
            for h in range(N_HEADS):
                _attn_head(h, tab[pat_id], spans[pat_id], q_ref, (k0, k1, k2, k3),
                           (v0, v1, v2, v3), pb_ref, gn_ref, o_ref)


def _attn(qkv, pair_blocks, gn, batch, seq):
    rows = seq // GRID_W
    tq = Q_ROWS * GRID_W
    kc = K_CHUNK_ROWS * GRID_W
    n_rb = rows // Q_ROWS
    n_kc = K_ROWS // K_CHUNK_ROWS
    chunks_per_seq = seq // kc
    tab = _pair_index_table(rows)

    def q_map(b, rb):
        return (b * n_rb + rb, 0)

    def kv_map(col, c):
        def f(b, rb):
            first = jnp.clip(rb * (Q_ROWS // K_CHUNK_ROWS) - (WIN_ROWS // 2) // K_CHUNK_ROWS,
                             0, chunks_per_seq - n_kc)
            return (b * chunks_per_seq + first + c, col)
        return f

    k_specs = [pl.BlockSpec((kc, D_ATTN), kv_map(1, c)) for c in range(n_kc)]
    v_specs = [pl.BlockSpec((kc, D_ATTN), kv_map(2, c)) for c in range(n_kc)]
    return pl.pallas_call(
        functools.partial(_attn_kernel, tab=tab, spans=_chunk_spans(tab)),
        grid=(batch, n_rb),
        in_specs=[pl.BlockSpec((tq, D_ATTN), q_map)] + k_specs + v_specs + [
            pl.BlockSpec(pair_blocks.shape, lambda b, rb: (0, 0, 0, 0),
                         pipeline_mode=pl.Buffered(1)),
            pl.BlockSpec((1, D_ATTN), lambda b, rb: (0, 0)),
        ],
        out_specs=pl.BlockSpec((tq, D_ATTN), q_map),
        out_shape=jax.ShapeDtypeStruct((batch * seq, D_ATTN), BF16),
        compiler_params=_params(("parallel", "arbitrary")),
        name="attn",
    )(qkv, *([qkv] * (2 * n_kc)), pair_blocks, gn)


def _outproj_kernel(x_ref, a_ref, c_ref, wa_ref, wc_ref, o_ref):
    acc = jnp.dot(a_ref[...], wa_ref[...].astype(BF16), preferred_element_type=F32)
    acc = acc + jnp.dot(c_ref[...], wc_ref[...].astype(BF16), preferred_element_type=F32)
    o_ref[...] = x_ref[...] + acc


def _outproj(x, a, c, w_out):
    t, d = x.shape
    tm = OUT_TM
    return pl.pallas_call(
        _outproj_kernel,
        grid=(t // tm,),
        in_specs=[
            pl.BlockSpec((tm, d), lambda i: (i, 0)),
            pl.BlockSpec((tm, D_ATTN), lambda i: (i, 0)),
            pl.BlockSpec((tm, D_CONV), lambda i: (i, 0)),
            pl.BlockSpec((D_ATTN, d), lambda i: (0, 0), pipeline_mode=pl.Buffered(1)),
            pl.BlockSpec((D_CONV, d), lambda i: (1, 0), pipeline_mode=pl.Buffered(1)),
        ],
        out_specs=pl.BlockSpec((tm, d), lambda i: (i, 0)),
        out_shape=jax.ShapeDtypeStruct((t, d), F32),
        compiler_params=_params(("parallel",)),
        name="outproj",
    )(x, a, c, w_out, w_out)


def _ple_kernel(x_ref, p_ref, g_ref, wg_ref, wp_ref, gf_ref, o_ref):
    x = x_ref[...]
    xn = _rms(x, g_ref[...]).astype(BF16)
    gate = jax.nn.sigmoid(jnp.dot(xn, wg_ref[...].astype(BF16), preferred_element_type=F32))
    proj = jnp.dot(p_ref[...].astype(BF16), wp_ref[...].astype(BF16),
                   preferred_element_type=F32)
    o_ref[...] = _rms(x + gate * proj, gf_ref[...])


def _ple(x, p, g, w_gate, w_proj, gf):
    t, d = x.shape
    tm = PLE_TM
    return pl.pallas_call(
        _ple_kernel,
        grid=(t // tm,),
        in_specs=[
            pl.BlockSpec((tm, d), lambda i: (i, 0)),
            pl.BlockSpec((tm, PLE_DIM), lambda i: (i, 0)),
            pl.BlockSpec((1, d), lambda i: (0, 0)),
            pl.BlockSpec((d, d), lambda i: (0, 0), pipeline_mode=pl.Buffered(1)),
            pl.BlockSpec((PLE_DIM, d), lambda i: (0, 0), pipeline_mode=pl.Buffered(1)),
            pl.BlockSpec((1, d), lambda i: (0, 0)),
        ],
        out_specs=pl.BlockSpec((tm, d), lambda i: (i, 0)),
        out_shape=jax.ShapeDtypeStruct((t, d), F32),
        compiler_params=_params(("parallel",)),
        name="ple",
    )(x, p, g, w_gate, w_proj, gf)


def kernel(x, p, ffn1_norm, ffn1_wg, ffn1_wu, ffn1_wd, mix_norm, w_in, rpb, conv_w, conv_b,
           attn_out_norm, conv_out_norm, w_out, ffn2_norm, ffn2_wg, ffn2_wu, ffn2_wd,
           ple_norm, ple_w_gate, ple_w_proj, final_norm):
    b, s, d = x.shape
    assert ffn1_wg.shape[0] == 1
    assert d == D_MODEL and s % (Q_ROWS * GRID_W) == 0 and s % PROJ_TM == 0
    t = b * s
    h = x.reshape(t, d)
    row = lambda v: v.reshape(1, -1)
    for i in range(1):
        h = _ffn(h, row(ffn1_norm[i]), ffn1_wg[i], ffn1_wu[i], ffn1_wd[i])
        qkv, y_conv = _inproj(h, row(mix_norm[i]), w_in[i], conv_w[i], row(conv_b[i]),
                              row(conv_out_norm[i]), s)
        y_attn = _attn(qkv, _pair_blocks(rpb[i]), row(attn_out_norm[i]), b, s)
        h = _outproj(h, y_attn, y_conv, w_out[i])
        h = _ffn(h, row(ffn2_norm[i]), ffn2_wg[i], ffn2_wu[i], ffn2_wd[i])
        h = _ple(h, p[i].reshape(t, PLE_DIM), row(ple_norm[i]), ple_w_gate[i], ple_w_proj[i],
                 row(final_norm))
    return h.reshape(b, s, d)
```

```python
import functools

import numpy as np
import jax
import jax.numpy as jnp
from jax import lax
from jax.experimental import pallas as pl
from jax.experimental.pallas import tpu as pltpu

D_MODEL = 2048
GRID_W = 64
PLE_DIM = 256
D_ATTN = D_MODEL // 2
D_CONV = D_MODEL - D_ATTN
N_HEADS = 8
HEAD_DIM = D_ATTN // N_HEADS
GROUP_DIM = 128
CONV_W = 3
WIN_ROWS = 8
WIN_COLS = 16
RMS_EPS = 1e-6
NEG = -1e30
LOG2E = 1.4426950408889634
Q_SCALE = HEAD_DIM ** -0.5 * LOG2E

BF16 = jnp.bfloat16
F32 = jnp.float32

VMEM_LIMIT = 56 * 1024 * 1024
BIG_VMEM_LIMIT = 60 * 1024 * 1024
FFN_TM = 1024
FFN_TF = 512
PROJ_TM = 1024
PROJ_TN = 1024
CONV_TC = 256
CONV_SUBTILES = 2
HALO = 16
Q_ROWS = 8
K_ROWS = 16
K_CHUNK_ROWS = 4
OUT_TM = 512
PLE_TM = 512


def _rms(x, g):
    ms = jnp.mean(x * x, axis=-1, keepdims=True)
    return x * lax.rsqrt(ms + RMS_EPS) * g


def _params(sem, vmem_limit=VMEM_LIMIT):
    return pltpu.CompilerParams(dimension_semantics=sem, vmem_limit_bytes=vmem_limit)


def _ffn_kernel(x_hbm, g_ref, wg_ref, wu_ref, wd_ref, o_ref, xbuf, xn_ref, sem):
    i = pl.program_id(0)
    j = pl.program_id(1)
    tm = xbuf.shape[0]

    def x_copy(tile):
        rows = pl.ds(pl.multiple_of(tile * tm, tm), tm)
        return pltpu.make_async_copy(x_hbm.at[rows, :], xbuf, sem)

    @pl.when(j == 0)
    def _():
        @pl.when(i == 0)
        def _():
            x_copy(0).start()

        x_copy(i).wait()
        x = xbuf[...]
        xn_ref[...] = _rms(x, g_ref[...]).astype(BF16)
        o_ref[...] = x

    @pl.when((j == 1) & (i + 1 < pl.num_programs(0)))
    def _():
        x_copy(i + 1).start()

    xn = xn_ref[...]
    gate = jnp.dot(xn, wg_ref[...].astype(BF16), preferred_element_type=F32)
    up = jnp.dot(xn, wu_ref[...].astype(BF16), preferred_element_type=F32)
    act = (jax.nn.silu(gate) * up * 0.5).astype(BF16)
    o_ref[...] += jnp.dot(act, wd_ref[...].astype(BF16), preferred_element_type=F32)


def _ffn(x, g, wg, wu, wd):
    t, d = x.shape
    f = wg.shape[1]
    return pl.pallas_call(
        _ffn_kernel,
        grid=(t // FFN_TM, f // FFN_TF),
        in_specs=[
            pl.BlockSpec(memory_space=pl.ANY),
            pl.BlockSpec((1, d), lambda i, j: (0, 0)),
            pl.BlockSpec((d, FFN_TF), lambda i, j: (0, j)),
            pl.BlockSpec((d, FFN_TF), lambda i, j: (0, j)),
            pl.BlockSpec((FFN_TF, d), lambda i, j: (j, 0)),
        ],
        out_specs=pl.BlockSpec((FFN_TM, d), lambda i, j: (i, 0)),
        out_shape=jax.ShapeDtypeStruct((t, d), F32),
        scratch_shapes=[pltpu.VMEM((FFN_TM, d), F32),
                        pltpu.VMEM((FFN_TM, d), BF16),
                        pltpu.SemaphoreType.DMA(())],
        compiler_params=_params(("arbitrary", "arbitrary"), BIG_VMEM_LIMIT),
        name="ffn",
    )(x, g, wg, wu, wd)


N_QKV_STEPS = 3 * D_ATTN // PROJ_TN
N_CONV_STEPS = D_CONV // CONV_TC


def _inproj_kernel(x_ref, xp_ref, xnx_ref, g_ref, wq_ref, wb_ref, wc_ref, wu_ref, cw_ref,
                   cb_ref, gn_ref, qkv_ref, conv_ref, xn_ref, cu_ref, *, seq):
    tm = x_ref.shape[0]
    j = pl.program_id(1)

    @pl.when(j == 0)
    def _():
        g = g_ref[...]
        xn_ref[0:HALO, :] = _rms(xp_ref[...], g).astype(BF16)
        xn_ref[HALO:HALO + tm, :] = _rms(x_ref[...], g).astype(BF16)
        xn_ref[HALO + tm:, :] = _rms(xnx_ref[...], g).astype(BF16)

    @pl.when(j < N_QKV_STEPS)
    def _():
        acc = jnp.dot(xn_ref[HALO:HALO + tm, :], wq_ref[...].astype(BF16),
                      preferred_element_type=F32)
        col_scale = jnp.where(j < D_ATTN // PROJ_TN, Q_SCALE, 1.0)
        qkv_ref[...] = (acc * col_scale).astype(BF16)

    @pl.when(j >= N_QKV_STEPS)
    def _():
        wb = wb_ref[...].astype(BF16)
        wc = wc_ref[...].astype(BF16)
        wu = wu_ref[...].astype(BF16)
        cw = cw_ref[...]
        cb = cb_ref[...]
        gn = gn_ref[...]
        ts = tm // CONV_SUBTILES
        for sub in range(CONV_SUBTILES):
            r0 = sub * ts
            xs = xn_ref[r0:r0 + ts + 2 * HALO, :]
            gate_b = jnp.dot(xn_ref[HALO + r0:HALO + r0 + ts, :], wb, preferred_element_type=F32)
            gate_c = jnp.dot(xs, wc, preferred_element_type=F32)
            u = jnp.dot(xs, wu, preferred_element_type=F32)
            cu_ref[sub] = gate_c * u

            pos = (lax.broadcasted_iota(jnp.int32, (ts, 1), 0) + pl.program_id(0) * tm + r0) % seq
            prev = jnp.where(pos == 0, 0.0, cu_ref[sub, HALO - 1:HALO - 1 + ts, :])
            cur = cu_ref[sub, HALO:HALO + ts, :]
            nxt = jnp.where(pos == seq - 1, 0.0, cu_ref[sub, HALO + 1:HALO + 1 + ts, :])
            y = prev * cw[0:1, :]
            y = y + cur * cw[1:2, :]
            y = y + nxt * cw[2:3, :]
            y = gate_b * (y + cb)
            for c in range(0, y.shape[1], GROUP_DIM):
                sl = slice(c, c + GROUP_DIM)
                conv_ref[r0:r0 + ts, sl] = _rms(y[:, sl], gn[:, sl]).astype(BF16)


def _inproj(x, g, w_in, conv_w, conv_b, gn, seq):
    t, d = x.shape
    tm, tn, tc = PROJ_TM, PROJ_TN, CONV_TC
    hb = tm // HALO
    nhb = t // HALO
    col_b = 3 * D_ATTN // tc
    col_c = col_b + D_CONV // tc
    col_u = col_c + D_CONV // tc
    qstep = lambda j: jnp.minimum(j, N_QKV_STEPS - 1)
    cstep = lambda j: jnp.maximum(j - N_QKV_STEPS, 0)
    n_tiles = t // tm
    xtile = lambda i, j: jnp.minimum(i + jnp.where(j > 0, 1, 0), n_tiles - 1)
    qwstep = lambda j: jnp.where(j < N_QKV_STEPS, j, 0)
    return pl.pallas_call(
        functools.partial(_inproj_kernel, seq=seq),
        grid=(n_tiles, N_QKV_STEPS + N_CONV_STEPS),
        in_specs=[
            pl.BlockSpec((tm, d), lambda i, j: (xtile(i, j), 0)),
            pl.BlockSpec((HALO, d), lambda i, j: (jnp.maximum(xtile(i, j) * hb - 1, 0), 0)),
            pl.BlockSpec((HALO, d),
                         lambda i, j: (jnp.minimum((xtile(i, j) + 1) * hb, nhb - 1), 0)),
            pl.BlockSpec((1, d), lambda i, j: (0, 0)),
            pl.BlockSpec((d, tn), lambda i, j: (0, qwstep(j))),
            pl.BlockSpec((d, tc), lambda i, j: (0, col_b + cstep(j))),
            pl.BlockSpec((d, tc), lambda i, j: (0, col_c + cstep(j))),
            pl.BlockSpec((d, tc), lambda i, j: (0, col_u + cstep(j))),
            pl.BlockSpec((CONV_W, tc), lambda i, j: (0, cstep(j))),
            pl.BlockSpec((1, tc), lambda i, j: (0, cstep(j))),
            pl.BlockSpec((1, tc), lambda i, j: (0, cstep(j))),
        ],
        out_specs=[pl.BlockSpec((tm, tn), lambda i, j: (i, qstep(j))),
                   pl.BlockSpec((tm, tc), lambda i, j: (i, cstep(j)))],
        out_shape=[jax.ShapeDtypeStruct((t, 3 * D_ATTN), BF16),
                   jax.ShapeDtypeStruct((t, D_CONV), BF16)],
        scratch_shapes=[pltpu.VMEM((tm + 2 * HALO, d), BF16),
                        pltpu.VMEM((CONV_SUBTILES, tm // CONV_SUBTILES + 2 * HALO, tc), F32)],
        compiler_params=_params(("parallel", "arbitrary"), BIG_VMEM_LIMIT),
        name="inproj",
    )(x, x, x, g, w_in, w_in, w_in, w_in, conv_w, conv_b, gn)


N_REL = 2 * WIN_ROWS - 1
PAIR_FULL = 0
PAIR_LEFT_NEG = N_REL - 1
PAIR_RIGHT_NEG = 2 * N_REL - 1
PAIR_NEG = 3 * N_REL - 1
N_PAIR_BLOCKS = 3 * N_REL
N_ROW_PATTERNS = 3
N_PAIRS = K_ROWS // 2


def _pair_blocks(rpb):
    nh, nr, nc = rpb.shape
    w = GRID_W
    c = np.arange(w)
    col_start = np.clip(c - WIN_COLS // 2, 0, w - WIN_COLS)
    in_win = (c[None, :] >= col_start[:, None]) & (c[None, :] < col_start[:, None] + WIN_COLS)
    period = 2 * w
    left = w - WIN_COLS
    u = jnp.pad(rpb, ((0, 0), (0, 0), (left, period - nc - left)))
    toep = jnp.tile(u, (1, 1, w))[..., :w * (period - 1)].reshape(nh, nr, w, period - 1)[..., w - 1:]
    toep = jnp.where(in_win[None, None], toep * LOG2E, NEG)
    neg = jnp.full((nh, nr, w, w), NEG, F32)
    return jnp.concatenate([
        jnp.concatenate([toep[:, :-1], toep[:, 1:]], axis=-1),
        jnp.concatenate([neg, toep], axis=-1),
        jnp.concatenate([toep, neg], axis=-1),
        jnp.concatenate([neg[:, :1], neg[:, :1]], axis=-1)], axis=1)


def _pair_index_table(rows):
    tab = np.zeros((N_ROW_PATTERNS, Q_ROWS, N_PAIRS), np.int32)
    for pat, r0 in enumerate((0, Q_ROWS, rows - Q_ROWS)):
        k0 = min(max(r0 - WIN_ROWS // 2, 0), rows - K_ROWS)
        for i in range(Q_ROWS):
            r = r0 + i
            row_start = min(max(r - WIN_ROWS // 2, 0), rows - WIN_ROWS)
            for m in range(N_PAIRS):
                kr = k0 + 2 * m
                in0 = row_start <= kr < row_start + WIN_ROWS
                in1 = row_start <= kr + 1 < row_start + WIN_ROWS
                a = kr - r + WIN_ROWS - 1
                if in0 and in1:
                    tab[pat, i, m] = PAIR_FULL + a
                elif in1:
                    tab[pat, i, m] = PAIR_LEFT_NEG + a + 1
                elif in0:
                    tab[pat, i, m] = PAIR_RIGHT_NEG + a
                else:
                    tab[pat, i, m] = PAIR_NEG
    return tab


def _chunk_spans(tab):
    pairs_per_chunk = K_CHUNK_ROWS // 2
    spans = []
    for pat in range(tab.shape[0]):
        row = []
        for c in range(K_ROWS // K_CHUNK_ROWS):
            blk = tab[pat, :, c * pairs_per_chunk:(c + 1) * pairs_per_chunk]
            hit = np.nonzero((blk != PAIR_NEG).any(axis=1))[0]
            if hit.size == 0:
                row.append((0, 0))
            else:
                assert hit[-1] - hit[0] + 1 == hit.size
                row.append((int(hit[0]), int(hit[-1]) + 1))
        spans.append(row)
    return spans


def _attn_head(h, tab, spans, q_ref, ks, vs, pb_ref, gn_ref, o_ref):
    hs = slice(h * HEAD_DIM, (h + 1) * HEAD_DIM)
    pairs_per_chunk = K_CHUNK_ROWS // 2
    live = [c for c, (first, last) in enumerate(spans) if last > first]
    w = GRID_W
    scores = {}
    m = [None] * Q_ROWS
    for c in live:
        first, last = spans[c]
        s = lax.dot_general(q_ref[first * w:last * w, hs], ks[c][:, hs],
                            (((1,), (1,)), ((), ())), preferred_element_type=F32)
        for i in range(first, last):
            bias = jnp.concatenate([pb_ref[h, int(tab[i, c * pairs_per_chunk + k])]
                                    for k in range(pairs_per_chunk)], axis=1)
            si = s[(i - first) * w:(i - first + 1) * w] + bias
            scores[c, i] = si
            mi = jnp.max(si, axis=1, keepdims=True)
            m[i] = mi if m[i] is None else jnp.maximum(m[i], mi)
    l = [None] * Q_ROWS
    o = [None] * Q_ROWS
    for c in live:
        first, last = spans[c]
        ps = []
        for i in range(first, last):
            p = jnp.exp2(scores[c, i] - m[i])
            li = jnp.sum(p, axis=1, keepdims=True)
            l[i] = li if l[i] is None else l[i] + li
            ps.append(p.astype(BF16))
        oc = jnp.dot(jnp.concatenate(ps, axis=0), vs[c][:, hs], preferred_element_type=F32)
        for i in range(first, last):
            oi = oc[(i - first) * w:(i - first + 1) * w]
            o[i] = oi if o[i] is None else o[i] + oi
    gn = gn_ref[:, hs]
    for i in range(Q_ROWS):
        o_ref[i * w:(i + 1) * w, hs] = _rms(o[i] / l[i], gn).astype(BF16)


def _attn_kernel(q_ref, k0, k1, k2, k3, v0, v1, v2, v3, pb_ref, gn_ref, o_ref, *, tab, spans):
    rb = pl.program_id(1)
    pat = jnp.where(rb == 0, 0, jnp.where(rb == pl.num_programs(1) - 1, 2, 1))
    for pat_id in range(N_ROW_PATTERNS):
        @pl.when(pat == pat_id)
        def _(pat_id=pat_id):
            for h in range(N_HEADS):
                _attn_head(h, tab[pat_id], spans[pat_id], q_ref, (k0, k1, k2, k3),
                           (v0, v1, v2, v3), pb_ref, gn_ref, o_ref)


def _attn(qkv, pair_blocks, gn, batch, seq):
    rows = seq // GRID_W
    tq = Q_ROWS * GRID_W
    kc = K_CHUNK_ROWS * GRID_W
    n_rb = rows // Q_ROWS
    n_kc = K_ROWS // K_CHUNK_ROWS
    chunks_per_seq = seq // kc
    tab = _pair_index_table(rows)

    def q_map(b, rb):
        return (b * n_rb + rb, 0)

    def kv_map(col, c):
        def f(b, rb):
            first = jnp.clip(rb * (Q_ROWS // K_CHUNK_ROWS) - (WIN_ROWS // 2) // K_CHUNK_ROWS,
                             0, chunks_per_seq - n_kc)
            return (b * chunks_per_seq + first + c, col)
        return f

    k_specs = [pl.BlockSpec((kc, D_ATTN), kv_map(1, c)) for c in range(n_kc)]
    v_specs = [pl.BlockSpec((kc, D_ATTN), kv_map(2, c)) for c in range(n_kc)]
    return pl.pallas_call(
        functools.partial(_attn_kernel, tab=tab, spans=_chunk_spans(tab)),
        grid=(batch, n_rb),
        in_specs=[pl.BlockSpec((tq, D_ATTN), q_map)] + k_specs + v_specs + [
            pl.BlockSpec(pair_blocks.shape, lambda b, rb: (0, 0, 0, 0),
                         pipeline_mode=pl.Buffered(1)),
            pl.BlockSpec((1, D_ATTN), lambda b, rb: (0, 0)),
        ],
        out_specs=pl.BlockSpec((tq, D_ATTN), q_map),
        out_shape=jax.ShapeDtypeStruct((batch * seq, D_ATTN), BF16),
        compiler_params=_params(("parallel", "arbitrary")),
        name="attn",
    )(qkv, *([qkv] * (2 * n_kc)), pair_blocks, gn)


def _outproj_kernel(x_ref, a_ref, c_ref, wa_ref, wc_ref, o_ref):
    acc = jnp.dot(a_ref[...], wa_ref[...].astype(BF16), preferred_element_type=F32)
    acc = acc + jnp.dot(c_ref[...], wc_ref[...].astype(BF16), preferred_element_type=F32)
    o_ref[...] = x_ref[...] + acc


def _outproj(x, a, c, w_out):
    t, d = x.shape
    tm = OUT_TM
    return pl.pallas_call(
        _outproj_kernel,
        grid=(t // tm,),
        in_specs=[
            pl.BlockSpec((tm, d), lambda i: (i, 0)),
            pl.BlockSpec((tm, D_ATTN), lambda i: (i, 0)),
            pl.BlockSpec((tm, D_CONV), lambda i: (i, 0)),
            pl.BlockSpec((D_ATTN, d), lambda i: (0, 0), pipeline_mode=pl.Buffered(1)),
            pl.BlockSpec((D_CONV, d), lambda i: (1, 0), pipeline_mode=pl.Buffered(1)),
        ],
        out_specs=pl.BlockSpec((tm, d), lambda i: (i, 0)),
        out_shape=jax.ShapeDtypeStruct((t, d), F32),
        compiler_params=_params(("parallel",)),
        name="outproj",
    )(x, a, c, w_out, w_out)


def _ple_kernel(x_ref, p_ref, g_ref, wg_ref, wp_ref, gf_ref, o_ref):
    x = x_ref[...]
    xn = _rms(x, g_ref[...]).astype(BF16)
    gate = jax.nn.sigmoid(jnp.dot(xn, wg_ref[...].astype(BF16), preferred_element_type=F32))
    proj = jnp.dot(p_ref[...].astype(BF16), wp_ref[...].astype(BF16),
                   preferred_element_type=F32)
    o_ref[...] = _rms(x + gate * proj, gf_ref[...])


def _ple(x, p, g, w_gate, w_proj, gf):
    t, d = x.shape
    tm = PLE_TM
    return pl.pallas_call(
        _ple_kernel,
        grid=(t // tm,),
        in_specs=[
            pl.BlockSpec((tm, d), lambda i: (i, 0)),
            pl.BlockSpec((tm, PLE_DIM), lambda i: (i, 0)),
            pl.BlockSpec((1, d), lambda i: (0, 0)),
            pl.BlockSpec((d, d), lambda i: (0, 0), pipeline_mode=pl.Buffered(1)),
            pl.BlockSpec((PLE_DIM, d), lambda i: (0, 0), pipeline_mode=pl.Buffered(1)),
            pl.BlockSpec((1, d), lambda i: (0, 0)),
        ],
        out_specs=pl.BlockSpec((tm, d), lambda i: (i, 0)),
        out_shape=jax.ShapeDtypeStruct((t, d), F32),
        compiler_params=_params(("parallel",)),
        name="ple",
    )(x, p, g, w_gate, w_proj, gf)


def kernel(x, p, ffn1_norm, ffn1_wg, ffn1_wu, ffn1_wd, mix_norm, w_in, rpb, conv_w, conv_b,
           attn_out_norm, conv_out_norm, w_out, ffn2_norm, ffn2_wg, ffn2_wu, ffn2_wd,
           ple_norm, ple_w_gate, ple_w_proj, final_norm):
    b, s, d = x.shape
    assert ffn1_wg.shape[0] == 1
    assert d == D_MODEL and s % (Q_ROWS * GRID_W) == 0 and s % PROJ_TM == 0
    t = b * s
    h = x.reshape(t, d)
    row = lambda v: v.reshape(1, -1)
    for i in range(1):
        h = _ffn(h, row(ffn1_norm[i]), ffn1_wg[i], ffn1_wu[i], ffn1_wd[i])
        qkv, y_conv = _inproj(h, row(mix_norm[i]), w_in[i], conv_w[i], row(conv_b[i]),
                              row(conv_out_norm[i]), s)
        y_attn = _attn(qkv, _pair_blocks(rpb[i]), row(attn_out_norm[i]), b, s)
        h = _outproj(h, y_attn, y_conv, w_out[i])
        h = _ffn(h, row(ffn2_norm[i]), ffn2_wg[i], ffn2_wu[i], ffn2_wd[i])
        h = _ple(h, p[i].reshape(t, PLE_DIM), row(ple_norm[i]), ple_w_gate[i], ple_w_proj[i],
                 row(final_norm))
    return h.reshape(b, s, d)
```

```python
import functools

import numpy as np
import jax
import jax.numpy as jnp
from jax import lax
from jax.experimental import pallas as pl
from jax.experimental.pallas import tpu as pltpu

D_MODEL = 2048
GRID_W = 64
PLE_DIM = 256
D_ATTN = D_MODEL // 2
D_CONV = D_MODEL - D_ATTN
N_HEADS = 8
HEAD_DIM = D_ATTN // N_HEADS
GROUP_DIM = 128
CONV_W = 3
WIN_ROWS = 8
WIN_COLS = 16
RMS_EPS = 1e-6
NEG = -1e30
LOG2E = 1.4426950408889634
Q_SCALE = HEAD_DIM ** -0.5 * LOG2E

BF16 = jnp.bfloat16
F32 = jnp.float32

VMEM_LIMIT = 56 * 1024 * 1024
BIG_VMEM_LIMIT = 60 * 1024 * 1024
FFN_TM = 1024
FFN_TF = 512
X_PIECES = 4
PROJ_TM = 1024
PROJ_TN = 1024
CONV_TC = 256
CONV_SUBTILES = 2
HALO = 16
Q_ROWS = 8
K_ROWS = 16
K_CHUNK_ROWS = 4
OUT_TM = 512
PLE_TM = 512


def _rms(x, g):
    ms = jnp.mean(x * x, axis=-1, keepdims=True)
    return x * lax.rsqrt(ms + RMS_EPS) * g


def _params(sem, vmem_limit=VMEM_LIMIT):
    return pltpu.CompilerParams(dimension_semantics=sem, vmem_limit_bytes=vmem_limit)


def _ffn_kernel(x_hbm, g_ref, wg_ref, wu_ref, wd_ref, o_ref, xbuf, xn_ref, sem):
    i = pl.program_id(0)
    j = pl.program_id(1)
    tm = xbuf.shape[0]

    tp = tm // X_PIECES

    def x_copy(tile, piece):
        rows = pl.ds(pl.multiple_of(tile * tm + piece * tp, tp), tp)
        return pltpu.make_async_copy(x_hbm.at[rows, :], xbuf.at[piece * tp:(piece + 1) * tp, :],
                                     sem)

    @pl.when(j == 0)
    def _():
        @pl.when(i == 0)
        def _():
            for piece in range(X_PIECES):
                x_copy(0, piece).start()

        for piece in range(X_PIECES):
            x_copy(i, piece).wait()
        x = xbuf[...]
        xn_ref[...] = _rms(x, g_ref[...]).astype(BF16)
        o_ref[...] = x

    for piece in range(X_PIECES):
        @pl.when((j == 1 + piece) & (i + 1 < pl.num_programs(0)))
        def _(piece=piece):
            x_copy(i + 1, piece).start()

    xn = xn_ref[...]
    gate = jnp.dot(xn, wg_ref[...].astype(BF16), preferred_element_type=F32)
    up = jnp.dot(xn, wu_ref[...].astype(BF16), preferred_element_type=F32)
    act = (jax.nn.silu(gate) * up * 0.5).astype(BF16)
    o_ref[...] += jnp.dot(act, wd_ref[...].astype(BF16), preferred_element_type=F32)


def _ffn(x, g, wg, wu, wd):
    t, d = x.shape
    f = wg.shape[1]
    return pl.pallas_call(
        _ffn_kernel,
        grid=(t // FFN_TM, f // FFN_TF),
        in_specs=[
            pl.BlockSpec(memory_space=pl.ANY),
            pl.BlockSpec((1, d), lambda i, j: (0, 0)),
            pl.BlockSpec((d, FFN_TF), lambda i, j: (0, j)),
            pl.BlockSpec((d, FFN_TF), lambda i, j: (0, j)),
            pl.BlockSpec((FFN_TF, d), lambda i, j: (j, 0)),
        ],
        out_specs=pl.BlockSpec((FFN_TM, d), lambda i, j: (i, 0)),
        out_shape=jax.ShapeDtypeStruct((t, d), F32),
        scratch_shapes=[pltpu.VMEM((FFN_TM, d), F32),
                        pltpu.VMEM((FFN_TM, d), BF16),
                        pltpu.SemaphoreType.DMA(())],
        compiler_params=_params(("arbitrary", "arbitrary"), BIG_VMEM_LIMIT),
        name="ffn",
    )(x, g, wg, wu, wd)


N_QKV_STEPS = 3 * D_ATTN // PROJ_TN
N_CONV_STEPS = D_CONV // CONV_TC


def _inproj_kernel(x_ref, xp_ref, xnx_ref, g_ref, wq_ref, wb_ref, wc_ref, wu_ref, cw_ref,
                   cb_ref, gn_ref, qkv_ref, conv_ref, xn_ref, cu_ref, *, seq):
    tm = x_ref.shape[0]
    j = pl.program_id(1)

    @pl.when(j == 0)
    def _():
        g = g_ref[...]
        xn_ref[0:HALO, :] = _rms(xp_ref[...], g).astype(BF16)
        xn_ref[HALO:HALO + tm, :] = _rms(x_ref[...], g).astype(BF16)
        xn_ref[HALO + tm:, :] = _rms(xnx_ref[...], g).astype(BF16)

    @pl.when(j < N_QKV_STEPS)
    def _():
        acc = jnp.dot(xn_ref[HALO:HALO + tm, :], wq_ref[...].astype(BF16),
                      preferred_element_type=F32)
        col_scale = jnp.where(j < D_ATTN // PROJ_TN, Q_SCALE, 1.0)
        qkv_ref[...] = (acc * col_scale).astype(BF16)

    @pl.when(j >= N_QKV_STEPS)
    def _():
        wb = wb_ref[...].astype(BF16)
        wc = wc_ref[...].astype(BF16)
        wu = wu_ref[...].astype(BF16)
        cw = cw_ref[...]
        cb = cb_ref[...]
        gn = gn_ref[...]
        ts = tm // CONV_SUBTILES
        for sub in range(CONV_SUBTILES):
            r0 = sub * ts
            xs = xn_ref[r0:r0 + ts + 2 * HALO, :]
            gate_b = jnp.dot(xn_ref[HALO + r0:HALO + r0 + ts, :], wb, preferred_element_type=F32)
            gate_c = jnp.dot(xs, wc, preferred_element_type=F32)
            u = jnp.dot(xs, wu, preferred_element_type=F32)
            cu_ref[sub] = gate_c * u

            pos = (lax.broadcasted_iota(jnp.int32, (ts, 1), 0) + pl.program_id(0) * tm + r0) % seq
            prev = jnp.where(pos == 0, 0.0, cu_ref[sub, HALO - 1:HALO - 1 + ts, :])
            cur = cu_ref[sub, HALO:HALO + ts, :]
            nxt = jnp.where(pos == seq - 1, 0.0, cu_ref[sub, HALO + 1:HALO + 1 + ts, :])
            y = prev * cw[0:1, :]
            y = y + cur * cw[1:2, :]
            y = y + nxt * cw[2:3, :]
            y = gate_b * (y + cb)
            for c in range(0, y.shape[1], GROUP_DIM):
                sl = slice(c, c + GROUP_DIM)
                conv_ref[r0:r0 + ts, sl] = _rms(y[:, sl], gn[:, sl]).astype(BF16)


def _inproj(x, g, w_in, conv_w, conv_b, gn, seq):
    t, d = x.shape
    tm, tn, tc = PROJ_TM, PROJ_TN, CONV_TC
    hb = tm // HALO
    nhb = t // HALO
    col_b = 3 * D_ATTN // tc
    col_c = col_b + D_CONV // tc
    col_u = col_c + D_CONV // tc
    qstep = lambda j: jnp.minimum(j, N_QKV_STEPS - 1)
    cstep = lambda j: jnp.maximum(j - N_QKV_STEPS, 0)
    n_tiles = t // tm
    xtile = lambda i, j: jnp.minimum(i + jnp.where(j > 0, 1, 0), n_tiles - 1)
    qwstep = lambda j: jnp.where(j < N_QKV_STEPS, j, 0)
    return pl.pallas_call(
        functools.partial(_inproj_kernel, seq=seq),
        grid=(n_tiles, N_QKV_STEPS + N_CONV_STEPS),
        in_specs=[
            pl.BlockSpec((tm, d), lambda i, j: (xtile(i, j), 0)),
            pl.BlockSpec((HALO, d), lambda i, j: (jnp.maximum(xtile(i, j) * hb - 1, 0), 0)),
            pl.BlockSpec((HALO, d),
                         lambda i, j: (jnp.minimum((xtile(i, j) + 1) * hb, nhb - 1), 0)),
            pl.BlockSpec((1, d), lambda i, j: (0, 0)),
            pl.BlockSpec((d, tn), lambda i, j: (0, qwstep(j))),
            pl.BlockSpec((d, tc), lambda i, j: (0, col_b + cstep(j))),
            pl.BlockSpec((d, tc), lambda i, j: (0, col_c + cstep(j))),
            pl.BlockSpec((d, tc), lambda i, j: (0, col_u + cstep(j))),
            pl.BlockSpec((CONV_W, tc), lambda i, j: (0, cstep(j))),
            pl.BlockSpec((1, tc), lambda i, j: (0, cstep(j))),
            pl.BlockSpec((1, tc), lambda i, j: (0, cstep(j))),
        ],
        out_specs=[pl.BlockSpec((tm, tn), lambda i, j: (i, qstep(j))),
                   pl.BlockSpec((tm, tc), lambda i, j: (i, cstep(j)))],
        out_shape=[jax.ShapeDtypeStruct((t, 3 * D_ATTN), BF16),
                   jax.ShapeDtypeStruct((t, D_CONV), BF16)],
        scratch_shapes=[pltpu.VMEM((tm + 2 * HALO, d), BF16),
                        pltpu.VMEM((CONV_SUBTILES, tm // CONV_SUBTILES + 2 * HALO, tc), F32)],
        compiler_params=_params(("parallel", "arbitrary"), BIG_VMEM_LIMIT),
        name="inproj",
    )(x, x, x, g, w_in, w_in, w_in, w_in, conv_w, conv_b, gn)


N_REL = 2 * WIN_ROWS - 1
PAIR_FULL = 0
PAIR_LEFT_NEG = N_REL - 1
PAIR_RIGHT_NEG = 2 * N_REL - 1
PAIR_NEG = 3 * N_REL - 1
N_PAIR_BLOCKS = 3 * N_REL
N_ROW_PATTERNS = 3
N_PAIRS = K_ROWS // 2


def _pair_blocks(rpb):
    nh, nr, nc = rpb.shape
    w = GRID_W
    c = np.arange(w)
    col_start = np.clip(c - WIN_COLS // 2, 0, w - WIN_COLS)
    in_win = (c[None, :] >= col_start[:, None]) & (c[None, :] < col_start[:, None] + WIN_COLS)
    period = 2 * w
    left = w - WIN_COLS
    u = jnp.pad(rpb, ((0, 0), (0, 0), (left, period - nc - left)))
    toep = jnp.tile(u, (1, 1, w))[..., :w * (period - 1)].reshape(nh, nr, w, period - 1)[..., w - 1:]
    toep = jnp.where(in_win[None, None], toep * LOG2E, NEG)
    neg = jnp.full((nh, nr, w, w), NEG, F32)
    return jnp.concatenate([
        jnp.concatenate([toep[:, :-1], toep[:, 1:]], axis=-1),
        jnp.concatenate([neg, toep], axis=-1),
        jnp.concatenate([toep, neg], axis=-1),
        jnp.concatenate([neg[:, :1], neg[:, :1]], axis=-1)], axis=1)


def _pair_index_table(rows):
    tab = np.zeros((N_ROW_PATTERNS, Q_ROWS, N_PAIRS), np.int32)
    for pat, r0 in enumerate((0, Q_ROWS, rows - Q_ROWS)):
        k0 = min(max(r0 - WIN_ROWS // 2, 0), rows - K_ROWS)
        for i in range(Q_ROWS):
            r = r0 + i
            row_start = min(max(r - WIN_ROWS // 2, 0), rows - WIN_ROWS)
            for m in range(N_PAIRS):
                kr = k0 + 2 * m
                in0 = row_start <= kr < row_start + WIN_ROWS
                in1 = row_start <= kr + 1 < row_start + WIN_ROWS
                a = kr - r + WIN_ROWS - 1
                if in0 and in1:
                    tab[pat, i, m] = PAIR_FULL + a
                elif in1:
                    tab[pat, i, m] = PAIR_LEFT_NEG + a + 1
                elif in0:
                    tab[pat, i, m] = PAIR_RIGHT_NEG + a
                else:
                    tab[pat, i, m] = PAIR_NEG
    return tab


def _chunk_spans(tab):
    pairs_per_chunk = K_CHUNK_ROWS // 2
    spans = []
    for pat in range(tab.shape[0]):
        row = []
        for c in range(K_ROWS // K_CHUNK_ROWS):
            blk = tab[pat, :, c * pairs_per_chunk:(c + 1) * pairs_per_chunk]
            hit = np.nonzero((blk != PAIR_NEG).any(axis=1))[0]
            if hit.size == 0:
                row.append((0, 0))
            else:
                assert hit[-1] - hit[0] + 1 == hit.size
                row.append((int(hit[0]), int(hit[-1]) + 1))
        spans.append(row)
    return spans


def _attn_head(h, tab, spans, q_ref, ks, vs, pb_ref, gn_ref, o_ref):
    hs = slice(h * HEAD_DIM, (h + 1) * HEAD_DIM)
    pairs_per_chunk = K_CHUNK_ROWS // 2
    live = [c for c, (first, last) in enumerate(spans) if last > first]
    w = GRID_W
    scores = {}
    m = [None] * Q_ROWS
    for c in live:
        first, last = spans[c]
        s = lax.dot_general(q_ref[first * w:last * w, hs], ks[c][:, hs],
                            (((1,), (1,)), ((), ())), preferred_element_type=F32)
        for i in range(first, last):
            bias = jnp.concatenate([pb_ref[h, int(tab[i, c * pairs_per_chunk + k])]
                                    for k in range(pairs_per_chunk)], axis=1)
            si = s[(i - first) * w:(i - first + 1) * w] + bias
            scores[c, i] = si
            mi = jnp.max(si, axis=1, keepdims=True)
            m[i] = mi if m[i] is None else jnp.maximum(m[i], mi)
    l = [None] * Q_ROWS
    o = [None] * Q_ROWS
    for c in live:
        first, last = spans[c]
        ps = []
        for i in range(first, last):
            p = jnp.exp2(scores[c, i] - m[i])
            li = jnp.sum(p, axis=1, keepdims=True)
            l[i] = li if l[i] is None else l[i] + li
            ps.append(p.astype(BF16))
        oc = jnp.dot(jnp.concatenate(ps, axis=0), vs[c][:, hs], preferred_element_type=F32)
        for i in range(first, last):
            oi = oc[(i - first) * w:(i - first + 1) * w]
            o[i] = oi if o[i] is None else o[i] + oi
    gn = gn_ref[:, hs]
    for i in range(Q_ROWS):
        o_ref[i * w:(i + 1) * w, hs] = _rms(o[i] / l[i], gn).astype(BF16)


def _attn_kernel(q_ref, k0, k1, k2, k3, v0, v1, v2, v3, pb_ref, gn_ref, o_ref, *, tab, spans):
    rb = pl.program_id(1)
    pat = jnp.where(rb == 0, 0, jnp.where(rb == pl.num_programs(1) - 1, 2, 1))
    for pat_id in range(N_ROW_PATTERNS):
        @pl.when(pat == pat_id)
        def _(pat_id=pat_id):
            for h in range(N_HEADS):
                _attn_head(h, tab[pat_id], spans[pat_id], q_ref, (k0, k1, k2, k3),
                           (v0, v1, v2, v3), pb_ref, gn_ref, o_ref)


def _attn(qkv, pair_blocks, gn, batch, seq):
    rows = seq // GRID_W
    tq = Q_ROWS * GRID_W
    kc = K_CHUNK_ROWS * GRID_W
    n_rb = rows // Q_ROWS
    n_kc = K_ROWS // K_CHUNK_ROWS
    chunks_per_seq = seq // kc
    tab = _pair_index_table(rows)

    def q_map(b, rb):
        return (b * n_rb + rb, 0)

    def kv_map(col, c):
        def f(b, rb):
            first = jnp.clip(rb * (Q_ROWS // K_CHUNK_ROWS) - (WIN_ROWS // 2) // K_CHUNK_ROWS,
                             0, chunks_per_seq - n_kc)
            return (b * chunks_per_seq + first + c, col)
        return f

    k_specs = [pl.BlockSpec((kc, D_ATTN), kv_map(1, c)) for c in range(n_kc)]
    v_specs = [pl.BlockSpec((kc, D_ATTN), kv_map(2, c)) for c in range(n_kc)]
    return pl.pallas_call(
        functools.partial(_attn_kernel, tab=tab, spans=_chunk_spans(tab)),
        grid=(batch, n_rb),
        in_specs=[pl.BlockSpec((tq, D_ATTN), q_map)] + k_specs + v_specs + [
            pl.BlockSpec(pair_blocks.shape, lambda b, rb: (0, 0, 0, 0),
                         pipeline_mode=pl.Buffered(1)),
            pl.BlockSpec((1, D_ATTN), lambda b, rb: (0, 0)),
        ],
        out_specs=pl.BlockSpec((tq, D_ATTN), q_map),
        out_shape=jax.ShapeDtypeStruct((batch * seq, D_ATTN), BF16),
        compiler_params=_params(("parallel", "arbitrary")),
        name="attn",
    )(qkv, *([qkv] * (2 * n_kc)), pair_blocks, gn)


def _outproj_kernel(x_ref, a_ref, c_ref, wa_ref, wc_ref, o_ref):
    acc = jnp.dot(a_ref[...], wa_ref[...].astype(BF16), preferred_element_type=F32)
    acc = acc + jnp.dot(c_ref[...], wc_ref[...].astype(BF16), preferred_element_type=F32)
    o_ref[...] = x_ref[...] + acc


def _outproj(x, a, c, w_out):
    t, d = x.shape
    tm = OUT_TM
    return pl.pallas_call(
        _outproj_kernel,
        grid=(t // tm,),
        in_specs=[
            pl.BlockSpec((tm, d), lambda i: (i, 0)),
            pl.BlockSpec((tm, D_ATTN), lambda i: (i, 0)),
            pl.BlockSpec((tm, D_CONV), lambda i: (i, 0)),
            pl.BlockSpec((D_ATTN, d), lambda i: (0, 0), pipeline_mode=pl.Buffered(1)),
            pl.BlockSpec((D_CONV, d), lambda i: (1, 0), pipeline_mode=pl.Buffered(1)),
        ],
        out_specs=pl.BlockSpec((tm, d), lambda i: (i, 0)),
        out_shape=jax.ShapeDtypeStruct((t, d), F32),
        compiler_params=_params(("parallel",)),
        name="outproj",
    )(x, a, c, w_out, w_out)


def _ple_kernel(x_ref, p_ref, g_ref, wg_ref, wp_ref, gf_ref, o_ref):
    x = x_ref[...]
    xn = _rms(x, g_ref[...]).astype(BF16)
    gate = jax.nn.sigmoid(jnp.dot(xn, wg_ref[...].astype(BF16), preferred_element_type=F32))
    proj = jnp.dot(p_ref[...].astype(BF16), wp_ref[...].astype(BF16),
                   preferred_element_type=F32)
    o_ref[...] = _rms(x + gate * proj, gf_ref[...])


def _ple(x, p, g, w_gate, w_proj, gf):
    t, d = x.shape
    tm = PLE_TM
    return pl.pallas_call(
        _ple_kernel,
        grid=(t // tm,),
        in_specs=[
            pl.BlockSpec((tm, d), lambda i: (i, 0)),
            pl.BlockSpec((tm, PLE_DIM), lambda i: (i, 0)),
            pl.BlockSpec((1, d), lambda i: (0, 0)),
            pl.BlockSpec((d, d), lambda i: (0, 0), pipeline_mode=pl.Buffered(1)),
            pl.BlockSpec((PLE_DIM, d), lambda i: (0, 0), pipeline_mode=pl.Buffered(1)),
            pl.BlockSpec((1, d), lambda i: (0, 0)),
        ],
        out_specs=pl.BlockSpec((tm, d), lambda i: (i, 0)),
        out_shape=jax.ShapeDtypeStruct((t, d), F32),
        compiler_params=_params(("parallel",)),
        name="ple",
    )(x, p, g, w_gate, w_proj, gf)


def kernel(x, p, ffn1_norm, ffn1_wg, ffn1_wu, ffn1_wd, mix_norm, w_in, rpb, conv_w, conv_b,
           attn_out_norm, conv_out_norm, w_out, ffn2_norm, ffn2_wg, ffn2_wu, ffn2_wd,
           ple_norm, ple_w_gate, ple_w_proj, final_norm):
    b, s, d = x.shape
    assert ffn1_wg.shape[0] == 1
    assert d == D_MODEL and s % (Q_ROWS * GRID_W) == 0 and s % PROJ_TM == 0
    t = b * s
    h = x.reshape(t, d)
    row = lambda v: v.reshape(1, -1)
    for i in range(1):
        h = _ffn(h, row(ffn1_norm[i]), ffn1_wg[i], ffn1_wu[i], ffn1_wd[i])
        qkv, y_conv = _inproj(h, row(mix_norm[i]), w_in[i], conv_w[i], row(conv_b[i]),
                              row(conv_out_norm[i]), s)
        y_attn = _attn(qkv, _pair_blocks(rpb[i]), row(attn_out_norm[i]), b, s)
        h = _outproj(h, y_attn, y_conv, w_out[i])
        h = _ffn(h, row(ffn2_norm[i]), ffn2_wg[i], ffn2_wu[i], ffn2_wd[i])
        h = _ple(h, p[i].reshape(t, PLE_DIM), row(ple_norm[i]), ple_w_gate[i], ple_w_proj[i],
                 row(final_norm))
    return h.reshape(b, s, d)
```

```python
import functools

import numpy as np
import jax
import jax.numpy as jnp
from jax import lax
from jax.experimental import pallas as pl
from jax.experimental.pallas import tpu as pltpu

D_MODEL = 2048
GRID_W = 64
PLE_DIM = 256
D_ATTN = D_MODEL // 2
D_CONV = D_MODEL - D_ATTN
N_HEADS = 8
HEAD_DIM = D_ATTN // N_HEADS
GROUP_DIM = 128
CONV_W = 3
WIN_ROWS = 8
WIN_COLS = 16
RMS_EPS = 1e-6
NEG = -1e30
LOG2E = 1.4426950408889634
Q_SCALE = HEAD_DIM ** -0.5 * LOG2E

BF16 = jnp.bfloat16
F32 = jnp.float32

VMEM_LIMIT = 56 * 1024 * 1024
BIG_VMEM_LIMIT = 60 * 1024 * 1024
FFN_TM = 1024
FFN_TF = 512
PROJ_TM = 1024
PROJ_TN = 1024
CONV_TC = 256
CONV_SUBTILES = 2
HALO = 16
Q_ROWS = 8
K_ROWS = 16
K_CHUNK_ROWS = 4
OUT_TM = 512
PLE_TM = 512


def _rms(x, g):
    ms = jnp.mean(x * x, axis=-1, keepdims=True)
    return x * lax.rsqrt(ms + RMS_EPS) * g


def _params(sem, vmem_limit=VMEM_LIMIT):
    return pltpu.CompilerParams(dimension_semantics=sem, vmem_limit_bytes=vmem_limit)


def _ffn_kernel(x_hbm, g_ref, wg_ref, wu_ref, wd_ref, o_ref, xbuf, xn_ref, sem):
    i = pl.program_id(0)
    j = pl.program_id(1)
    tm = xbuf.shape[0]

    def x_copy(tile):
        rows = pl.ds(pl.multiple_of(tile * tm, tm), tm)
        return pltpu.make_async_copy(x_hbm.at[rows, :], xbuf, sem)

    def half_swiglu(xn):
        gate = jnp.dot(xn, wg_ref[...].astype(BF16), preferred_element_type=F32)
        up = jnp.dot(xn, wu_ref[...].astype(BF16), preferred_element_type=F32)
        act = (jax.nn.silu(gate) * up * 0.5).astype(BF16)
        return jnp.dot(act, wd_ref[...].astype(BF16), preferred_element_type=F32)

    @pl.when(j == 0)
    def _():
        @pl.when(i == 0)
        def _():
            x_copy(0).start()

        x_copy(i).wait()
        x = xbuf[...]
        xn = _rms(x, g_ref[...]).astype(BF16)
        xn_ref[...] = xn
        o_ref[...] = x + half_swiglu(xn)

    @pl.when((j == 1) & (i + 1 < pl.num_programs(0)))
    def _():
        x_copy(i + 1).start()

    @pl.when(j > 0)
    def _():
        o_ref[...] += half_swiglu(xn_ref[...])


def _ffn(x, g, wg, wu, wd):
    t, d = x.shape
    f = wg.shape[1]
    return pl.pallas_call(
        _ffn_kernel,
        grid=(t // FFN_TM, f // FFN_TF),
        in_specs=[
            pl.BlockSpec(memory_space=pl.ANY),
            pl.BlockSpec((1, d), lambda i, j: (0, 0)),
            pl.BlockSpec((d, FFN_TF), lambda i, j: (0, j)),
            pl.BlockSpec((d, FFN_TF), lambda i, j: (0, j)),
            pl.BlockSpec((FFN_TF, d), lambda i, j: (j, 0)),
        ],
        out_specs=pl.BlockSpec((FFN_TM, d), lambda i, j: (i, 0)),
        out_shape=jax.ShapeDtypeStruct((t, d), F32),
        scratch_shapes=[pltpu.VMEM((FFN_TM, d), F32),
                        pltpu.VMEM((FFN_TM, d), BF16),
                        pltpu.SemaphoreType.DMA(())],
        compiler_params=_params(("arbitrary", "arbitrary"), BIG_VMEM_LIMIT),
        name="ffn",
    )(x, g, wg, wu, wd)


N_QKV_STEPS = 3 * D_ATTN // PROJ_TN
N_CONV_STEPS = D_CONV // CONV_TC


def _inproj_kernel(x_ref, xp_ref, xnx_ref, g_ref, wq_ref, wb_ref, wc_ref, wu_ref, cw_ref,
                   cb_ref, gn_ref, qkv_ref, conv_ref, xn_ref, cu_ref, *, seq):
    tm = x_ref.shape[0]
    j = pl.program_id(1)

    def qkv_tile(xn):
        acc = jnp.dot(xn, wq_ref[...].astype(BF16), preferred_element_type=F32)
        col_scale = jnp.where(j < D_ATTN // PROJ_TN, Q_SCALE, 1.0)
        qkv_ref[...] = (acc * col_scale).astype(BF16)

    @pl.when(j == 0)
    def _():
        g = g_ref[...]
        xn_ref[0:HALO, :] = _rms(xp_ref[...], g).astype(BF16)
        xn_ref[HALO + tm:, :] = _rms(xnx_ref[...], g).astype(BF16)
        xn = _rms(x_ref[...], g).astype(BF16)
        xn_ref[HALO:HALO + tm, :] = xn
        qkv_tile(xn)

    @pl.when((j > 0) & (j < N_QKV_STEPS))
    def _():
        qkv_tile(xn_ref[HALO:HALO + tm, :])

    @pl.when(j >= N_QKV_STEPS)
    def _():
        wb = wb_ref[...].astype(BF16)
        wc = wc_ref[...].astype(BF16)
        wu = wu_ref[...].astype(BF16)
        cw = cw_ref[...]
        cb = cb_ref[...]
        gn = gn_ref[...]
        ts = tm // CONV_SUBTILES
        for sub in range(CONV_SUBTILES):
            r0 = sub * ts
            xs = xn_ref[r0:r0 + ts + 2 * HALO, :]
            gate_b = jnp.dot(xn_ref[HALO + r0:HALO + r0 + ts, :], wb, preferred_element_type=F32)
            gate_c = jnp.dot(xs, wc, preferred_element_type=F32)
            u = jnp.dot(xs, wu, preferred_element_type=F32)
            cu_ref[sub] = gate_c * u

            pos = (lax.broadcasted_iota(jnp.int32, (ts, 1), 0) + pl.program_id(0) * tm + r0) % seq
            prev = jnp.where(pos == 0, 0.0, cu_ref[sub, HALO - 1:HALO - 1 + ts, :])
            cur = cu_ref[sub, HALO:HALO + ts, :]
            nxt = jnp.where(pos == seq - 1, 0.0, cu_ref[sub, HALO + 1:HALO + 1 + ts, :])
            y = prev * cw[0:1, :]
            y = y + cur * cw[1:2, :]
            y = y + nxt * cw[2:3, :]
            y = gate_b * (y + cb)
            for c in range(0, y.shape[1], GROUP_DIM):
                sl = slice(c, c + GROUP_DIM)
                conv_ref[r0:r0 + ts, sl] = _rms(y[:, sl], gn[:, sl]).astype(BF16)


def _inproj(x, g, w_in, conv_w, conv_b, gn, seq):
    t, d = x.shape
    tm, tn, tc = PROJ_TM, PROJ_TN, CONV_TC
    hb = tm // HALO
    nhb = t // HALO
    col_b = 3 * D_ATTN // tc
    col_c = col_b + D_CONV // tc
    col_u = col_c + D_CONV // tc
    qstep = lambda j: jnp.minimum(j, N_QKV_STEPS - 1)
    cstep = lambda j: jnp.maximum(j - N_QKV_STEPS, 0)
    n_tiles = t // tm
    xtile = lambda i, j: jnp.minimum(i + jnp.where(j > 0, 1, 0), n_tiles - 1)
    qwstep = lambda j: jnp.where(j < N_QKV_STEPS, j, 0)
    return pl.pallas_call(
        functools.partial(_inproj_kernel, seq=seq),
        grid=(n_tiles, N_QKV_STEPS + N_CONV_STEPS),
        in_specs=[
            pl.BlockSpec((tm, d), lambda i, j: (xtile(i, j), 0)),
            pl.BlockSpec((HALO, d), lambda i, j: (jnp.maximum(xtile(i, j) * hb - 1, 0), 0)),
            pl.BlockSpec((HALO, d),
                         lambda i, j: (jnp.minimum((xtile(i, j) + 1) * hb, nhb - 1), 0)),
            pl.BlockSpec((1, d), lambda i, j: (0, 0)),
            pl.BlockSpec((d, tn), lambda i, j: (0, qwstep(j))),
            pl.BlockSpec((d, tc), lambda i, j: (0, col_b + cstep(j))),
            pl.BlockSpec((d, tc), lambda i, j: (0, col_c + cstep(j))),
            pl.BlockSpec((d, tc), lambda i, j: (0, col_u + cstep(j))),
            pl.BlockSpec((CONV_W, tc), lambda i, j: (0, cstep(j))),
            pl.BlockSpec((1, tc), lambda i, j: (0, cstep(j))),
            pl.BlockSpec((1, tc), lambda i, j: (0, cstep(j))),
        ],
        out_specs=[pl.BlockSpec((tm, tn), lambda i, j: (i, qstep(j))),
                   pl.BlockSpec((tm, tc), lambda i, j: (i, cstep(j)))],
        out_shape=[jax.ShapeDtypeStruct((t, 3 * D_ATTN), BF16),
                   jax.ShapeDtypeStruct((t, D_CONV), BF16)],
        scratch_shapes=[pltpu.VMEM((tm + 2 * HALO, d), BF16),
                        pltpu.VMEM((CONV_SUBTILES, tm // CONV_SUBTILES + 2 * HALO, tc), F32)],
        compiler_params=_params(("parallel", "arbitrary"), BIG_VMEM_LIMIT),
        name="inproj",
    )(x, x, x, g, w_in, w_in, w_in, w_in, conv_w, conv_b, gn)


N_REL = 2 * WIN_ROWS - 1
PAIR_FULL = 0
PAIR_LEFT_NEG = N_REL - 1
PAIR_RIGHT_NEG = 2 * N_REL - 1
PAIR_NEG = 3 * N_REL - 1
N_PAIR_BLOCKS = 3 * N_REL
N_ROW_PATTERNS = 3
N_PAIRS = K_ROWS // 2


def _pair_blocks(rpb):
    nh, nr, nc = rpb.shape
    w = GRID_W
    c = np.arange(w)
    col_start = np.clip(c - WIN_COLS // 2, 0, w - WIN_COLS)
    in_win = (c[None, :] >= col_start[:, None]) & (c[None, :] < col_start[:, None] + WIN_COLS)
    period = 2 * w
    left = w - WIN_COLS
    u = jnp.pad(rpb, ((0, 0), (0, 0), (left, period - nc - left)))
    toep = jnp.tile(u, (1, 1, w))[..., :w * (period - 1)].reshape(nh, nr, w, period - 1)[..., w - 1:]
    toep = jnp.where(in_win[None, None], toep * LOG2E, NEG)
    neg = jnp.full((nh, nr, w, w), NEG, F32)
    return jnp.concatenate([
        jnp.concatenate([toep[:, :-1], toep[:, 1:]], axis=-1),
        jnp.concatenate([neg, toep], axis=-1),
        jnp.concatenate([toep, neg], axis=-1),
        jnp.concatenate([neg[:, :1], neg[:, :1]], axis=-1)], axis=1)


def _pair_index_table(rows):
    tab = np.zeros((N_ROW_PATTERNS, Q_ROWS, N_PAIRS), np.int32)
    for pat, r0 in enumerate((0, Q_ROWS, rows - Q_ROWS)):
        k0 = min(max(r0 - WIN_ROWS // 2, 0), rows - K_ROWS)
        for i in range(Q_ROWS):
            r = r0 + i
            row_start = min(max(r - WIN_ROWS // 2, 0), rows - WIN_ROWS)
            for m in range(N_PAIRS):
                kr = k0 + 2 * m
                in0 = row_start <= kr < row_start + WIN_ROWS
                in1 = row_start <= kr + 1 < row_start + WIN_ROWS
                a = kr - r + WIN_ROWS - 1
                if in0 and in1:
                    tab[pat, i, m] = PAIR_FULL + a
                elif in1:
                    tab[pat, i, m] = PAIR_LEFT_NEG + a + 1
                elif in0:
                    tab[pat, i, m] = PAIR_RIGHT_NEG + a
                else:
                    tab[pat, i, m] = PAIR_NEG
    return tab


def _chunk_spans(tab):
    pairs_per_chunk = K_CHUNK_ROWS // 2
    spans = []
    for pat in range(tab.shape[0]):
        row = []
        for c in range(K_ROWS // K_CHUNK_ROWS):
            blk = tab[pat, :, c * pairs_per_chunk:(c + 1) * pairs_per_chunk]
            hit = np.nonzero((blk != PAIR_NEG).any(axis=1))[0]
            if hit.size == 0:
                row.append((0, 0))
            else:
                assert hit[-1] - hit[0] + 1 == hit.size
                row.append((int(hit[0]), int(hit[-1]) + 1))
        spans.append(row)
    return spans


def _attn_head(h, tab, spans, q_ref, ks, vs, pb_ref, gn_ref, o_ref):
    hs = slice(h * HEAD_DIM, (h + 1) * HEAD_DIM)
    pairs_per_chunk = K_CHUNK_ROWS // 2
    live = [c for c, (first, last) in enumerate(spans) if last > first]
    w = GRID_W
    scores = {}
    m = [None] * Q_ROWS
    for c in live:
        first, last = spans[c]
        s = lax.dot_general(q_ref[first * w:last * w, hs], ks[c][:, hs],
                            (((1,), (1,)), ((), ())), preferred_element_type=F32)
        for i in range(first, last):
            bias = jnp.concatenate([pb_ref[h, int(tab[i, c * pairs_per_chunk + k])]
                                    for k in range(pairs_per_chunk)], axis=1)
            si = s[(i - first) * w:(i - first + 1) * w] + bias
            scores[c, i] = si
            mi = jnp.max(si, axis=1, keepdims=True)
            m[i] = mi if m[i] is None else jnp.maximum(m[i], mi)
    l = [None] * Q_ROWS
    o = [None] * Q_ROWS
    for c in live:
        first, last = spans[c]
        ps = []
        for i in range(first, last):
            p = jnp.exp2(scores[c, i] - m[i])
            li = jnp.sum(p, axis=1, keepdims=True)
            l[i] = li if l[i] is None else l[i] + li
            ps.append(p.astype(BF16))
        oc = jnp.dot(jnp.concatenate(ps, axis=0), vs[c][:, hs], preferred_element_type=F32)
        for i in range(first, last):
            oi = oc[(i - first) * w:(i - first + 1) * w]
            o[i] = oi if o[i] is None else o[i] + oi
    gn = gn_ref[:, hs]
    for i in range(Q_ROWS):
        o_ref[i * w:(i + 1) * w, hs] = _rms(o[i] / l[i], gn).astype(BF16)


def _attn_kernel(q_ref, k0, k1, k2, k3, v0, v1, v2, v3, pb_ref, gn_ref, o_ref, *, tab, spans):
    rb = pl.program_id(1)
    pat = jnp.where(rb == 0, 0, jnp.where(rb == pl.num_programs(1) - 1, 2, 1))
    for pat_id in range(N_ROW_PATTERNS):
        @pl.when(pat == pat_id)
        def _(pat_id=pat_id):
            for h in range(N_HEADS):
                _attn_head(h, tab[pat_id], spans[pat_id], q_ref, (k0, k1, k2, k3),
                           (v0, v1, v2, v3), pb_ref, gn_ref, o_ref)


def _attn(qkv, pair_blocks, gn, batch, seq):
    rows = seq // GRID_W
    tq = Q_ROWS * GRID_W
    kc = K_CHUNK_ROWS * GRID_W
    n_rb = rows // Q_ROWS
    n_kc = K_ROWS // K_CHUNK_ROWS
    chunks_per_seq = seq // kc
    tab = _pair_index_table(rows)

    def q_map(b, rb):
        return (b * n_rb + rb, 0)

    def kv_map(col, c):
        def f(b, rb):
            first = jnp.clip(rb * (Q_ROWS // K_CHUNK_ROWS) - (WIN_ROWS // 2) // K_CHUNK_ROWS,
                             0, chunks_per_seq - n_kc)
            return (b * chunks_per_seq + first + c, col)
        return f

    k_specs = [pl.BlockSpec((kc, D_ATTN), kv_map(1, c)) for c in range(n_kc)]
    v_specs = [pl.BlockSpec((kc, D_ATTN), kv_map(2, c)) for c in range(n_kc)]
    return pl.pallas_call(
        functools.partial(_attn_kernel, tab=tab, spans=_chunk_spans(tab)),
        grid=(batch, n_rb),
        in_specs=[pl.BlockSpec((tq, D_ATTN), q_map)] + k_specs + v_specs + [
            pl.BlockSpec(pair_blocks.shape, lambda b, rb: (0, 0, 0, 0),
                         pipeline_mode=pl.Buffered(1)),
            pl.BlockSpec((1, D_ATTN), lambda b, rb: (0, 0)),
        ],
        out_specs=pl.BlockSpec((tq, D_ATTN), q_map),
        out_shape=jax.ShapeDtypeStruct((batch * seq, D_ATTN), BF16),
        compiler_params=_params(("parallel", "arbitrary")),
        name="attn",
    )(qkv, *([qkv] * (2 * n_kc)), pair_blocks, gn)


def _outproj_kernel(x_ref, a_ref, c_ref, wa_ref, wc_ref, o_ref):
    acc = jnp.dot(a_ref[...], wa_ref[...].astype(BF16), preferred_element_type=F32)
    acc = acc + jnp.dot(c_ref[...], wc_ref[...].astype(BF16), preferred_element_type=F32)
    o_ref[...] = x_ref[...] + acc


def _outproj(x, a, c, w_out):
    t, d = x.shape
    tm = OUT_TM
    return pl.pallas_call(
        _outproj_kernel,
        grid=(t // tm,),
        in_specs=[
            pl.BlockSpec((tm, d), lambda i: (i, 0)),
            pl.BlockSpec((tm, D_ATTN), lambda i: (i, 0)),
            pl.BlockSpec((tm, D_CONV), lambda i: (i, 0)),
            pl.BlockSpec((D_ATTN, d), lambda i: (0, 0), pipeline_mode=pl.Buffered(1)),
            pl.BlockSpec((D_CONV, d), lambda i: (1, 0), pipeline_mode=pl.Buffered(1)),
        ],
        out_specs=pl.BlockSpec((tm, d), lambda i: (i, 0)),
        out_shape=jax.ShapeDtypeStruct((t, d), F32),
        compiler_params=_params(("parallel",)),
        name="outproj",
    )(x, a, c, w_out, w_out)


def _ple_kernel(x_ref, p_ref, g_ref, wg_ref, wp_ref, gf_ref, o_ref):
    x = x_ref[...]
    xn = _rms(x, g_ref[...]).astype(BF16)
    gate = jax.nn.sigmoid(jnp.dot(xn, wg_ref[...].astype(BF16), preferred_element_type=F32))
    proj = jnp.dot(p_ref[...].astype(BF16), wp_ref[...].astype(BF16),
                   preferred_element_type=F32)
    o_ref[...] = _rms(x + gate * proj, gf_ref[...])


def _ple(x, p, g, w_gate, w_proj, gf):
    t, d = x.shape
    tm = PLE_TM
    return pl.pallas_call(
        _ple_kernel,
        grid=(t // tm,),
        in_specs=[
            pl.BlockSpec((tm, d), lambda i: (i, 0)),
            pl.BlockSpec((tm, PLE_DIM), lambda i: (i, 0)),
            pl.BlockSpec((1, d), lambda i: (0, 0)),
            pl.BlockSpec((d, d), lambda i: (0, 0), pipeline_mode=pl.Buffered(1)),
            pl.BlockSpec((PLE_DIM, d), lambda i: (0, 0), pipeline_mode=pl.Buffered(1)),
            pl.BlockSpec((1, d), lambda i: (0, 0)),
        ],
        out_specs=pl.BlockSpec((tm, d), lambda i: (i, 0)),
        out_shape=jax.ShapeDtypeStruct((t, d), F32),
        compiler_params=_params(("parallel",)),
        name="ple",
    )(x, p, g, w_gate, w_proj, gf)


def kernel(x, p, ffn1_norm, ffn1_wg, ffn1_wu, ffn1_wd, mix_norm, w_in, rpb, conv_w, conv_b,
           attn_out_norm, conv_out_norm, w_out, ffn2_norm, ffn2_wg, ffn2_wu, ffn2_wd,
           ple_norm, ple_w_gate, ple_w_proj, final_norm):
    b, s, d = x.shape
    assert ffn1_wg.shape[0] == 1
    assert d == D_MODEL and s % (Q_ROWS * GRID_W) == 0 and s % PROJ_TM == 0
    t = b * s
    h = x.reshape(t, d)
    row = lambda v: v.reshape(1, -1)
    for i in range(1):
        h = _ffn(h, row(ffn1_norm[i]), ffn1_wg[i], ffn1_wu[i], ffn1_wd[i])
        qkv, y_conv = _inproj(h, row(mix_norm[i]), w_in[i], conv_w[i], row(conv_b[i]),
                              row(conv_out_norm[i]), s)
        y_attn = _attn(qkv, _pair_blocks(rpb[i]), row(attn_out_norm[i]), b, s)
        h = _outproj(h, y_attn, y_conv, w_out[i])
        h = _ffn(h, row(ffn2_norm[i]), ffn2_wg[i], ffn2_wu[i], ffn2_wd[i])
        h = _ple(h, p[i].reshape(t, PLE_DIM), row(ple_norm[i]), ple_w_gate[i], ple_w_proj[i],
                 row(final_norm))
    return h.reshape(b, s, d)
```

```python
import functools

import numpy as np
import jax
import jax.numpy as jnp
from jax import lax
from jax.experimental import pallas as pl
from jax.experimental.pallas import tpu as pltpu

D_MODEL = 2048
GRID_W = 64
PLE_DIM = 256
D_ATTN = D_MODEL // 2
D_CONV = D_MODEL - D_ATTN
N_HEADS = 8
HEAD_DIM = D_ATTN // N_HEADS
GROUP_DIM = 128
CONV_W = 3
WIN_ROWS = 8
WIN_COLS = 16
RMS_EPS = 1e-6
NEG = -1e30
LOG2E = 1.4426950408889634
Q_SCALE = HEAD_DIM ** -0.5 * LOG2E

BF16 = jnp.bfloat16
F32 = jnp.float32

VMEM_LIMIT = 56 * 1024 * 1024
BIG_VMEM_LIMIT = 60 * 1024 * 1024
FFN_TM = 1024
FFN_TF = 512
PROJ_TM = 1024
PROJ_TN = 1024
CONV_TC = 256
CONV_SUBTILES = 2
HALO = 16
Q_ROWS = 8
K_ROWS = 16
K_CHUNK_ROWS = 4
OUT_TM = 512
PLE_TM = 512


def _rms(x, g):
    ms = jnp.mean(x * x, axis=-1, keepdims=True)
    return x * lax.rsqrt(ms + RMS_EPS) * g


def _params(sem, vmem_limit=VMEM_LIMIT):
    return pltpu.CompilerParams(dimension_semantics=sem, vmem_limit_bytes=vmem_limit)


def _ffn_kernel(x_hbm, g_ref, wg_ref, wu_ref, wd_ref, o_ref, xbuf, xn_ref, sem):
    i = pl.program_id(0)
    j = pl.program_id(1)
    tm = xbuf.shape[0]

    def x_copy(tile):
        rows = pl.ds(pl.multiple_of(tile * tm, tm), tm)
        return pltpu.make_async_copy(x_hbm.at[rows, :], xbuf, sem)

    def half_swiglu(xn):
        gate = jnp.dot(xn, wg_ref[...].astype(BF16), preferred_element_type=F32)
        up = jnp.dot(xn, wu_ref[...].astype(BF16), preferred_element_type=F32)
        act = (jax.nn.silu(gate) * up * 0.5).astype(BF16)
        return jnp.dot(act, wd_ref[...].astype(BF16), preferred_element_type=F32)

    @pl.when(j == 0)
    def _():
        @pl.when(i == 0)
        def _():
            x_copy(0).start()

        x_copy(i).wait()
        x = xbuf[...]
        xn = _rms(x, g_ref[...]).astype(BF16)
        xn_ref[...] = xn
        o_ref[...] = x + half_swiglu(xn)

    @pl.when((j == 1) & (i + 1 < pl.num_programs(0)))
    def _():
        x_copy(i + 1).start()

    @pl.when(j > 0)
    def _():
        o_ref[...] += half_swiglu(xn_ref[...])


def _ffn(x, g, wg, wu, wd):
    t, d = x.shape
    f = wg.shape[1]
    return pl.pallas_call(
        _ffn_kernel,
        grid=(t // FFN_TM, f // FFN_TF),
        in_specs=[
            pl.BlockSpec(memory_space=pl.ANY),
            pl.BlockSpec((1, d), lambda i, j: (0, 0)),
            pl.BlockSpec((d, FFN_TF), lambda i, j: (0, j)),
            pl.BlockSpec((d, FFN_TF), lambda i, j: (0, j)),
            pl.BlockSpec((FFN_TF, d), lambda i, j: (j, 0)),
        ],
        out_specs=pl.BlockSpec((FFN_TM, d), lambda i, j: (i, 0)),
        out_shape=jax.ShapeDtypeStruct((t, d), F32),
        scratch_shapes=[pltpu.VMEM((FFN_TM, d), F32),
                        pltpu.VMEM((FFN_TM, d), BF16),
                        pltpu.SemaphoreType.DMA(())],
        compiler_params=_params(("arbitrary", "arbitrary"), BIG_VMEM_LIMIT),
        name="ffn",
    )(x, g, wg, wu, wd)


N_QKV_STEPS = 3 * D_ATTN // PROJ_TN
N_CONV_STEPS = D_CONV // CONV_TC


def _inproj_kernel(x_ref, xp_ref, xnx_ref, g_ref, wq_ref, wb_ref, wc_ref, wu_ref, cw_ref,
                   cb_ref, gn_ref, qkv_ref, conv_ref, xn_ref, cu_ref, *, seq):
    tm = x_ref.shape[0]
    j = pl.program_id(1)

    def qkv_tile(xn):
        acc = jnp.dot(xn, wq_ref[...].astype(BF16), preferred_element_type=F32)
        col_scale = jnp.where(j < D_ATTN // PROJ_TN, Q_SCALE, 1.0)
        qkv_ref[...] = (acc * col_scale).astype(BF16)

    @pl.when(j == 0)
    def _():
        g = g_ref[...]
        xn_ref[0:HALO, :] = _rms(xp_ref[...], g).astype(BF16)
        xn_ref[HALO + tm:, :] = _rms(xnx_ref[...], g).astype(BF16)
        xn = _rms(x_ref[...], g).astype(BF16)
        xn_ref[HALO:HALO + tm, :] = xn
        qkv_tile(xn)

    @pl.when((j > 0) & (j < N_QKV_STEPS))
    def _():
        qkv_tile(xn_ref[HALO:HALO + tm, :])

    @pl.when(j >= N_QKV_STEPS)
    def _():
        wb = wb_ref[...].astype(BF16)
        wc = wc_ref[...].astype(BF16)
        wu = wu_ref[...].astype(BF16)
        cw = cw_ref[...]
        cb = cb_ref[...]
        gn = gn_ref[...]
        ts = tm // CONV_SUBTILES
        for sub in range(CONV_SUBTILES):
            r0 = sub * ts
            xs = xn_ref[r0:r0 + ts + 2 * HALO, :]
            gate_b = jnp.dot(xn_ref[HALO + r0:HALO + r0 + ts, :], wb, preferred_element_type=F32)
            gate_c = jnp.dot(xs, wc, preferred_element_type=F32)
            u = jnp.dot(xs, wu, preferred_element_type=F32)
            cu_ref[sub] = gate_c * u

            pos = (lax.broadcasted_iota(jnp.int32, (ts, 1), 0) + pl.program_id(0) * tm + r0) % seq
            prev = jnp.where(pos == 0, 0.0, cu_ref[sub, HALO - 1:HALO - 1 + ts, :])
            cur = cu_ref[sub, HALO:HALO + ts, :]
            nxt = jnp.where(pos == seq - 1, 0.0, cu_ref[sub, HALO + 1:HALO + 1 + ts, :])
            y = prev * cw[0:1, :]
            y = y + cur * cw[1:2, :]
            y = y + nxt * cw[2:3, :]
            y = gate_b * (y + cb)
            for c in range(0, y.shape[1], GROUP_DIM):
                sl = slice(c, c + GROUP_DIM)
                conv_ref[r0:r0 + ts, sl] = _rms(y[:, sl], gn[:, sl]).astype(BF16)


def _inproj(x, g, w_in, conv_w, conv_b, gn, seq):
    t, d = x.shape
    tm, tn, tc = PROJ_TM, PROJ_TN, CONV_TC
    hb = tm // HALO
    nhb = t // HALO
    col_b = 3 * D_ATTN // tc
    col_c = col_b + D_CONV // tc
    col_u = col_c + D_CONV // tc
    qstep = lambda j: jnp.minimum(j, N_QKV_STEPS - 1)
    cstep = lambda j: jnp.maximum(j - N_QKV_STEPS, 0)
    n_tiles = t // tm
    xtile = lambda i, j: jnp.minimum(i + jnp.where(j > 0, 1, 0), n_tiles - 1)
    qwstep = lambda j: jnp.where(j < N_QKV_STEPS, j, 0)
    return pl.pallas_call(
        functools.partial(_inproj_kernel, seq=seq),
        grid=(n_tiles, N_QKV_STEPS + N_CONV_STEPS),
        in_specs=[
            pl.BlockSpec((tm, d), lambda i, j: (xtile(i, j), 0)),
            pl.BlockSpec((HALO, d), lambda i, j: (jnp.maximum(xtile(i, j) * hb - 1, 0), 0)),
            pl.BlockSpec((HALO, d),
                         lambda i, j: (jnp.minimum((xtile(i, j) + 1) * hb, nhb - 1), 0)),
            pl.BlockSpec((1, d), lambda i, j: (0, 0)),
            pl.BlockSpec((d, tn), lambda i, j: (0, qwstep(j))),
            pl.BlockSpec((d, tc), lambda i, j: (0, col_b + cstep(j))),
            pl.BlockSpec((d, tc), lambda i, j: (0, col_c + cstep(j))),
            pl.BlockSpec((d, tc), lambda i, j: (0, col_u + cstep(j))),
            pl.BlockSpec((CONV_W, tc), lambda i, j: (0, cstep(j))),
            pl.BlockSpec((1, tc), lambda i, j: (0, cstep(j))),
            pl.BlockSpec((1, tc), lambda i, j: (0, cstep(j))),
        ],
        out_specs=[pl.BlockSpec((tm, tn), lambda i, j: (i, qstep(j))),
                   pl.BlockSpec((tm, tc), lambda i, j: (i, cstep(j)))],
        out_shape=[jax.ShapeDtypeStruct((t, 3 * D_ATTN), BF16),
                   jax.ShapeDtypeStruct((t, D_CONV), BF16)],
        scratch_shapes=[pltpu.VMEM((tm + 2 * HALO, d), BF16),
                        pltpu.VMEM((CONV_SUBTILES, tm // CONV_SUBTILES + 2 * HALO, tc), F32)],
        compiler_params=_params(("parallel", "arbitrary"), BIG_VMEM_LIMIT),
        name="inproj",
    )(x, x, x, g, w_in, w_in, w_in, w_in, conv_w, conv_b, gn)


N_REL = 2 * WIN_ROWS - 1
PAIR_FULL = 0
PAIR_LEFT_NEG = N_REL - 1
PAIR_RIGHT_NEG = 2 * N_REL - 1
PAIR_NEG = 3 * N_REL - 1
N_PAIR_BLOCKS = 3 * N_REL
N_ROW_PATTERNS = 3
N_PAIRS = K_ROWS // 2


def _pair_blocks(rpb):
    nh, nr, nc = rpb.shape
    w = GRID_W
    c = np.arange(w)
    col_start = np.clip(c - WIN_COLS // 2, 0, w - WIN_COLS)
    in_win = (c[None, :] >= col_start[:, None]) & (c[None, :] < col_start[:, None] + WIN_COLS)
    left = w - WIN_COLS
    u = jnp.pad(rpb, ((0, 0), (0, 0), (left, 2 * w - 1 - nc - left)))
    toep = jnp.stack([u[..., w - 1 - q:2 * w - 1 - q] for q in range(w)], axis=-2)
    toep = jnp.where(in_win[None, None], toep * LOG2E, NEG)
    neg = jnp.full((nh, nr, w, w), NEG, F32)
    return jnp.concatenate([
        jnp.concatenate([toep[:, :-1], toep[:, 1:]], axis=-1),
        jnp.concatenate([neg, toep], axis=-1),
        jnp.concatenate([toep, neg], axis=-1),
        jnp.concatenate([neg[:, :1], neg[:, :1]], axis=-1)], axis=1)


def _pair_index_table(rows):
    tab = np.zeros((N_ROW_PATTERNS, Q_ROWS, N_PAIRS), np.int32)
    for pat, r0 in enumerate((0, Q_ROWS, rows - Q_ROWS)):
        k0 = min(max(r0 - WIN_ROWS // 2, 0), rows - K_ROWS)
        for i in range(Q_ROWS):
            r = r0 + i
            row_start = min(max(r - WIN_ROWS // 2, 0), rows - WIN_ROWS)
            for m in range(N_PAIRS):
                kr = k0 + 2 * m
                in0 = row_start <= kr < row_start + WIN_ROWS
                in1 = row_start <= kr + 1 < row_start + WIN_ROWS
                a = kr - r + WIN_ROWS - 1
                if in0 and in1:
                    tab[pat, i, m] = PAIR_FULL + a
                elif in1:
                    tab[pat, i, m] = PAIR_LEFT_NEG + a + 1
                elif in0:
                    tab[pat, i, m] = PAIR_RIGHT_NEG + a
                else:
                    tab[pat, i, m] = PAIR_NEG
    return tab


def _chunk_spans(tab):
    pairs_per_chunk = K_CHUNK_ROWS // 2
    spans = []
    for pat in range(tab.shape[0]):
        row = []
        for c in range(K_ROWS // K_CHUNK_ROWS):
            blk = tab[pat, :, c * pairs_per_chunk:(c + 1) * pairs_per_chunk]
            hit = np.nonzero((blk != PAIR_NEG).any(axis=1))[0]
            if hit.size == 0:
                row.append((0, 0))
            else:
                assert hit[-1] - hit[0] + 1 == hit.size
                row.append((int(hit[0]), int(hit[-1]) + 1))
        spans.append(row)
    return spans


def _attn_head(h, tab, spans, q_ref, ks, vs, pb_ref, gn_ref, o_ref):
    hs = slice(h * HEAD_DIM, (h + 1) * HEAD_DIM)
    pairs_per_chunk = K_CHUNK_ROWS // 2
    live = [c for c, (first, last) in enumerate(spans) if last > first]
    w = GRID_W
    scores = {}
    m = [None] * Q_ROWS
    for c in live:
        first, last = spans[c]
        s = lax.dot_general(q_ref[first * w:last * w, hs], ks[c][:, hs],
                            (((1,), (1,)), ((), ())), preferred_element_type=F32)
        for i in range(first, last):
            bias = jnp.concatenate([pb_ref[h, int(tab[i, c * pairs_per_chunk + k])]
                                    for k in range(pairs_per_chunk)], axis=1)
            si = s[(i - first) * w:(i - first + 1) * w] + bias
            scores[c, i] = si
            mi = jnp.max(si, axis=1, keepdims=True)
            m[i] = mi if m[i] is None else jnp.maximum(m[i], mi)
    l = [None] * Q_ROWS
    o = [None] * Q_ROWS
    for c in live:
        first, last = spans[c]
        ps = []
        for i in range(first, last):
            p = jnp.exp2(scores[c, i] - m[i])
            li = jnp.sum(p, axis=1, keepdims=True)
            l[i] = li if l[i] is None else l[i] + li
            ps.append(p.astype(BF16))
        oc = jnp.dot(jnp.concatenate(ps, axis=0), vs[c][:, hs], preferred_element_type=F32)
        for i in range(first, last):
            oi = oc[(i - first) * w:(i - first + 1) * w]
            o[i] = oi if o[i] is None else o[i] + oi
    gn = gn_ref[:, hs]
    for i in range(Q_ROWS):
        o_ref[i * w:(i + 1) * w, hs] = _rms(o[i] / l[i], gn).astype(BF16)


def _attn_kernel(q_ref, k0, k1, k2, k3, v0, v1, v2, v3, pb_ref, gn_ref, o_ref, *, tab, spans):
    rb = pl.program_id(1)
    pat = jnp.where(rb == 0, 0, jnp.where(rb == pl.num_programs(1) - 1, 2, 1))
    for pat_id in range(N_ROW_PATTERNS):
        @pl.when(pat == pat_id)
        def _(pat_id=pat_id):
            for h in range(N_HEADS):
                _attn_head(h, tab[pat_id], spans[pat_id], q_ref, (k0, k1, k2, k3),
                           (v0, v1, v2, v3), pb_ref, gn_ref, o_ref)


def _attn(qkv, pair_blocks, gn, batch, seq):
    rows = seq // GRID_W
    tq = Q_ROWS * GRID_W
    kc = K_CHUNK_ROWS * GRID_W
    n_rb = rows // Q_ROWS
    n_kc = K_ROWS // K_CHUNK_ROWS
    chunks_per_seq = seq // kc
    tab = _pair_index_table(rows)

    def q_map(b, rb):
        return (b * n_rb + rb, 0)

    def kv_map(col, c):
        def f(b, rb):
            first = jnp.clip(rb * (Q_ROWS // K_CHUNK_ROWS) - (WIN_ROWS // 2) // K_CHUNK_ROWS,
                             0, chunks_per_seq - n_kc)
            return (b * chunks_per_seq + first + c, col)
        return f

    k_specs = [pl.BlockSpec((kc, D_ATTN), kv_map(1, c)) for c in range(n_kc)]
    v_specs = [pl.BlockSpec((kc, D_ATTN), kv_map(2, c)) for c in range(n_kc)]
    return pl.pallas_call(
        functools.partial(_attn_kernel, tab=tab, spans=_chunk_spans(tab)),
        grid=(batch, n_rb),
        in_specs=[pl.BlockSpec((tq, D_ATTN), q_map)] + k_specs + v_specs + [
            pl.BlockSpec(pair_blocks.shape, lambda b, rb: (0, 0, 0, 0),
                         pipeline_mode=pl.Buffered(1)),
            pl.BlockSpec((1, D_ATTN), lambda b, rb: (0, 0)),
        ],
        out_specs=pl.BlockSpec((tq, D_ATTN), q_map),
        out_shape=jax.ShapeDtypeStruct((batch * seq, D_ATTN), BF16),
        compiler_params=_params(("parallel", "arbitrary")),
        name="attn",
    )(qkv, *([qkv] * (2 * n_kc)), pair_blocks, gn)


def _outproj_kernel(x_ref, a_ref, c_ref, wa_ref, wc_ref, o_ref):
    acc = jnp.dot(a_ref[...], wa_ref[...].astype(BF16), preferred_element_type=F32)
    acc = acc + jnp.dot(c_ref[...], wc_ref[...].astype(BF16), preferred_element_type=F32)
    o_ref[...] = x_ref[...] + acc


def _outproj(x, a, c, w_out):
    t, d = x.shape
    tm = OUT_TM
    return pl.pallas_call(
        _outproj_kernel,
        grid=(t // tm,),
        in_specs=[
            pl.BlockSpec((tm, d), lambda i: (i, 0)),
            pl.BlockSpec((tm, D_ATTN), lambda i: (i, 0)),
            pl.BlockSpec((tm, D_CONV), lambda i: (i, 0)),
            pl.BlockSpec((D_ATTN, d), lambda i: (0, 0), pipeline_mode=pl.Buffered(1)),
            pl.BlockSpec((D_CONV, d), lambda i: (1, 0), pipeline_mode=pl.Buffered(1)),
        ],
        out_specs=pl.BlockSpec((tm, d), lambda i: (i, 0)),
        out_shape=jax.ShapeDtypeStruct((t, d), F32),
        compiler_params=_params(("parallel",)),
        name="outproj",
    )(x, a, c, w_out, w_out)


def _ple_kernel(x_ref, p_ref, g_ref, wg_ref, wp_ref, gf_ref, o_ref):
    x = x_ref[...]
    xn = _rms(x, g_ref[...]).astype(BF16)
    gate = jax.nn.sigmoid(jnp.dot(xn, wg_ref[...].astype(BF16), preferred_element_type=F32))
    proj = jnp.dot(p_ref[...].astype(BF16), wp_ref[...].astype(BF16),
                   preferred_element_type=F32)
    o_ref[...] = _rms(x + gate * proj, gf_ref[...])


def _ple(x, p, g, w_gate, w_proj, gf):
    t, d = x.shape
    tm = PLE_TM
    return pl.pallas_call(
        _ple_kernel,
        grid=(t // tm,),
        in_specs=[
            pl.BlockSpec((tm, d), lambda i: (i, 0)),
            pl.BlockSpec((tm, PLE_DIM), lambda i: (i, 0)),
            pl.BlockSpec((1, d), lambda i: (0, 0)),
            pl.BlockSpec((d, d), lambda i: (0, 0), pipeline_mode=pl.Buffered(1)),
            pl.BlockSpec((PLE_DIM, d), lambda i: (0, 0), pipeline_mode=pl.Buffered(1)),
            pl.BlockSpec((1, d), lambda i: (0, 0)),
        ],
        out_specs=pl.BlockSpec((tm, d), lambda i: (i, 0)),
        out_shape=jax.ShapeDtypeStruct((t, d), F32),
        compiler_params=_params(("parallel",)),
        name="ple",
    )(x, p, g, w_gate, w_proj, gf)


def kernel(x, p, ffn1_norm, ffn1_wg, ffn1_wu, ffn1_wd, mix_norm, w_in, rpb, conv_w, conv_b,
           attn_out_norm, conv_out_norm, w_out, ffn2_norm, ffn2_wg, ffn2_wu, ffn2_wd,
           ple_norm, ple_w_gate, ple_w_proj, final_norm):
    b, s, d = x.shape
    assert ffn1_wg.shape[0] == 1
    assert d == D_MODEL and s % (Q_ROWS * GRID_W) == 0 and s % PROJ_TM == 0
    t = b * s
    h = x.reshape(t, d)
    row = lambda v: v.reshape(1, -1)
    for i in range(1):
        h = _ffn(h, row(ffn1_norm[i]), ffn1_wg[i], ffn1_wu[i], ffn1_wd[i])
        qkv, y_conv = _inproj(h, row(mix_norm[i]), w_in[i], conv_w[i], row(conv_b[i]),
                              row(conv_out_norm[i]), s)
        y_attn = _attn(qkv, _pair_blocks(rpb[i]), row(attn_out_norm[i]), b, s)
        h = _outproj(h, y_attn, y_conv, w_out[i])
        h = _ffn(h, row(ffn2_norm[i]), ffn2_wg[i], ffn2_wu[i], ffn2_wd[i])
        h = _ple(h, p[i].reshape(t, PLE_DIM), row(ple_norm[i]), ple_w_gate[i], ple_w_proj[i],
                 row(final_norm))
    return h.reshape(b, s, d)
```

```python
import functools

import numpy as np
import jax
import jax.numpy as jnp
from jax import lax
from jax.experimental import pallas as pl
from jax.experimental.pallas import tpu as pltpu

D_MODEL = 2048
GRID_W = 64
PLE_DIM = 256
D_ATTN = D_MODEL // 2
D_CONV = D_MODEL - D_ATTN
N_HEADS = 8
HEAD_DIM = D_ATTN // N_HEADS
GROUP_DIM = 128
CONV_W = 3
WIN_ROWS = 8
WIN_COLS = 16
RMS_EPS = 1e-6
NEG = -1e30
LOG2E = 1.4426950408889634
Q_SCALE = HEAD_DIM ** -0.5 * LOG2E

BF16 = jnp.bfloat16
F32 = jnp.float32

VMEM_LIMIT = 56 * 1024 * 1024
BIG_VMEM_LIMIT = 60 * 1024 * 1024
FFN_TM = 1024
FFN_TF = 512
PROJ_TM = 1024
PROJ_TN = 1024
CONV_TC = 256
CONV_SUBTILES = 2
HALO = 16
Q_ROWS = 8
K_ROWS = 16
K_CHUNK_ROWS = 4
OUT_TM = 512
PLE_TM = 512


def _rms(x, g):
    ms = jnp.mean(x * x, axis=-1, keepdims=True)
    return x * lax.rsqrt(ms + RMS_EPS) * g


def _params(sem, vmem_limit=VMEM_LIMIT):
    return pltpu.CompilerParams(dimension_semantics=sem, vmem_limit_bytes=vmem_limit)


def _ffn_kernel(x_hbm, g_ref, wg_ref, wu_ref, wd_ref, o_ref, xbuf, xn_ref, sem):
    i = pl.program_id(0)
    j = pl.program_id(1)
    tm = xbuf.shape[0]

    def x_copy(tile):
        rows = pl.ds(pl.multiple_of(tile * tm, tm), tm)
        return pltpu.make_async_copy(x_hbm.at[rows, :], xbuf, sem)

    def half_swiglu(xn):
        gate = jnp.dot(xn, wg_ref[...].astype(BF16), preferred_element_type=F32)
        up = jnp.dot(xn, wu_ref[...].astype(BF16), preferred_element_type=F32)
        act = (jax.nn.silu(gate) * up * 0.5).astype(BF16)
        return jnp.dot(act, wd_ref[...].astype(BF16), preferred_element_type=F32)

    @pl.when(j == 0)
    def _():
        @pl.when(i == 0)
        def _():
            x_copy(0).start()

        x_copy(i).wait()
        x = xbuf[...]
        xn = _rms(x, g_ref[...]).astype(BF16)
        xn_ref[...] = xn
        o_ref[...] = x + half_swiglu(xn)

    @pl.when((j == 1) & (i + 1 < pl.num_programs(0)))
    def _():
        x_copy(i + 1).start()

    @pl.when(j > 0)
    def _():
        o_ref[...] += half_swiglu(xn_ref[...])


def _ffn(x, g, wg, wu, wd):
    t, d = x.shape
    f = wg.shape[1]
    return pl.pallas_call(
        _ffn_kernel,
        grid=(t // FFN_TM, f // FFN_TF),
        in_specs=[
            pl.BlockSpec(memory_space=pl.ANY),
            pl.BlockSpec((1, d), lambda i, j: (0, 0)),
            pl.BlockSpec((d, FFN_TF), lambda i, j: (0, j)),
            pl.BlockSpec((d, FFN_TF), lambda i, j: (0, j)),
            pl.BlockSpec((FFN_TF, d), lambda i, j: (j, 0)),
        ],
        out_specs=pl.BlockSpec((FFN_TM, d), lambda i, j: (i, 0)),
        out_shape=jax.ShapeDtypeStruct((t, d), F32),
        scratch_shapes=[pltpu.VMEM((FFN_TM, d), F32),
                        pltpu.VMEM((FFN_TM, d), BF16),
                        pltpu.SemaphoreType.DMA(())],
        compiler_params=_params(("arbitrary", "arbitrary"), BIG_VMEM_LIMIT),
        name="ffn",
    )(x, g, wg, wu, wd)


N_QKV_STEPS = 3 * D_ATTN // PROJ_TN
N_CONV_STEPS = D_CONV // CONV_TC


def _inproj_kernel(xa_ref, xb_ref, xp_ref, xnx_ref, g_ref, wq_ref, wb_ref, wc_ref, wu_ref,
                   cw_ref, cb_ref, gn_ref, qkv_ref, conv_ref, xn_ref, cu_ref, *, seq):
    tm = xa_ref.shape[0] + xb_ref.shape[0]
    j = pl.program_id(1)

    def qkv_tile(xn):
        acc = jnp.dot(xn, wq_ref[...].astype(BF16), preferred_element_type=F32)
        col_scale = jnp.where(j < D_ATTN // PROJ_TN, Q_SCALE, 1.0)
        qkv_ref[...] = (acc * col_scale).astype(BF16)

    @pl.when(j == 0)
    def _():
        g = g_ref[...]
        xn_ref[0:HALO, :] = _rms(xp_ref[...], g).astype(BF16)
        xn_ref[HALO + tm:, :] = _rms(xnx_ref[...], g).astype(BF16)
        xn = jnp.concatenate([_rms(xa_ref[...], g), _rms(xb_ref[...], g)], axis=0).astype(BF16)
        xn_ref[HALO:HALO + tm, :] = xn
        qkv_tile(xn)

    @pl.when((j > 0) & (j < N_QKV_STEPS))
    def _():
        qkv_tile(xn_ref[HALO:HALO + tm, :])

    @pl.when(j >= N_QKV_STEPS)
    def _():
        wb = wb_ref[...].astype(BF16)
        wc = wc_ref[...].astype(BF16)
        wu = wu_ref[...].astype(BF16)
        cw = cw_ref[...]
        cb = cb_ref[...]
        gn = gn_ref[...]
        ts = tm // CONV_SUBTILES
        for sub in range(CONV_SUBTILES):
            r0 = sub * ts
            xs = xn_ref[r0:r0 + ts + 2 * HALO, :]
            gate_b = jnp.dot(xn_ref[HALO + r0:HALO + r0 + ts, :], wb, preferred_element_type=F32)
            gate_c = jnp.dot(xs, wc, preferred_element_type=F32)
            u = jnp.dot(xs, wu, preferred_element_type=F32)
            cu_ref[sub] = gate_c * u

            pos = (lax.broadcasted_iota(jnp.int32, (ts, 1), 0) + pl.program_id(0) * tm + r0) % seq
            prev = jnp.where(pos == 0, 0.0, cu_ref[sub, HALO - 1:HALO - 1 + ts, :])
            cur = cu_ref[sub, HALO:HALO + ts, :]
            nxt = jnp.where(pos == seq - 1, 0.0, cu_ref[sub, HALO + 1:HALO + 1 + ts, :])
            y = prev * cw[0:1, :]
            y = y + cur * cw[1:2, :]
            y = y + nxt * cw[2:3, :]
            y = gate_b * (y + cb)
            for c in range(0, y.shape[1], GROUP_DIM):
                sl = slice(c, c + GROUP_DIM)
                conv_ref[r0:r0 + ts, sl] = _rms(y[:, sl], gn[:, sl]).astype(BF16)


def _inproj(x, g, w_in, conv_w, conv_b, gn, seq):
    t, d = x.shape
    tm, tn, tc = PROJ_TM, PROJ_TN, CONV_TC
    hb = tm // HALO
    nhb = t // HALO
    col_b = 3 * D_ATTN // tc
    col_c = col_b + D_CONV // tc
    col_u = col_c + D_CONV // tc
    qstep = lambda j: jnp.minimum(j, N_QKV_STEPS - 1)
    cstep = lambda j: jnp.maximum(j - N_QKV_STEPS, 0)
    n_tiles = t // tm
    xtile = lambda i, j, after: jnp.minimum(i + jnp.where(j > after, 1, 0), n_tiles - 1)
    qwstep = lambda j: jnp.where(j < N_QKV_STEPS, j, 0)
    return pl.pallas_call(
        functools.partial(_inproj_kernel, seq=seq),
        grid=(n_tiles, N_QKV_STEPS + N_CONV_STEPS),
        in_specs=[
            pl.BlockSpec((tm // 2, d), lambda i, j: (2 * xtile(i, j, N_QKV_STEPS), 0)),
            pl.BlockSpec((tm // 2, d), lambda i, j: (2 * xtile(i, j, N_QKV_STEPS + 1) + 1, 0)),
            pl.BlockSpec((HALO, d), lambda i, j: (jnp.maximum(xtile(i, j, 0) * hb - 1, 0), 0)),
            pl.BlockSpec((HALO, d),
                         lambda i, j: (jnp.minimum((xtile(i, j, 0) + 1) * hb, nhb - 1), 0)),
            pl.BlockSpec((1, d), lambda i, j: (0, 0)),
            pl.BlockSpec((d, tn), lambda i, j: (0, qwstep(j))),
            pl.BlockSpec((d, tc), lambda i, j: (0, col_b + cstep(j))),
            pl.BlockSpec((d, tc), lambda i, j: (0, col_c + cstep(j))),
            pl.BlockSpec((d, tc), lambda i, j: (0, col_u + cstep(j))),
            pl.BlockSpec((CONV_W, tc), lambda i, j: (0, cstep(j))),
            pl.BlockSpec((1, tc), lambda i, j: (0, cstep(j))),
            pl.BlockSpec((1, tc), lambda i, j: (0, cstep(j))),
        ],
        out_specs=[pl.BlockSpec((tm, tn), lambda i, j: (i, qstep(j))),
                   pl.BlockSpec((tm, tc), lambda i, j: (i, cstep(j)))],
        out_shape=[jax.ShapeDtypeStruct((t, 3 * D_ATTN), BF16),
                   jax.ShapeDtypeStruct((t, D_CONV), BF16)],
        scratch_shapes=[pltpu.VMEM((tm + 2 * HALO, d), BF16),
                        pltpu.VMEM((CONV_SUBTILES, tm // CONV_SUBTILES + 2 * HALO, tc), F32)],
        compiler_params=_params(("parallel", "arbitrary"), BIG_VMEM_LIMIT),
        name="inproj",
    )(x, x, x, x, g, w_in, w_in, w_in, w_in, conv_w, conv_b, gn)


N_REL = 2 * WIN_ROWS - 1
PAIR_FULL = 0
PAIR_LEFT_NEG = N_REL - 1
PAIR_RIGHT_NEG = 2 * N_REL - 1
PAIR_NEG = 3 * N_REL - 1
N_PAIR_BLOCKS = 3 * N_REL
N_ROW_PATTERNS = 3
N_PAIRS = K_ROWS // 2


def _pair_blocks(rpb):
    nh, nr, nc = rpb.shape
    w = GRID_W
    c = np.arange(w)
    col_start = np.clip(c - WIN_COLS // 2, 0, w - WIN_COLS)
    in_win = (c[None, :] >= col_start[:, None]) & (c[None, :] < col_start[:, None] + WIN_COLS)
    period = 2 * w
    left = w - WIN_COLS
    u = jnp.pad(rpb, ((0, 0), (0, 0), (left, period - nc - left)))
    toep = jnp.tile(u, (1, 1, w))[..., :w * (period - 1)].reshape(nh, nr, w, period - 1)[..., w - 1:]
    toep = jnp.where(in_win[None, None], toep * LOG2E, NEG)
    neg = jnp.full((nh, nr, w, w), NEG, F32)
    return jnp.concatenate([
        jnp.concatenate([toep[:, :-1], toep[:, 1:]], axis=-1),
        jnp.concatenate([neg, toep], axis=-1),
        jnp.concatenate([toep, neg], axis=-1),
        jnp.concatenate([neg[:, :1], neg[:, :1]], axis=-1)], axis=1)


def _pair_index_table(rows):
    tab = np.zeros((N_ROW_PATTERNS, Q_ROWS, N_PAIRS), np.int32)
    for pat, r0 in enumerate((0, Q_ROWS, rows - Q_ROWS)):
        k0 = min(max(r0 - WIN_ROWS // 2, 0), rows - K_ROWS)
        for i in range(Q_ROWS):
            r = r0 + i
            row_start = min(max(r - WIN_ROWS // 2, 0), rows - WIN_ROWS)
            for m in range(N_PAIRS):
                kr = k0 + 2 * m
                in0 = row_start <= kr < row_start + WIN_ROWS
                in1 = row_start <= kr + 1 < row_start + WIN_ROWS
                a = kr - r + WIN_ROWS - 1
                if in0 and in1:
                    tab[pat, i, m] = PAIR_FULL + a
                elif in1:
                    tab[pat, i, m] = PAIR_LEFT_NEG + a + 1
                elif in0:
                    tab[pat, i, m] = PAIR_RIGHT_NEG + a
                else:
                    tab[pat, i, m] = PAIR_NEG
    return tab


def _chunk_spans(tab):
    pairs_per_chunk = K_CHUNK_ROWS // 2
    spans = []
    for pat in range(tab.shape[0]):
        row = []
        for c in range(K_ROWS // K_CHUNK_ROWS):
            blk = tab[pat, :, c * pairs_per_chunk:(c + 1) * pairs_per_chunk]
            hit = np.nonzero((blk != PAIR_NEG).any(axis=1))[0]
            if hit.size == 0:
                row.append((0, 0))
            else:
                assert hit[-1] - hit[0] + 1 == hit.size
                row.append((int(hit[0]), int(hit[-1]) + 1))
        spans.append(row)
    return spans


def _attn_head(h, tab, spans, q_ref, ks, vs, pb_ref, gn_ref, o_ref):
    hs = slice(h * HEAD_DIM, (h + 1) * HEAD_DIM)
    pairs_per_chunk = K_CHUNK_ROWS // 2
    live = [c for c, (first, last) in enumerate(spans) if last > first]
    w = GRID_W
    scores = {}
    m = [None] * Q_ROWS
    for c in live:
        first, last = spans[c]
        s = lax.dot_general(q_ref[first * w:last * w, hs], ks[c][:, hs],
                            (((1,), (1,)), ((), ())), preferred_element_type=F32)
        for i in range(first, last):
            bias = jnp.concatenate([pb_ref[h, int(tab[i, c * pairs_per_chunk + k])]
                                    for k in range(pairs_per_chunk)], axis=1)
            si = s[(i - first) * w:(i - first + 1) * w] + bias
            scores[c, i] = si
            mi = jnp.max(si, axis=1, keepdims=True)
            m[i] = mi if m[i] is None else jnp.maximum(m[i], mi)
    l = [None] * Q_ROWS
    o = [None] * Q_ROWS
    for c in live:
        first, last = spans[c]
        ps = []
        for i in range(first, last):
            p = jnp.exp2(scores[c, i] - m[i])
            li = jnp.sum(p, axis=1, keepdims=True)
            l[i] = li if l[i] is None else l[i] + li
            ps.append(p.astype(BF16))
        oc = jnp.dot(jnp.concatenate(ps, axis=0), vs[c][:, hs], preferred_element_type=F32)
        for i in range(first, last):
            oi = oc[(i - first) * w:(i - first + 1) * w]
            o[i] = oi if o[i] is None else o[i] + oi
    gn = gn_ref[:, hs]
    for i in range(Q_ROWS):
        o_ref[i * w:(i + 1) * w, hs] = _rms(o[i] / l[i], gn).astype(BF16)


def _attn_kernel(q_ref, k0, k1, k2, k3, v0, v1, v2, v3, pb_ref, gn_ref, o_ref, *, tab, spans):
    rb = pl.program_id(1)
    pat = jnp.where(rb == 0, 0, jnp.where(rb == pl.num_programs(1) - 1, 2, 1))
    for pat_id in range(N_ROW_PATTERNS):
        @pl.when(pat == pat_id)
        def _(pat_id=pat_id):
            for h in range(N_HEADS):
                _attn_head(h, tab[pat_id], spans[pat_id], q_ref, (k0, k1, k2, k3),
                           (v0, v1, v2, v3), pb_ref, gn_ref, o_ref)


def _attn(qkv, pair_blocks, gn, batch, seq):
    rows = seq // GRID_W
    tq = Q_ROWS * GRID_W
    kc = K_CHUNK_ROWS * GRID_W
    n_rb = rows // Q_ROWS
    n_kc = K_ROWS // K_CHUNK_ROWS
    chunks_per_seq = seq // kc
    tab = _pair_index_table(rows)

    def q_map(b, rb):
        return (b * n_rb + rb, 0)

    def kv_map(col, c):
        def f(b, rb):
            first = jnp.clip(rb * (Q_ROWS // K_CHUNK_ROWS) - (WIN_ROWS // 2) // K_CHUNK_ROWS,
                             0, chunks_per_seq - n_kc)
            return (b * chunks_per_seq + first + c, col)
        return f

    k_specs = [pl.BlockSpec((kc, D_ATTN), kv_map(1, c)) for c in range(n_kc)]
    v_specs = [pl.BlockSpec((kc, D_ATTN), kv_map(2, c)) for c in range(n_kc)]
    return pl.pallas_call(
        functools.partial(_attn_kernel, tab=tab, spans=_chunk_spans(tab)),
        grid=(batch, n_rb),
        in_specs=[pl.BlockSpec((tq, D_ATTN), q_map)] + k_specs + v_specs + [
            pl.BlockSpec(pair_blocks.shape, lambda b, rb: (0, 0, 0, 0),
                         pipeline_mode=pl.Buffered(1)),
            pl.BlockSpec((1, D_ATTN), lambda b, rb: (0, 0)),
        ],
        out_specs=pl.BlockSpec((tq, D_ATTN), q_map),
        out_shape=jax.ShapeDtypeStruct((batch * seq, D_ATTN), BF16),
        compiler_params=_params(("parallel", "arbitrary")),
        name="attn",
    )(qkv, *([qkv] * (2 * n_kc)), pair_blocks, gn)


def _outproj_kernel(x_ref, a_ref, c_ref, wa_ref, wc_ref, o_ref):
    acc = jnp.dot(a_ref[...], wa_ref[...].astype(BF16), preferred_element_type=F32)
    acc = acc + jnp.dot(c_ref[...], wc_ref[...].astype(BF16), preferred_element_type=F32)
    o_ref[...] = x_ref[...] + acc


def _outproj(x, a, c, w_out):
    t, d = x.shape
    tm = OUT_TM
    return pl.pallas_call(
        _outproj_kernel,
        grid=(t // tm,),
        in_specs=[
            pl.BlockSpec((tm, d), lambda i: (i, 0)),
            pl.BlockSpec((tm, D_ATTN), lambda i: (i, 0)),
            pl.BlockSpec((tm, D_CONV), lambda i: (i, 0)),
            pl.BlockSpec((D_ATTN, d), lambda i: (0, 0), pipeline_mode=pl.Buffered(1)),
            pl.BlockSpec((D_CONV, d), lambda i: (1, 0), pipeline_mode=pl.Buffered(1)),
        ],
        out_specs=pl.BlockSpec((tm, d), lambda i: (i, 0)),
        out_shape=jax.ShapeDtypeStruct((t, d), F32),
        compiler_params=_params(("parallel",)),
        name="outproj",
    )(x, a, c, w_out, w_out)


def _ple_kernel(x_ref, p_ref, g_ref, wg_ref, wp_ref, gf_ref, o_ref):
    x = x_ref[...]
    xn = _rms(x, g_ref[...]).astype(BF16)
    gate = jax.nn.sigmoid(jnp.dot(xn, wg_ref[...].astype(BF16), preferred_element_type=F32))
    proj = jnp.dot(p_ref[...].astype(BF16), wp_ref[...].astype(BF16),
                   preferred_element_type=F32)
    o_ref[...] = _rms(x + gate * proj, gf_ref[...])


def _ple(x, p, g, w_gate, w_proj, gf):
    t, d = x.shape
    tm = PLE_TM
    return pl.pallas_call(
        _ple_kernel,
        grid=(t // tm,),
        in_specs=[
            pl.BlockSpec((tm, d), lambda i: (i, 0)),
            pl.BlockSpec((tm, PLE_DIM), lambda i: (i, 0)),
            pl.BlockSpec((1, d), lambda i: (0, 0)),
            pl.BlockSpec((d, d), lambda i: (0, 0), pipeline_mode=pl.Buffered(1)),
            pl.BlockSpec((PLE_DIM, d), lambda i: (0, 0), pipeline_mode=pl.Buffered(1)),
            pl.BlockSpec((1, d), lambda i: (0, 0)),
        ],
        out_specs=pl.BlockSpec((tm, d), lambda i: (i, 0)),
        out_shape=jax.ShapeDtypeStruct((t, d), F32),
        compiler_params=_params(("parallel",)),
        name="ple",
    )(x, p, g, w_gate, w_proj, gf)


def kernel(x, p, ffn1_norm, ffn1_wg, ffn1_wu, ffn1_wd, mix_norm, w_in, rpb, conv_w, conv_b,
           attn_out_norm, conv_out_norm, w_out, ffn2_norm, ffn2_wg, ffn2_wu, ffn2_wd,
           ple_norm, ple_w_gate, ple_w_proj, final_norm):
    b, s, d = x.shape
    assert ffn1_wg.shape[0] == 1
    assert d == D_MODEL and s % (Q_ROWS * GRID_W) == 0 and s % PROJ_TM == 0
    t = b * s
    h = x.reshape(t, d)
    row = lambda v: v.reshape(1, -1)
    for i in range(1):
        h = _ffn(h, row(ffn1_norm[i]), ffn1_wg[i], ffn1_wu[i], ffn1_wd[i])
        qkv, y_conv = _inproj(h, row(mix_norm[i]), w_in[i], conv_w[i], row(conv_b[i]),
                              row(conv_out_norm[i]), s)
        y_attn = _attn(qkv, _pair_blocks(rpb[i]), row(attn_out_norm[i]), b, s)
        h = _outproj(h, y_attn, y_conv, w_out[i])
        h = _ffn(h, row(ffn2_norm[i]), ffn2_wg[i], ffn2_wu[i], ffn2_wd[i])
        h = _ple(h, p[i].reshape(t, PLE_DIM), row(ple_norm[i]), ple_w_gate[i], ple_w_proj[i],
                 row(final_norm))
    return h.reshape(b, s, d)
```

```python
import functools

import numpy as np
import jax
import jax.numpy as jnp
from jax import lax
from jax.experimental import pallas as pl
from jax.experimental.pallas import tpu as pltpu

D_MODEL = 2048
GRID_W = 64
PLE_DIM = 256
D_ATTN = D_MODEL // 2
D_CONV = D_MODEL - D_ATTN
N_HEADS = 8
HEAD_DIM = D_ATTN // N_HEADS
GROUP_DIM = 128
CONV_W = 3
WIN_ROWS = 8
WIN_COLS = 16
RMS_EPS = 1e-6
NEG = -1e30
LOG2E = 1.4426950408889634
Q_SCALE = HEAD_DIM ** -0.5 * LOG2E

BF16 = jnp.bfloat16
F32 = jnp.float32

VMEM_LIMIT = 56 * 1024 * 1024
BIG_VMEM_LIMIT = 60 * 1024 * 1024
FFN_TM = 1024
FFN_TF = 512
PROJ_TM = 1024
PROJ_TN = 1024
CONV_TC = 256
CONV_SUBTILES = 2
HALO = 16
Q_ROWS = 8
K_ROWS = 16
K_CHUNK_ROWS = 4
OUT_TM = 512
PLE_TM = 512


def _rms(x, g):
    ms = jnp.mean(x * x, axis=-1, keepdims=True)
    return x * lax.rsqrt(ms + RMS_EPS) * g


def _params(sem, vmem_limit=VMEM_LIMIT):
    return pltpu.CompilerParams(dimension_semantics=sem, vmem_limit_bytes=vmem_limit)


def _ffn_kernel(x_hbm, g_ref, wg_ref, wu_ref, wd_ref, o_ref, xbuf, xn_ref, sem):
    i = pl.program_id(0)
    j = pl.program_id(1)
    tm = xbuf.shape[0]

    def x_copy(tile):
        rows = pl.ds(pl.multiple_of(tile * tm, tm), tm)
        return pltpu.make_async_copy(x_hbm.at[rows, :], xbuf, sem)

    def half_swiglu(xn):
        gate = jnp.dot(xn, wg_ref[...].astype(BF16), preferred_element_type=F32)
        up = jnp.dot(xn, wu_ref[...].astype(BF16), preferred_element_type=F32)
        act = (jax.nn.silu(gate) * up * 0.5).astype(BF16)
        return jnp.dot(act, wd_ref[...].astype(BF16), preferred_element_type=F32)

    @pl.when(j == 0)
    def _():
        @pl.when(i == 0)
        def _():
            x_copy(0).start()

        x_copy(i).wait()
        x = xbuf[...]
        xn = _rms(x, g_ref[...]).astype(BF16)
        xn_ref[...] = xn
        o_ref[...] = x + half_swiglu(xn)

    @pl.when((j == 1) & (i + 1 < pl.num_programs(0)))
    def _():
        x_copy(i + 1).start()

    @pl.when(j > 0)
    def _():
        o_ref[...] += half_swiglu(xn_ref[...])


def _ffn(x, g, wg, wu, wd):
    t, d = x.shape
    f = wg.shape[1]
    return pl.pallas_call(
        _ffn_kernel,
        grid=(t // FFN_TM, f // FFN_TF),
        in_specs=[
            pl.BlockSpec(memory_space=pl.ANY),
            pl.BlockSpec((1, d), lambda i, j: (0, 0)),
            pl.BlockSpec((d, FFN_TF), lambda i, j: (0, j)),
            pl.BlockSpec((d, FFN_TF), lambda i, j: (0, j)),
            pl.BlockSpec((FFN_TF, d), lambda i, j: (j, 0)),
        ],
        out_specs=pl.BlockSpec((FFN_TM, d), lambda i, j: (i, 0)),
        out_shape=jax.ShapeDtypeStruct((t, d), F32),
        scratch_shapes=[pltpu.VMEM((FFN_TM, d), F32),
                        pltpu.VMEM((FFN_TM, d), BF16),
                        pltpu.SemaphoreType.DMA(())],
        compiler_params=_params(("arbitrary", "arbitrary"), BIG_VMEM_LIMIT),
        name="ffn",
    )(x, g, wg, wu, wd)


N_QKV_STEPS = 3 * D_ATTN // PROJ_TN
N_CONV_STEPS = D_CONV // CONV_TC


def _inproj_kernel(xa_ref, xb_ref, xp_ref, xnx_ref, g_ref, wq_ref, wb_ref, wc_ref, wu_ref,
                   cw_ref, cb_ref, gn_ref, qkv_ref, conv_ref, xn_ref, cu_ref, *, seq):
    tm = xa_ref.shape[0] + xb_ref.shape[0]
    j = pl.program_id(1)

    def qkv_tile(xn):
        acc = jnp.dot(xn, wq_ref[...].astype(BF16), preferred_element_type=F32)
        col_scale = jnp.where(j < D_ATTN // PROJ_TN, Q_SCALE, 1.0)
        qkv_ref[...] = (acc * col_scale).astype(BF16)

    @pl.when(j == 0)
    def _():
        g = g_ref[...]
        xn_ref[0:HALO, :] = _rms(xp_ref[...], g).astype(BF16)
        xn_ref[HALO + tm:, :] = _rms(xnx_ref[...], g).astype(BF16)
        xn = jnp.concatenate([_rms(xa_ref[...], g), _rms(xb_ref[...], g)], axis=0).astype(BF16)
        xn_ref[HALO:HALO + tm, :] = xn
        qkv_tile(xn)

    @pl.when((j > 0) & (j < N_QKV_STEPS))
    def _():
        qkv_tile(xn_ref[HALO:HALO + tm, :])

    @pl.when(j >= N_QKV_STEPS)
    def _():
        wb = wb_ref[...].astype(BF16)
        wc = wc_ref[...].astype(BF16)
        wu = wu_ref[...].astype(BF16)
        cw = cw_ref[...]
        cb = cb_ref[...]
        gn = gn_ref[...]
        ts = tm // CONV_SUBTILES
        for sub in range(CONV_SUBTILES):
            r0 = sub * ts
            xs = xn_ref[r0:r0 + ts + 2 * HALO, :]
            gate_b = jnp.dot(xn_ref[HALO + r0:HALO + r0 + ts, :], wb, preferred_element_type=F32)
            gate_c = jnp.dot(xs, wc, preferred_element_type=F32)
            u = jnp.dot(xs, wu, preferred_element_type=F32)
            cu_ref[sub] = gate_c * u

            pos = (lax.broadcasted_iota(jnp.int32, (ts, 1), 0) + pl.program_id(0) * tm + r0) % seq
            prev = jnp.where(pos == 0, 0.0, cu_ref[sub, HALO - 1:HALO - 1 + ts, :])
            cur = cu_ref[sub, HALO:HALO + ts, :]
            nxt = jnp.where(pos == seq - 1, 0.0, cu_ref[sub, HALO + 1:HALO + 1 + ts, :])
            y = prev * cw[0:1, :]
            y = y + cur * cw[1:2, :]
            y = y + nxt * cw[2:3, :]
            y = gate_b * (y + cb)
            for c in range(0, y.shape[1], GROUP_DIM):
                sl = slice(c, c + GROUP_DIM)
                conv_ref[r0:r0 + ts, sl] = _rms(y[:, sl], gn[:, sl]).astype(BF16)


def _inproj(x, g, w_in, conv_w, conv_b, gn, seq):
    t, d = x.shape
    tm, tn, tc = PROJ_TM, PROJ_TN, CONV_TC
    hb = tm // HALO
    nhb = t // HALO
    col_b = 3 * D_ATTN // tc
    col_c = col_b + D_CONV // tc
    col_u = col_c + D_CONV // tc
    qstep = lambda j: jnp.minimum(j, N_QKV_STEPS - 1)
    cstep = lambda j: jnp.maximum(j - N_QKV_STEPS, 0)
    n_tiles = t // tm
    xtile = lambda i, j, after: jnp.minimum(i + jnp.where(j > after, 1, 0), n_tiles - 1)
    qwstep = lambda j: jnp.where(j < N_QKV_STEPS, j, 0)
    return pl.pallas_call(
        functools.partial(_inproj_kernel, seq=seq),
        grid=(n_tiles, N_QKV_STEPS + N_CONV_STEPS),
        in_specs=[
            pl.BlockSpec((tm // 2, d), lambda i, j: (2 * xtile(i, j, N_QKV_STEPS), 0)),
            pl.BlockSpec((tm // 2, d), lambda i, j: (2 * xtile(i, j, N_QKV_STEPS + 1) + 1, 0)),
            pl.BlockSpec((HALO, d), lambda i, j: (jnp.maximum(xtile(i, j, 0) * hb - 1, 0), 0)),
            pl.BlockSpec((HALO, d),
                         lambda i, j: (jnp.minimum((xtile(i, j, 0) + 1) * hb, nhb - 1), 0)),
            pl.BlockSpec((1, d), lambda i, j: (0, 0)),
            pl.BlockSpec((d, tn), lambda i, j: (0, qwstep(j))),
            pl.BlockSpec((d, tc), lambda i, j: (0, col_b + cstep(j))),
            pl.BlockSpec((d, tc), lambda i, j: (0, col_c + cstep(j))),
            pl.BlockSpec((d, tc), lambda i, j: (0, col_u + cstep(j))),
            pl.BlockSpec((CONV_W, tc), lambda i, j: (0, cstep(j))),
            pl.BlockSpec((1, tc), lambda i, j: (0, cstep(j))),
            pl.BlockSpec((1, tc), lambda i, j: (0, cstep(j))),
        ],
        out_specs=[pl.BlockSpec((tm, tn), lambda i, j: (i, qstep(j))),
                   pl.BlockSpec((tm, tc), lambda i, j: (i, cstep(j)))],
        out_shape=[jax.ShapeDtypeStruct((t, 3 * D_ATTN), BF16),
                   jax.ShapeDtypeStruct((t, D_CONV), BF16)],
        scratch_shapes=[pltpu.VMEM((tm + 2 * HALO, d), BF16),
                        pltpu.VMEM((CONV_SUBTILES, tm // CONV_SUBTILES + 2 * HALO, tc), F32)],
        compiler_params=_params(("parallel", "arbitrary"), BIG_VMEM_LIMIT),
        name="inproj",
    )(x, x, x, x, g, w_in, w_in, w_in, w_in, conv_w, conv_b, gn)


N_REL = 2 * WIN_ROWS - 1
PAIR_FULL = 0
PAIR_LEFT_NEG = N_REL - 1
PAIR_RIGHT_NEG = 2 * N_REL - 1
PAIR_NEG = 3 * N_REL - 1
N_ROW_PATTERNS = 3
N_PAIRS = K_ROWS // 2


def _runs(ids, starts):
    out = []
    for v in ids:
        if out and out[-1][1] == v and v not in starts:
            out[-1][1] = v + 1
        else:
            out.append([v, v + 1])
    return out


def _pair_blocks(rpb, ids):
    nh, nr, nc = rpb.shape
    w = GRID_W
    c = np.arange(w)
    col_start = np.clip(c - WIN_COLS // 2, 0, w - WIN_COLS)
    in_win = (c[None, :] >= col_start[:, None]) & (c[None, :] < col_start[:, None] + WIN_COLS)
    period = 2 * w
    left = w - WIN_COLS
    u = jnp.pad(rpb, ((0, 0), (0, 0), (left, period - nc - left)))
    toep = jnp.tile(u, (1, 1, w))[..., :w * (period - 1)].reshape(nh, nr, w, period - 1)[..., w - 1:]
    toep = jnp.where(in_win[None, None], toep * LOG2E, NEG)
    parts = []
    for first, last in _runs(ids, (PAIR_LEFT_NEG, PAIR_RIGHT_NEG, PAIR_NEG)):
        n = last - first
        neg = jnp.full((nh, n, w, w), NEG, F32)
        if last <= PAIR_LEFT_NEG:
            halves = [toep[:, first:last], toep[:, first + 1:last + 1]]
        elif PAIR_LEFT_NEG <= first and last <= PAIR_RIGHT_NEG:
            halves = [neg, toep[:, first - PAIR_LEFT_NEG:last - PAIR_LEFT_NEG]]
        elif PAIR_RIGHT_NEG <= first and last <= PAIR_NEG:
            halves = [toep[:, first - PAIR_RIGHT_NEG:last - PAIR_RIGHT_NEG], neg]
        else:
            assert (first, last) == (PAIR_NEG, PAIR_NEG + 1)
            halves = [neg, neg]
        parts.append(jnp.concatenate(halves, axis=-1))
    return jnp.concatenate(parts, axis=1)


def _pair_index_table(rows):
    tab = np.zeros((N_ROW_PATTERNS, Q_ROWS, N_PAIRS), np.int32)
    for pat, r0 in enumerate((0, Q_ROWS, rows - Q_ROWS)):
        k0 = min(max(r0 - WIN_ROWS // 2, 0), rows - K_ROWS)
        for i in range(Q_ROWS):
            r = r0 + i
            row_start = min(max(r - WIN_ROWS // 2, 0), rows - WIN_ROWS)
            for m in range(N_PAIRS):
                kr = k0 + 2 * m
                in0 = row_start <= kr < row_start + WIN_ROWS
                in1 = row_start <= kr + 1 < row_start + WIN_ROWS
                a = kr - r + WIN_ROWS - 1
                if in0 and in1:
                    tab[pat, i, m] = PAIR_FULL + a
                elif in1:
                    tab[pat, i, m] = PAIR_LEFT_NEG + a + 1
                elif in0:
                    tab[pat, i, m] = PAIR_RIGHT_NEG + a
                else:
                    tab[pat, i, m] = PAIR_NEG
    return tab


def _chunk_spans(tab):
    pairs_per_chunk = K_CHUNK_ROWS // 2
    spans = []
    for pat in range(tab.shape[0]):
        row = []
        for c in range(K_ROWS // K_CHUNK_ROWS):
            blk = tab[pat, :, c * pairs_per_chunk:(c + 1) * pairs_per_chunk]
            hit = np.nonzero((blk != PAIR_NEG).any(axis=1))[0]
            if hit.size == 0:
                row.append((0, 0))
            else:
                assert hit[-1] - hit[0] + 1 == hit.size
                row.append((int(hit[0]), int(hit[-1]) + 1))
        spans.append(row)
    return spans


def _attn_head(h, tab, spans, q_ref, ks, vs, pb_ref, gn_ref, o_ref):
    hs = slice(h * HEAD_DIM, (h + 1) * HEAD_DIM)
    pairs_per_chunk = K_CHUNK_ROWS // 2
    live = [c for c, (first, last) in enumerate(spans) if last > first]
    w = GRID_W
    scores = {}
    m = [None] * Q_ROWS
    for c in live:
        first, last = spans[c]
        s = lax.dot_general(q_ref[first * w:last * w, hs], ks[c][:, hs],
                            (((1,), (1,)), ((), ())), preferred_element_type=F32)
        for i in range(first, last):
            bias = jnp.concatenate([pb_ref[h, int(tab[i, c * pairs_per_chunk + k])]
                                    for k in range(pairs_per_chunk)], axis=1)
            si = s[(i - first) * w:(i - first + 1) * w] + bias
            scores[c, i] = si
            mi = jnp.max(si, axis=1, keepdims=True)
            m[i] = mi if m[i] is None else jnp.maximum(m[i], mi)
    l = [None] * Q_ROWS
    o = [None] * Q_ROWS
    for c in live:
        first, last = spans[c]
        ps = []
        for i in range(first, last):
            p = jnp.exp2(scores[c, i] - m[i])
            li = jnp.sum(p, axis=1, keepdims=True)
            l[i] = li if l[i] is None else l[i] + li
            ps.append(p.astype(BF16))
        oc = jnp.dot(jnp.concatenate(ps, axis=0), vs[c][:, hs], preferred_element_type=F32)
        for i in range(first, last):
            oi = oc[(i - first) * w:(i - first + 1) * w]
            o[i] = oi if o[i] is None else o[i] + oi
    gn = gn_ref[:, hs]
    for i in range(Q_ROWS):
        o_ref[i * w:(i + 1) * w, hs] = _rms(o[i] / l[i], gn).astype(BF16)


def _attn_kernel(q_ref, k0, k1, k2, k3, v0, v1, v2, v3, pb_ref, gn_ref, o_ref, *, tab, spans):
    rb = pl.program_id(1)
    pat = jnp.where(rb == 0, 0, jnp.where(rb == pl.num_programs(1) - 1, 2, 1))
    for pat_id in range(N_ROW_PATTERNS):
        @pl.when(pat == pat_id)
        def _(pat_id=pat_id):
            for h in range(N_HEADS):
                _attn_head(h, tab[pat_id], spans[pat_id], q_ref, (k0, k1, k2, k3),
                           (v0, v1, v2, v3), pb_ref, gn_ref, o_ref)


def _attn(qkv, rpb, gn, batch, seq):
    rows = seq // GRID_W
    tq = Q_ROWS * GRID_W
    kc = K_CHUNK_ROWS * GRID_W
    n_rb = rows // Q_ROWS
    n_kc = K_ROWS // K_CHUNK_ROWS
    chunks_per_seq = seq // kc
    tab = _pair_index_table(rows)
    spans = _chunk_spans(tab)
    ids = [int(v) for v in np.unique(tab)]
    pair_blocks = _pair_blocks(rpb, ids)
    tab = np.searchsorted(ids, tab)

    def q_map(b, rb):
        return (b * n_rb + rb, 0)

    def kv_map(col, c):
        def f(b, rb):
            first = jnp.clip(rb * (Q_ROWS // K_CHUNK_ROWS) - (WIN_ROWS // 2) // K_CHUNK_ROWS,
                             0, chunks_per_seq - n_kc)
            return (b * chunks_per_seq + first + c, col)
        return f

    k_specs = [pl.BlockSpec((kc, D_ATTN), kv_map(1, c)) for c in range(n_kc)]
    v_specs = [pl.BlockSpec((kc, D_ATTN), kv_map(2, c)) for c in range(n_kc)]
    return pl.pallas_call(
        functools.partial(_attn_kernel, tab=tab, spans=spans),
        grid=(batch, n_rb),
        in_specs=[pl.BlockSpec((tq, D_ATTN), q_map)] + k_specs + v_specs + [
            pl.BlockSpec(pair_blocks.shape, lambda b, rb: (0, 0, 0, 0),
                         pipeline_mode=pl.Buffered(1)),
            pl.BlockSpec((1, D_ATTN), lambda b, rb: (0, 0)),
        ],
        out_specs=pl.BlockSpec((tq, D_ATTN), q_map),
        out_shape=jax.ShapeDtypeStruct((batch * seq, D_ATTN), BF16),
        compiler_params=_params(("parallel", "arbitrary")),
        name="attn",
    )(qkv, *([qkv] * (2 * n_kc)), pair_blocks, gn)


def _outproj_kernel(x_ref, a_ref, c_ref, wa_ref, wc_ref, o_ref):
    acc = jnp.dot(a_ref[...], wa_ref[...].astype(BF16), preferred_element_type=F32)
    acc = acc + jnp.dot(c_ref[...], wc_ref[...].astype(BF16), preferred_element_type=F32)
    o_ref[...] = x_ref[...] + acc


def _outproj(x, a, c, w_out):
    t, d = x.shape
    tm = OUT_TM
    return pl.pallas_call(
        _outproj_kernel,
        grid=(t // tm,),
        in_specs=[
            pl.BlockSpec((tm, d), lambda i: (i, 0)),
            pl.BlockSpec((tm, D_ATTN), lambda i: (i, 0)),
            pl.BlockSpec((tm, D_CONV), lambda i: (i, 0)),
            pl.BlockSpec((D_ATTN, d), lambda i: (0, 0), pipeline_mode=pl.Buffered(1)),
            pl.BlockSpec((D_CONV, d), lambda i: (1, 0), pipeline_mode=pl.Buffered(1)),
        ],
        out_specs=pl.BlockSpec((tm, d), lambda i: (i, 0)),
        out_shape=jax.ShapeDtypeStruct((t, d), F32),
        compiler_params=_params(("parallel",)),
        name="outproj",
    )(x, a, c, w_out, w_out)


def _ple_kernel(x_ref, p_ref, g_ref, wg_ref, wp_ref, gf_ref, o_ref):
    x = x_ref[...]
    xn = _rms(x, g_ref[...]).astype(BF16)
    gate = jax.nn.sigmoid(jnp.dot(xn, wg_ref[...].astype(BF16), preferred_element_type=F32))
    proj = jnp.dot(p_ref[...].astype(BF16), wp_ref[...].astype(BF16),
                   preferred_element_type=F32)
    o_ref[...] = _rms(x + gate * proj, gf_ref[...])


def _ple(x, p, g, w_gate, w_proj, gf):
    t, d = x.shape
    tm = PLE_TM
    return pl.pallas_call(
        _ple_kernel,
        grid=(t // tm,),
        in_specs=[
            pl.BlockSpec((tm, d), lambda i: (i, 0)),
            pl.BlockSpec((tm, PLE_DIM), lambda i: (i, 0)),
            pl.BlockSpec((1, d), lambda i: (0, 0)),
            pl.BlockSpec((d, d), lambda i: (0, 0), pipeline_mode=pl.Buffered(1)),
            pl.BlockSpec((PLE_DIM, d), lambda i: (0, 0), pipeline_mode=pl.Buffered(1)),
            pl.BlockSpec((1, d), lambda i: (0, 0)),
        ],
        out_specs=pl.BlockSpec((tm, d), lambda i: (i, 0)),
        out_shape=jax.ShapeDtypeStruct((t, d), F32),
        compiler_params=_params(("parallel",)),
        name="ple",
    )(x, p, g, w_gate, w_proj, gf)


def kernel(x, p, ffn1_norm, ffn1_wg, ffn1_wu, ffn1_wd, mix_norm, w_in, rpb, conv_w, conv_b,
           attn_out_norm, conv_out_norm, w_out, ffn2_norm, ffn2_wg, ffn2_wu, ffn2_wd,
           ple_norm, ple_w_gate, ple_w_proj, final_norm):
    b, s, d = x.shape
    assert ffn1_wg.shape[0] == 1
    assert d == D_MODEL and s % (Q_ROWS * GRID_W) == 0 and s % PROJ_TM == 0
    t = b * s
    h = x.reshape(t, d)
    row = lambda v: v.reshape(1, -1)
    for i in range(1):
        h = _ffn(h, row(ffn1_norm[i]), ffn1_wg[i], ffn1_wu[i], ffn1_wd[i])
        qkv, y_conv = _inproj(h, row(mix_norm[i]), w_in[i], conv_w[i], row(conv_b[i]),
                              row(conv_out_norm[i]), s)
        y_attn = _attn(qkv, rpb[i], row(attn_out_norm[i]), b, s)
        h = _outproj(h, y_attn, y_conv, w_out[i])
        h = _ffn(h, row(ffn2_norm[i]), ffn2_wg[i], ffn2_wu[i], ffn2_wd[i])
        h = _ple(h, p[i].reshape(t, PLE_DIM), row(ple_norm[i]), ple_w_gate[i], ple_w_proj[i],
                 row(final_norm))
    return h.reshape(b, s, d)
```

```python
import functools

import numpy as np
import jax
import jax.numpy as jnp
from jax import lax
from jax.experimental import pallas as pl
from jax.experimental.pallas import tpu as pltpu

D_MODEL = 2048
GRID_W = 64
PLE_DIM = 256
D_ATTN = D_MODEL // 2
D_CONV = D_MODEL - D_ATTN
N_HEADS = 8
HEAD_DIM = D_ATTN // N_HEADS
GROUP_DIM = 128
CONV_W = 3
WIN_ROWS = 8
WIN_COLS = 16
RMS_EPS = 1e-6
NEG = -1e30
LOG2E = 1.4426950408889634
Q_SCALE = HEAD_DIM ** -0.5 * LOG2E

BF16 = jnp.bfloat16
F32 = jnp.float32

VMEM_LIMIT = 56 * 1024 * 1024
BIG_VMEM_LIMIT = 60 * 1024 * 1024
FFN_TM = 1024
FFN_TF = 512
PROJ_TM = 1024
PROJ_TN = 1024
CONV_TC = 256
CONV_SUBTILES = 2
HALO = 16
Q_ROWS = 8
K_ROWS = 16
K_CHUNK_ROWS = 4
OUT_TM = 512
PLE_TM = 512


def _rms(x, g):
    ms = jnp.mean(x * x, axis=-1, keepdims=True)
    return x * lax.rsqrt(ms + RMS_EPS) * g


def _params(sem, vmem_limit=VMEM_LIMIT):
    return pltpu.CompilerParams(dimension_semantics=sem, vmem_limit_bytes=vmem_limit)


def _ffn_kernel(x_hbm, g_ref, wg_ref, wu_ref, wd_ref, o_ref, xbuf, xn_ref, sem):
    i = pl.program_id(0)
    j = pl.program_id(1)
    tm = xbuf.shape[0]

    def x_copy(tile):
        rows = pl.ds(pl.multiple_of(tile * tm, tm), tm)
        return pltpu.make_async_copy(x_hbm.at[rows, :], xbuf, sem)

    def half_swiglu(xn):
        gate = jnp.dot(xn, wg_ref[...].astype(BF16), preferred_element_type=F32)
        up = jnp.dot(xn, wu_ref[...].astype(BF16), preferred_element_type=F32)
        act = (jax.nn.silu(gate) * up * 0.5).astype(BF16)
        return jnp.dot(act, wd_ref[...].astype(BF16), preferred_element_type=F32)

    @pl.when(j == 0)
    def _():
        @pl.when(i == 0)
        def _():
            x_copy(0).start()

        x_copy(i).wait()
        x = xbuf[...]
        xn = _rms(x, g_ref[...]).astype(BF16)
        xn_ref[...] = xn
        o_ref[...] = x + half_swiglu(xn)

    @pl.when((j == 1) & (i + 1 < pl.num_programs(0)))
    def _():
        x_copy(i + 1).start()

    @pl.when(j > 0)
    def _():
        o_ref[...] += half_swiglu(xn_ref[...])


def _ffn(x, g, wg, wu, wd):
    t, d = x.shape
    f = wg.shape[1]
    return pl.pallas_call(
        _ffn_kernel,
        grid=(t // FFN_TM, f // FFN_TF),
        in_specs=[
            pl.BlockSpec(memory_space=pl.ANY),
            pl.BlockSpec((1, d), lambda i, j: (0, 0)),
            pl.BlockSpec((d, FFN_TF), lambda i, j: (0, j)),
            pl.BlockSpec((d, FFN_TF), lambda i, j: (0, j)),
            pl.BlockSpec((FFN_TF, d), lambda i, j: (j, 0)),
        ],
        out_specs=pl.BlockSpec((FFN_TM, d), lambda i, j: (i, 0)),
        out_shape=jax.ShapeDtypeStruct((t, d), F32),
        scratch_shapes=[pltpu.VMEM((FFN_TM, d), F32),
                        pltpu.VMEM((FFN_TM, d), BF16),
                        pltpu.SemaphoreType.DMA(())],
        compiler_params=_params(("arbitrary", "arbitrary"), BIG_VMEM_LIMIT),
        name="ffn",
    )(x, g, wg, wu, wd)


N_QKV_STEPS = 3 * D_ATTN // PROJ_TN
N_CONV_STEPS = D_CONV // CONV_TC


def _inproj_kernel(xa_ref, xb_ref, xp_ref, xnx_ref, g_ref, wq_ref, wb_ref, wc_ref, wu_ref,
                   cw_ref, cb_ref, gn_ref, qkv_ref, conv_ref, xn_ref, cu_ref, *, seq):
    tm = xa_ref.shape[0] + xb_ref.shape[0]
    j = pl.program_id(1)

    def qkv_tile(xn):
        acc = jnp.dot(xn, wq_ref[...].astype(BF16), preferred_element_type=F32)
        col_scale = jnp.where(j < D_ATTN // PROJ_TN, Q_SCALE, 1.0)
        qkv_ref[...] = (acc * col_scale).astype(BF16)

    @pl.when(j == 0)
    def _():
        g = g_ref[...]
        xn_ref[0:HALO, :] = _rms(xp_ref[...], g).astype(BF16)
        xn_ref[HALO + tm:, :] = _rms(xnx_ref[...], g).astype(BF16)
        xn = jnp.concatenate([_rms(xa_ref[...], g), _rms(xb_ref[...], g)], axis=0).astype(BF16)
        xn_ref[HALO:HALO + tm, :] = xn
        qkv_tile(xn)

    @pl.when((j > 0) & (j < N_QKV_STEPS))
    def _():
        qkv_tile(xn_ref[HALO:HALO + tm, :])

    @pl.when(j >= N_QKV_STEPS)
    def _():
        wb = wb_ref[...].astype(BF16)
        wc = wc_ref[...].astype(BF16)
        wu = wu_ref[...].astype(BF16)
        cw = cw_ref[...]
        cb = cb_ref[...]
        gn = gn_ref[...]
        ts = tm // CONV_SUBTILES
        for sub in range(CONV_SUBTILES):
            r0 = sub * ts
            xs = xn_ref[r0:r0 + ts + 2 * HALO, :]
            gate_b = jnp.dot(xn_ref[HALO + r0:HALO + r0 + ts, :], wb, preferred_element_type=F32)
            gate_c = jnp.dot(xs, wc, preferred_element_type=F32)
            u = jnp.dot(xs, wu, preferred_element_type=F32)
            cu_ref[sub] = gate_c * u

            pos = (lax.broadcasted_iota(jnp.int32, (ts, 1), 0) + pl.program_id(0) * tm + r0) % seq
            prev = jnp.where(pos == 0, 0.0, cu_ref[sub, HALO - 1:HALO - 1 + ts, :])
            cur = cu_ref[sub, HALO:HALO + ts, :]
            nxt = jnp.where(pos == seq - 1, 0.0, cu_ref[sub, HALO + 1:HALO + 1 + ts, :])
            y = prev * cw[0:1, :]
            y = y + cur * cw[1:2, :]
            y = y + nxt * cw[2:3, :]
            y = gate_b * (y + cb)
            for c in range(0, y.shape[1], GROUP_DIM):
                sl = slice(c, c + GROUP_DIM)
                conv_ref[r0:r0 + ts, sl] = _rms(y[:, sl], gn[:, sl]).astype(BF16)


def _inproj(x, g, w_in, conv_w, conv_b, gn, seq):
    t, d = x.shape
    tm, tn, tc = PROJ_TM, PROJ_TN, CONV_TC
    hb = tm // HALO
    nhb = t // HALO
    col_b = 3 * D_ATTN // tc
    col_c = col_b + D_CONV // tc
    col_u = col_c + D_CONV // tc
    qstep = lambda j: jnp.minimum(j, N_QKV_STEPS - 1)
    cstep = lambda j: jnp.maximum(j - N_QKV_STEPS, 0)
    n_tiles = t // tm
    xtile = lambda i, j, after: jnp.minimum(i + jnp.where(j > after, 1, 0), n_tiles - 1)
    qwstep = lambda j: jnp.where(j < N_QKV_STEPS, j, 0)
    return pl.pallas_call(
        functools.partial(_inproj_kernel, seq=seq),
        grid=(n_tiles, N_QKV_STEPS + N_CONV_STEPS),
        in_specs=[
            pl.BlockSpec((tm // 2, d), lambda i, j: (2 * xtile(i, j, N_QKV_STEPS), 0)),
            pl.BlockSpec((tm // 2, d), lambda i, j: (2 * xtile(i, j, N_QKV_STEPS + 1) + 1, 0)),
            pl.BlockSpec((HALO, d), lambda i, j: (jnp.maximum(xtile(i, j, 0) * hb - 1, 0), 0)),
            pl.BlockSpec((HALO, d),
                         lambda i, j: (jnp.minimum((xtile(i, j, 0) + 1) * hb, nhb - 1), 0)),
            pl.BlockSpec((1, d), lambda i, j: (0, 0)),
            pl.BlockSpec((d, tn), lambda i, j: (0, qwstep(j))),
            pl.BlockSpec((d, tc), lambda i, j: (0, col_b + cstep(j))),
            pl.BlockSpec((d, tc), lambda i, j: (0, col_c + cstep(j))),
            pl.BlockSpec((d, tc), lambda i, j: (0, col_u + cstep(j))),
            pl.BlockSpec((CONV_W, tc), lambda i, j: (0, cstep(j))),
            pl.BlockSpec((1, tc), lambda i, j: (0, cstep(j))),
            pl.BlockSpec((1, tc), lambda i, j: (0, cstep(j))),
        ],
        out_specs=[pl.BlockSpec((tm, tn), lambda i, j: (i, qstep(j))),
                   pl.BlockSpec((tm, tc), lambda i, j: (i, cstep(j)))],
        out_shape=[jax.ShapeDtypeStruct((t, 3 * D_ATTN), BF16),
                   jax.ShapeDtypeStruct((t, D_CONV), BF16)],
        scratch_shapes=[pltpu.VMEM((tm + 2 * HALO, d), BF16),
                        pltpu.VMEM((CONV_SUBTILES, tm // CONV_SUBTILES + 2 * HALO, tc), F32)],
        compiler_params=_params(("parallel", "arbitrary"), BIG_VMEM_LIMIT),
        name="inproj",
    )(x, x, x, x, g, w_in, w_in, w_in, w_in, conv_w, conv_b, gn)


N_REL = 2 * WIN_ROWS - 1
PAIR_FULL = 0
PAIR_LEFT_NEG = N_REL - 1
PAIR_RIGHT_NEG = 2 * N_REL - 1
PAIR_NEG = 3 * N_REL - 1
N_ROW_PATTERNS = 3
N_PAIRS = K_ROWS // 2


def _pair_halves(block_id):
    if block_id < PAIR_LEFT_NEG:
        return block_id, block_id + 1
    if block_id < PAIR_RIGHT_NEG:
        return None, block_id - PAIR_LEFT_NEG
    if block_id < PAIR_NEG:
        return block_id - PAIR_RIGHT_NEG, None
    return None, None


def _pair_seeds(rpb, ids):
    nh, nr, nc = rpb.shape
    w = GRID_W
    scaled = jnp.concatenate([rpb * LOG2E, jnp.zeros((nh, 1, nc), F32)], axis=1)
    halves = [_pair_halves(b) for b in ids]
    left = jnp.stack([scaled[:, nr if l is None else l] for l, _ in halves], axis=1)
    right = jnp.stack([scaled[:, nr if r is None else r] for _, r in halves], axis=1)
    gap = jnp.zeros((nh, len(ids), w - 2 * WIN_COLS + 1), F32)
    seeds = jnp.concatenate(
        [left[..., WIN_COLS - 1:], gap, right, gap, left[..., :WIN_COLS - 1]], axis=-1)
    return seeds.reshape(nh * len(ids), 2 * w)


def _build_pair_blocks(seed_ref, pb_ref, ids):
    w = GRID_W
    q = lax.broadcasted_iota(jnp.int32, (w, 2 * w), 0)
    lane = lax.broadcasted_iota(jnp.int32, (w, 2 * w), 1)
    k = lane % w
    col_start = jnp.clip(q - WIN_COLS // 2, 0, w - WIN_COLS)
    in_win = (k >= col_start) & (k < col_start + WIN_COLS)
    masks = {
        (True, True): in_win,
        (False, True): in_win & (lane >= w),
        (True, False): in_win & (lane < w),
    }
    for h in range(N_HEADS):
        for n, block_id in enumerate(ids):
            left, right = _pair_halves(block_id)
            if left is None and right is None:
                pb_ref[h, n] = jnp.full((w, 2 * w), NEG, F32)
                continue
            row = h * len(ids) + n
            seed = jnp.broadcast_to(seed_ref[row:row + 1, :], (w, 2 * w))
            rotated = pltpu.roll(seed, 0, 1, stride=1, stride_axis=0)
            pb_ref[h, n] = jnp.where(masks[left is not None, right is not None], rotated, NEG)


def _pair_index_table(rows):
    tab = np.zeros((N_ROW_PATTERNS, Q_ROWS, N_PAIRS), np.int32)
    for pat, r0 in enumerate((0, Q_ROWS, rows - Q_ROWS)):
        k0 = min(max(r0 - WIN_ROWS // 2, 0), rows - K_ROWS)
        for i in range(Q_ROWS):
            r = r0 + i
            row_start = min(max(r - WIN_ROWS // 2, 0), rows - WIN_ROWS)
            for m in range(N_PAIRS):
                kr = k0 + 2 * m
                in0 = row_start <= kr < row_start + WIN_ROWS
                in1 = row_start <= kr + 1 < row_start + WIN_ROWS
                a = kr - r + WIN_ROWS - 1
                if in0 and in1:
                    tab[pat, i, m] = PAIR_FULL + a
                elif in1:
                    tab[pat, i, m] = PAIR_LEFT_NEG + a + 1
                elif in0:
                    tab[pat, i, m] = PAIR_RIGHT_NEG + a
                else:
                    tab[pat, i, m] = PAIR_NEG
    return tab


def _chunk_spans(tab):
    pairs_per_chunk = K_CHUNK_ROWS // 2
    spans = []
    for pat in range(tab.shape[0]):
        row = []
        for c in range(K_ROWS // K_CHUNK_ROWS):
            blk = tab[pat, :, c * pairs_per_chunk:(c + 1) * pairs_per_chunk]
            hit = np.nonzero((blk != PAIR_NEG).any(axis=1))[0]
            if hit.size == 0:
                row.append((0, 0))
            else:
                assert hit[-1] - hit[0] + 1 == hit.size
                row.append((int(hit[0]), int(hit[-1]) + 1))
        spans.append(row)
    return spans


def _attn_head(h, tab, spans, q_ref, ks, vs, pb_ref, gn_ref, o_ref):
    hs = slice(h * HEAD_DIM, (h + 1) * HEAD_DIM)
    pairs_per_chunk = K_CHUNK_ROWS // 2
    live = [c for c, (first, last) in enumerate(spans) if last > first]
    w = GRID_W
    scores = {}
    m = [None] * Q_ROWS
    for c in live:
        first, last = spans[c]
        s = lax.dot_general(q_ref[first * w:last * w, hs], ks[c][:, hs],
                            (((1,), (1,)), ((), ())), preferred_element_type=F32)
        for i in range(first, last):
            bias = jnp.concatenate([pb_ref[h, int(tab[i, c * pairs_per_chunk + k])]
                                    for k in range(pairs_per_chunk)], axis=1)
            si = s[(i - first) * w:(i - first + 1) * w] + bias
            scores[c, i] = si
            mi = jnp.max(si, axis=1, keepdims=True)
            m[i] = mi if m[i] is None else jnp.maximum(m[i], mi)
    l = [None] * Q_ROWS
    o = [None] * Q_ROWS
    for c in live:
        first, last = spans[c]
        ps = []
        for i in range(first, last):
            p = jnp.exp2(scores[c, i] - m[i])
            li = jnp.sum(p, axis=1, keepdims=True)
            l[i] = li if l[i] is None else l[i] + li
            ps.append(p.astype(BF16))
        oc = jnp.dot(jnp.concatenate(ps, axis=0), vs[c][:, hs], preferred_element_type=F32)
        for i in range(first, last):
            oi = oc[(i - first) * w:(i - first + 1) * w]
            o[i] = oi if o[i] is None else o[i] + oi
    gn = gn_ref[:, hs]
    for i in range(Q_ROWS):
        o_ref[i * w:(i + 1) * w, hs] = _rms(o[i] / l[i], gn).astype(BF16)


def _attn_kernel(q_ref, k0, k1, k2, k3, v0, v1, v2, v3, seed_ref, gn_ref, o_ref, pb_ref, *,
                 tab, spans, ids):
    rb = pl.program_id(1)

    @pl.when((pl.program_id(0) == 0) & (rb == 0))
    def _():
        _build_pair_blocks(seed_ref, pb_ref, ids)

    pat = jnp.where(rb == 0, 0, jnp.where(rb == pl.num_programs(1) - 1, 2, 1))
    for pat_id in range(N_ROW_PATTERNS):
        @pl.when(pat == pat_id)
        def _(pat_id=pat_id):
            for h in range(N_HEADS):
                _attn_head(h, tab[pat_id], spans[pat_id], q_ref, (k0, k1, k2, k3),
                           (v0, v1, v2, v3), pb_ref, gn_ref, o_ref)


def _attn(qkv, rpb, gn, batch, seq):
    rows = seq // GRID_W
    tq = Q_ROWS * GRID_W
    kc = K_CHUNK_ROWS * GRID_W
    n_rb = rows // Q_ROWS
    n_kc = K_ROWS // K_CHUNK_ROWS
    chunks_per_seq = seq // kc
    tab = _pair_index_table(rows)
    spans = _chunk_spans(tab)
    ids = [int(v) for v in np.unique(tab)]
    seeds = _pair_seeds(rpb, ids)
    tab = np.searchsorted(ids, tab)

    def q_map(b, rb):
        return (b * n_rb + rb, 0)

    def kv_map(col, c):
        def f(b, rb):
            first = jnp.clip(rb * (Q_ROWS // K_CHUNK_ROWS) - (WIN_ROWS // 2) // K_CHUNK_ROWS,
                             0, chunks_per_seq - n_kc)
            return (b * chunks_per_seq + first + c, col)
        return f

    k_specs = [pl.BlockSpec((kc, D_ATTN), kv_map(1, c)) for c in range(n_kc)]
    v_specs = [pl.BlockSpec((kc, D_ATTN), kv_map(2, c)) for c in range(n_kc)]
    return pl.pallas_call(
        functools.partial(_attn_kernel, tab=tab, spans=spans, ids=ids),
        grid=(batch, n_rb),
        in_specs=[pl.BlockSpec((tq, D_ATTN), q_map)] + k_specs + v_specs + [
            pl.BlockSpec(seeds.shape, lambda b, rb: (0, 0)),
            pl.BlockSpec((1, D_ATTN), lambda b, rb: (0, 0)),
        ],
        out_specs=pl.BlockSpec((tq, D_ATTN), q_map),
        out_shape=jax.ShapeDtypeStruct((batch * seq, D_ATTN), BF16),
        scratch_shapes=[pltpu.VMEM((N_HEADS, len(ids), GRID_W, 2 * GRID_W), F32)],
        compiler_params=_params(("arbitrary", "arbitrary")),
        name="attn",
    )(qkv, *([qkv] * (2 * n_kc)), seeds, gn)


def _outproj_kernel(x_ref, a_ref, c_ref, wa_ref, wc_ref, o_ref):
    acc = jnp.dot(a_ref[...], wa_ref[...].astype(BF16), preferred_element_type=F32)
    acc = acc + jnp.dot(c_ref[...], wc_ref[...].astype(BF16), preferred_element_type=F32)
    o_ref[...] = x_ref[...] + acc


def _outproj(x, a, c, w_out):
    t, d = x.shape
    tm = OUT_TM
    return pl.pallas_call(
        _outproj_kernel,
        grid=(t // tm,),
        in_specs=[
            pl.BlockSpec((tm, d), lambda i: (i, 0)),
            pl.BlockSpec((tm, D_ATTN), lambda i: (i, 0)),
            pl.BlockSpec((tm, D_CONV), lambda i: (i, 0)),
            pl.BlockSpec((D_ATTN, d), lambda i: (0, 0), pipeline_mode=pl.Buffered(1)),
            pl.BlockSpec((D_CONV, d), lambda i: (1, 0), pipeline_mode=pl.Buffered(1)),
        ],
        out_specs=pl.BlockSpec((tm, d), lambda i: (i, 0)),
        out_shape=jax.ShapeDtypeStruct((t, d), F32),
        compiler_params=_params(("parallel",)),
        name="outproj",
    )(x, a, c, w_out, w_out)


def _ple_kernel(x_ref, p_ref, g_ref, wg_ref, wp_ref, gf_ref, o_ref):
    x = x_ref[...]
    xn = _rms(x, g_ref[...]).astype(BF16)
    gate = jax.nn.sigmoid(jnp.dot(xn, wg_ref[...].astype(BF16), preferred_element_type=F32))
    proj = jnp.dot(p_ref[...].astype(BF16), wp_ref[...].astype(BF16),
                   preferred_element_type=F32)
    o_ref[...] = _rms(x + gate * proj, gf_ref[...])


def _ple(x, p, g, w_gate, w_proj, gf):
    t, d = x.shape
    tm = PLE_TM
    return pl.pallas_call(
        _ple_kernel,
        grid=(t // tm,),
        in_specs=[
            pl.BlockSpec((tm, d), lambda i: (i, 0)),
            pl.BlockSpec((tm, PLE_DIM), lambda i: (i, 0)),
            pl.BlockSpec((1, d), lambda i: (0, 0)),
            pl.BlockSpec((d, d), lambda i: (0, 0), pipeline_mode=pl.Buffered(1)),
            pl.BlockSpec((PLE_DIM, d), lambda i: (0, 0), pipeline_mode=pl.Buffered(1)),
            pl.BlockSpec((1, d), lambda i: (0, 0)),
        ],
        out_specs=pl.BlockSpec((tm, d), lambda i: (i, 0)),
        out_shape=jax.ShapeDtypeStruct((t, d), F32),
        compiler_params=_params(("parallel",)),
        name="ple",
    )(x, p, g, w_gate, w_proj, gf)


def kernel(x, p, ffn1_norm, ffn1_wg, ffn1_wu, ffn1_wd, mix_norm, w_in, rpb, conv_w, conv_b,
           attn_out_norm, conv_out_norm, w_out, ffn2_norm, ffn2_wg, ffn2_wu, ffn2_wd,
           ple_norm, ple_w_gate, ple_w_proj, final_norm):
    b, s, d = x.shape
    assert ffn1_wg.shape[0] == 1
    assert d == D_MODEL and s % (Q_ROWS * GRID_W) == 0 and s % PROJ_TM == 0
    t = b * s
    h = x.reshape(t, d)
    row = lambda v: v.reshape(1, -1)
    for i in range(1):
        h = _ffn(h, row(ffn1_norm[i]), ffn1_wg[i], ffn1_wu[i], ffn1_wd[i])
        qkv, y_conv = _inproj(h, row(mix_norm[i]), w_in[i], conv_w[i], row(conv_b[i]),
                              row(conv_out_norm[i]), s)
        y_attn = _attn(qkv, rpb[i], row(attn_out_norm[i]), b, s)
        h = _outproj(h, y_attn, y_conv, w_out[i])
        h = _ffn(h, row(ffn2_norm[i]), ffn2_wg[i], ffn2_wu[i], ffn2_wd[i])
        h = _ple(h, p[i].reshape(t, PLE_DIM), row(ple_norm[i]), ple_w_gate[i], ple_w_proj[i],
                 row(final_norm))
    return h.reshape(b, s, d)
```

```python
import functools

import numpy as np
import jax
import jax.numpy as jnp
from jax import lax
from jax.experimental import pallas as pl
from jax.experimental.pallas import tpu as pltpu

D_MODEL = 2048
GRID_W = 64
PLE_DIM = 256
D_ATTN = D_MODEL // 2
D_CONV = D_MODEL - D_ATTN
N_HEADS = 8
HEAD_DIM = D_ATTN // N_HEADS
GROUP_DIM = 128
CONV_W = 3
WIN_ROWS = 8
WIN_COLS = 16
RMS_EPS = 1e-6
NEG = -1e30
LOG2E = 1.4426950408889634
Q_SCALE = HEAD_DIM ** -0.5 * LOG2E

BF16 = jnp.bfloat16
F32 = jnp.float32

VMEM_LIMIT = 56 * 1024 * 1024
BIG_VMEM_LIMIT = 60 * 1024 * 1024
FFN_TM = 1024
FFN_TF = 512
PROJ_TM = 1024
PROJ_TN = 1024
CONV_TC = 256
CONV_SUBTILES = 2
HALO = 16
Q_ROWS = 8
K_ROWS = 16
K_CHUNK_ROWS = 4
OUT_TM = 512
PLE_TM = 512


def _rms(x, g):
    ms = jnp.mean(x * x, axis=-1, keepdims=True)
    return x * lax.rsqrt(ms + RMS_EPS) * g


def _params(sem, vmem_limit=VMEM_LIMIT):
    return pltpu.CompilerParams(dimension_semantics=sem, vmem_limit_bytes=vmem_limit)


def _ffn_kernel(x_hbm, g_ref, wg_ref, wu_ref, wd_ref, o_ref, xbuf, xn_ref, sem):
    i = pl.program_id(0)
    j = pl.program_id(1)
    tm = xbuf.shape[0]

    def x_copy(tile):
        rows = pl.ds(pl.multiple_of(tile * tm, tm), tm)
        return pltpu.make_async_copy(x_hbm.at[rows, :], xbuf, sem)

    def half_swiglu(xn):
        gate = jnp.dot(xn, wg_ref[...].astype(BF16), preferred_element_type=F32)
        up = jnp.dot(xn, wu_ref[...].astype(BF16), preferred_element_type=F32)
        act = (jax.nn.silu(gate) * up * 0.5).astype(BF16)
        return jnp.dot(act, wd_ref[...].astype(BF16), preferred_element_type=F32)

    @pl.when(j == 0)
    def _():
        @pl.when(i == 0)
        def _():
            x_copy(0).start()

        x_copy(i).wait()
        x = xbuf[...]
        xn = _rms(x, g_ref[...]).astype(BF16)
        xn_ref[...] = xn
        o_ref[...] = x + half_swiglu(xn)

    @pl.when((j == 1) & (i + 1 < pl.num_programs(0)))
    def _():
        x_copy(i + 1).start()

    @pl.when(j > 0)
    def _():
        o_ref[...] += half_swiglu(xn_ref[...])


def _ffn(x, g, wg, wu, wd):
    t, d = x.shape
    f = wg.shape[1]
    return pl.pallas_call(
        _ffn_kernel,
        grid=(t // FFN_TM, f // FFN_TF),
        in_specs=[
            pl.BlockSpec(memory_space=pl.ANY),
            pl.BlockSpec((1, d), lambda i, j: (0, 0)),
            pl.BlockSpec((d, FFN_TF), lambda i, j: (0, j)),
            pl.BlockSpec((d, FFN_TF), lambda i, j: (0, j)),
            pl.BlockSpec((FFN_TF, d), lambda i, j: (j, 0)),
        ],
        out_specs=pl.BlockSpec((FFN_TM, d), lambda i, j: (i, 0)),
        out_shape=jax.ShapeDtypeStruct((t, d), F32),
        scratch_shapes=[pltpu.VMEM((FFN_TM, d), F32),
                        pltpu.VMEM((FFN_TM, d), BF16),
                        pltpu.SemaphoreType.DMA(())],
        compiler_params=_params(("arbitrary", "arbitrary"), BIG_VMEM_LIMIT),
        name="ffn",
    )(x, g, wg, wu, wd)


N_QKV_STEPS = 3 * D_ATTN // PROJ_TN
N_CONV_STEPS = D_CONV // CONV_TC


def _inproj_kernel(xa_ref, xb_ref, xp_ref, xnx_ref, g_ref, wq_ref, wb_ref, wc_ref, wu_ref,
                   cw_ref, cb_ref, gn_ref, qkv_ref, conv_ref, xn_ref, cu_ref, *, seq):
    tm = xa_ref.shape[0] + xb_ref.shape[0]
    j = pl.program_id(1)

    def qkv_tile(xn):
        acc = jnp.dot(xn, wq_ref[...].astype(BF16), preferred_element_type=F32)
        col_scale = jnp.where(j < D_ATTN // PROJ_TN, Q_SCALE, 1.0)
        qkv_ref[...] = (acc * col_scale).astype(BF16)

    @pl.when(j == 0)
    def _():
        g = g_ref[...]
        xn_ref[0:HALO, :] = _rms(xp_ref[...], g).astype(BF16)
        xn_ref[HALO + tm:, :] = _rms(xnx_ref[...], g).astype(BF16)
        xn = jnp.concatenate([_rms(xa_ref[...], g), _rms(xb_ref[...], g)], axis=0).astype(BF16)
        xn_ref[HALO:HALO + tm, :] = xn
        qkv_tile(xn)

    @pl.when((j > 0) & (j < N_QKV_STEPS))
    def _():
        qkv_tile(xn_ref[HALO:HALO + tm, :])

    @pl.when(j >= N_QKV_STEPS)
    def _():
        wb = wb_ref[...].astype(BF16)
        wc = wc_ref[...].astype(BF16)
        wu = wu_ref[...].astype(BF16)
        cw = cw_ref[...]
        cb = cb_ref[...]
        gn = gn_ref[...]
        ts = tm // CONV_SUBTILES
        for sub in range(CONV_SUBTILES):
            r0 = sub * ts
            xs = xn_ref[r0:r0 + ts + 2 * HALO, :]
            gate_b = jnp.dot(xn_ref[HALO + r0:HALO + r0 + ts, :], wb, preferred_element_type=F32)
            gate_c = jnp.dot(xs, wc, preferred_element_type=F32)
            u = jnp.dot(xs, wu, preferred_element_type=F32)
            cu_ref[sub] = gate_c * u

            pos = (lax.broadcasted_iota(jnp.int32, (ts, 1), 0) + pl.program_id(0) * tm + r0) % seq
            prev = jnp.where(pos == 0, 0.0, cu_ref[sub, HALO - 1:HALO - 1 + ts, :])
            cur = cu_ref[sub, HALO:HALO + ts, :]
            nxt = jnp.where(pos == seq - 1, 0.0, cu_ref[sub, HALO + 1:HALO + 1 + ts, :])
            y = prev * cw[0:1, :]
            y = y + cur * cw[1:2, :]
            y = y + nxt * cw[2:3, :]
            y = gate_b * (y + cb)
            for c in range(0, y.shape[1], GROUP_DIM):
                sl = slice(c, c + GROUP_DIM)
                conv_ref[r0:r0 + ts, sl] = _rms(y[:, sl], gn[:, sl]).astype(BF16)


def _inproj(x, g, w_in, conv_w, conv_b, gn, seq):
    t, d = x.shape
    tm, tn, tc = PROJ_TM, PROJ_TN, CONV_TC
    hb = tm // HALO
    nhb = t // HALO
    col_b = 3 * D_ATTN // tc
    col_c = col_b + D_CONV // tc
    col_u = col_c + D_CONV // tc
    qstep = lambda j: jnp.minimum(j, N_QKV_STEPS - 1)
    cstep = lambda j: jnp.maximum(j - N_QKV_STEPS, 0)
    n_tiles = t // tm
    xtile = lambda i, j, after: jnp.minimum(i + jnp.where(j > after, 1, 0), n_tiles - 1)
    qwstep = lambda j: jnp.where(j < N_QKV_STEPS, j, 0)
    return pl.pallas_call(
        functools.partial(_inproj_kernel, seq=seq),
        grid=(n_tiles, N_QKV_STEPS + N_CONV_STEPS),
        in_specs=[
            pl.BlockSpec((tm // 2, d), lambda i, j: (2 * xtile(i, j, N_QKV_STEPS), 0)),
            pl.BlockSpec((tm // 2, d), lambda i, j: (2 * xtile(i, j, N_QKV_STEPS + 1) + 1, 0)),
            pl.BlockSpec((HALO, d), lambda i, j: (jnp.maximum(xtile(i, j, 0) * hb - 1, 0), 0)),
            pl.BlockSpec((HALO, d),
                         lambda i, j: (jnp.minimum((xtile(i, j, 0) + 1) * hb, nhb - 1), 0)),
            pl.BlockSpec((1, d), lambda i, j: (0, 0)),
            pl.BlockSpec((d, tn), lambda i, j: (0, qwstep(j))),
            pl.BlockSpec((d, tc), lambda i, j: (0, col_b + cstep(j))),
            pl.BlockSpec((d, tc), lambda i, j: (0, col_c + cstep(j))),
            pl.BlockSpec((d, tc), lambda i, j: (0, col_u + cstep(j))),
            pl.BlockSpec((CONV_W, tc), lambda i, j: (0, cstep(j))),
            pl.BlockSpec((1, tc), lambda i, j: (0, cstep(j))),
            pl.BlockSpec((1, tc), lambda i, j: (0, cstep(j))),
        ],
        out_specs=[pl.BlockSpec((tm, tn), lambda i, j: (i, qstep(j))),
                   pl.BlockSpec((tm, tc), lambda i, j: (i, cstep(j)))],
        out_shape=[jax.ShapeDtypeStruct((t, 3 * D_ATTN), BF16),
                   jax.ShapeDtypeStruct((t, D_CONV), BF16)],
        scratch_shapes=[pltpu.VMEM((tm + 2 * HALO, d), BF16),
                        pltpu.VMEM((CONV_SUBTILES, tm // CONV_SUBTILES + 2 * HALO, tc), F32)],
        compiler_params=_params(("parallel", "arbitrary"), BIG_VMEM_LIMIT),
        name="inproj",
    )(x, x, x, x, g, w_in, w_in, w_in, w_in, conv_w, conv_b, gn)


N_REL = 2 * WIN_ROWS - 1
PAIR_FULL = 0
PAIR_LEFT_NEG = N_REL - 1
PAIR_RIGHT_NEG = 2 * N_REL - 1
PAIR_NEG = 3 * N_REL - 1
N_ROW_PATTERNS = 3
N_PAIRS = K_ROWS // 2


def _pair_halves(block_id):
    if block_id < PAIR_LEFT_NEG:
        return block_id, block_id + 1
    if block_id < PAIR_RIGHT_NEG:
        return None, block_id - PAIR_LEFT_NEG
    if block_id < PAIR_NEG:
        return block_id - PAIR_RIGHT_NEG, None
    return None, None


def _pair_seeds(rpb, ids):
    nh, nr, nc = rpb.shape
    w = GRID_W
    scaled = jnp.concatenate([rpb * LOG2E, jnp.zeros((nh, 1, nc), F32)], axis=1)
    halves = [_pair_halves(b) for b in ids]
    left = jnp.stack([scaled[:, nr if l is None else l] for l, _ in halves], axis=1)
    right = jnp.stack([scaled[:, nr if r is None else r] for _, r in halves], axis=1)
    gap = jnp.zeros((nh, len(ids), w - 2 * WIN_COLS + 1), F32)
    seeds = jnp.concatenate(
        [left[..., WIN_COLS - 1:], gap, right, gap, left[..., :WIN_COLS - 1]], axis=-1)
    return seeds.reshape(nh * len(ids), 2 * w)


def _build_pair_blocks(seed_ref, pb_ref, ids):
    w = GRID_W
    q = lax.broadcasted_iota(jnp.int32, (w, 2 * w), 0)
    lane = lax.broadcasted_iota(jnp.int32, (w, 2 * w), 1)
    k = lane % w
    col_start = jnp.clip(q - WIN_COLS // 2, 0, w - WIN_COLS)
    in_win = (k >= col_start) & (k < col_start + WIN_COLS)
    masks = {
        (True, True): in_win,
        (False, True): in_win & (lane >= w),
        (True, False): in_win & (lane < w),
    }
    for h in range(N_HEADS):
        for n, block_id in enumerate(ids):
            left, right = _pair_halves(block_id)
            if left is None and right is None:
                pb_ref[h, n] = jnp.full((w, 2 * w), NEG, F32)
                continue
            row = h * len(ids) + n
            seed = jnp.broadcast_to(seed_ref[row:row + 1, :], (w, 2 * w))
            rotated = pltpu.roll(seed, 0, 1, stride=1, stride_axis=0)
            pb_ref[h, n] = jnp.where(masks[left is not None, right is not None], rotated, NEG)


def _pair_index_table(rows):
    tab = np.zeros((N_ROW_PATTERNS, Q_ROWS, N_PAIRS), np.int32)
    for pat, r0 in enumerate((0, Q_ROWS, rows - Q_ROWS)):
        k0 = min(max(r0 - WIN_ROWS // 2, 0), rows - K_ROWS)
        for i in range(Q_ROWS):
            r = r0 + i
            row_start = min(max(r - WIN_ROWS // 2, 0), rows - WIN_ROWS)
            for m in range(N_PAIRS):
                kr = k0 + 2 * m
                in0 = row_start <= kr < row_start + WIN_ROWS
                in1 = row_start <= kr + 1 < row_start + WIN_ROWS
                a = kr - r + WIN_ROWS - 1
                if in0 and in1:
                    tab[pat, i, m] = PAIR_FULL + a
                elif in1:
                    tab[pat, i, m] = PAIR_LEFT_NEG + a + 1
                elif in0:
                    tab[pat, i, m] = PAIR_RIGHT_NEG + a
                else:
                    tab[pat, i, m] = PAIR_NEG
    return tab


def _chunk_spans(tab):
    pairs_per_chunk = K_CHUNK_ROWS // 2
    spans = []
    for pat in range(tab.shape[0]):
        row = []
        for c in range(K_ROWS // K_CHUNK_ROWS):
            blk = tab[pat, :, c * pairs_per_chunk:(c + 1) * pairs_per_chunk]
            hit = np.nonzero((blk != PAIR_NEG).any(axis=1))[0]
            if hit.size == 0:
                row.append((0, 0))
            else:
                assert hit[-1] - hit[0] + 1 == hit.size
                row.append((int(hit[0]), int(hit[-1]) + 1))
        spans.append(row)
    return spans


def _attn_head(h, tab, spans, q_ref, ks, vs, pb_ref, gn_ref, o_ref):
    hs = slice(h * HEAD_DIM, (h + 1) * HEAD_DIM)
    pairs_per_chunk = K_CHUNK_ROWS // 2
    live = [c for c, (first, last) in enumerate(spans) if last > first]
    w = GRID_W
    scores = {}
    m = [None] * Q_ROWS
    for c in live:
        first, last = spans[c]
        s = lax.dot_general(q_ref[first * w:last * w, hs], ks[c][:, hs],
                            (((1,), (1,)), ((), ())), preferred_element_type=F32)
        for i in range(first, last):
            bias = jnp.concatenate([pb_ref[h, int(tab[i, c * pairs_per_chunk + k])]
                                    for k in range(pairs_per_chunk)], axis=1)
            si = s[(i - first) * w:(i - first + 1) * w] + bias
            scores[c, i] = si
            mi = jnp.max(si, axis=1, keepdims=True)
            m[i] = mi if m[i] is None else jnp.maximum(m[i], mi)
    l = [None] * Q_ROWS
    o = [None] * Q_ROWS
    for c in live:
        first, last = spans[c]
        ps = []
        for i in range(first, last):
            p = jnp.exp2(scores[c, i] - m[i])
            li = jnp.sum(p, axis=1, keepdims=True)
            l[i] = li if l[i] is None else l[i] + li
            ps.append(p.astype(BF16))
        oc = jnp.dot(jnp.concatenate(ps, axis=0), vs[c][:, hs], preferred_element_type=F32)
        for i in range(first, last):
            oi = oc[(i - first) * w:(i - first + 1) * w]
            o[i] = oi if o[i] is None else o[i] + oi
    gn = gn_ref[:, hs]
    for i in range(Q_ROWS):
        o_ref[i * w:(i + 1) * w, hs] = _rms(o[i] / l[i], gn).astype(BF16)


def _attn_kernel(q_ref, k0, k1, k2, k3, v0, v1, v2, v3, pb_ref, gn_ref, o_ref, *, tab, spans):
    rb = pl.program_id(1)
    pat = jnp.where(rb == 0, 0, jnp.where(rb == pl.num_programs(1) - 1, 2, 1))
    for pat_id in range(N_ROW_PATTERNS):
        @pl.when(pat == pat_id)
        def _(pat_id=pat_id):
            for h in range(N_HEADS):
                _attn_head(h, tab[pat_id], spans[pat_id], q_ref, (k0, k1, k2, k3),
                           (v0, v1, v2, v3), pb_ref, gn_ref, o_ref)


def _attn(qkv, rpb, gn, batch, seq):
    rows = seq // GRID_W
    tq = Q_ROWS * GRID_W
    kc = K_CHUNK_ROWS * GRID_W
    n_rb = rows // Q_ROWS
    n_kc = K_ROWS // K_CHUNK_ROWS
    chunks_per_seq = seq // kc
    tab = _pair_index_table(rows)
    spans = _chunk_spans(tab)
    ids = [int(v) for v in np.unique(tab)]
    pair_blocks = pl.pallas_call(
        functools.partial(_build_pair_blocks, ids=ids),
        out_shape=jax.ShapeDtypeStruct((N_HEADS, len(ids), GRID_W, 2 * GRID_W), F32),
        name="pairblocks",
    )(_pair_seeds(rpb, ids))
    tab = np.searchsorted(ids, tab)

    def q_map(b, rb):
        return (b * n_rb + rb, 0)

    def kv_map(col, c):
        def f(b, rb):
            first = jnp.clip(rb * (Q_ROWS // K_CHUNK_ROWS) - (WIN_ROWS // 2) // K_CHUNK_ROWS,
                             0, chunks_per_seq - n_kc)
            return (b * chunks_per_seq + first + c, col)
        return f

    k_specs = [pl.BlockSpec((kc, D_ATTN), kv_map(1, c)) for c in range(n_kc)]
    v_specs = [pl.BlockSpec((kc, D_ATTN), kv_map(2, c)) for c in range(n_kc)]
    return pl.pallas_call(
        functools.partial(_attn_kernel, tab=tab, spans=spans),
        grid=(batch, n_rb),
        in_specs=[pl.BlockSpec((tq, D_ATTN), q_map)] + k_specs + v_specs + [
            pl.BlockSpec(pair_blocks.shape, lambda b, rb: (0, 0, 0, 0),
                         pipeline_mode=pl.Buffered(1)),
            pl.BlockSpec((1, D_ATTN), lambda b, rb: (0, 0)),
        ],
        out_specs=pl.BlockSpec((tq, D_ATTN), q_map),
        out_shape=jax.ShapeDtypeStruct((batch * seq, D_ATTN), BF16),
        compiler_params=_params(("parallel", "arbitrary")),
        name="attn",
    )(qkv, *([qkv] * (2 * n_kc)), pair_blocks, gn)


def _outproj_kernel(x_ref, a_ref, c_ref, wa_ref, wc_ref, o_ref):
    acc = jnp.dot(a_ref[...], wa_ref[...].astype(BF16), preferred_element_type=F32)
    acc = acc + jnp.dot(c_ref[...], wc_ref[...].astype(BF16), preferred_element_type=F32)
    o_ref[...] = x_ref[...] + acc


def _outproj(x, a, c, w_out):
    t, d = x.shape
    tm = OUT_TM
    return pl.pallas_call(
        _outproj_kernel,
        grid=(t // tm,),
        in_specs=[
            pl.BlockSpec((tm, d), lambda i: (i, 0)),
            pl.BlockSpec((tm, D_ATTN), lambda i: (i, 0)),
            pl.BlockSpec((tm, D_CONV), lambda i: (i, 0)),
            pl.BlockSpec((D_ATTN, d), lambda i: (0, 0), pipeline_mode=pl.Buffered(1)),
            pl.BlockSpec((D_CONV, d), lambda i: (1, 0), pipeline_mode=pl.Buffered(1)),
        ],
        out_specs=pl.BlockSpec((tm, d), lambda i: (i, 0)),
        out_shape=jax.ShapeDtypeStruct((t, d), F32),
        compiler_params=_params(("parallel",)),
        name="outproj",
    )(x, a, c, w_out, w_out)


def _ple_kernel(x_ref, p_ref, g_ref, wg_ref, wp_ref, gf_ref, o_ref):
    x = x_ref[...]
    xn = _rms(x, g_ref[...]).astype(BF16)
    gate = jax.nn.sigmoid(jnp.dot(xn, wg_ref[...].astype(BF16), preferred_element_type=F32))
    proj = jnp.dot(p_ref[...].astype(BF16), wp_ref[...].astype(BF16),
                   preferred_element_type=F32)
    o_ref[...] = _rms(x + gate * proj, gf_ref[...])


def _ple(x, p, g, w_gate, w_proj, gf):
    t, d = x.shape
    tm = PLE_TM
    return pl.pallas_call(
        _ple_kernel,
        grid=(t // tm,),
        in_specs=[
            pl.BlockSpec((tm, d), lambda i: (i, 0)),
            pl.BlockSpec((tm, PLE_DIM), lambda i: (i, 0)),
            pl.BlockSpec((1, d), lambda i: (0, 0)),
            pl.BlockSpec((d, d), lambda i: (0, 0), pipeline_mode=pl.Buffered(1)),
            pl.BlockSpec((PLE_DIM, d), lambda i: (0, 0), pipeline_mode=pl.Buffered(1)),
            pl.BlockSpec((1, d), lambda i: (0, 0)),
        ],
        out_specs=pl.BlockSpec((tm, d), lambda i: (i, 0)),
        out_shape=jax.ShapeDtypeStruct((t, d), F32),
        compiler_params=_params(("parallel",)),
        name="ple",
    )(x, p, g, w_gate, w_proj, gf)


def kernel(x, p, ffn1_norm, ffn1_wg, ffn1_wu, ffn1_wd, mix_norm, w_in, rpb, conv_w, conv_b,
           attn_out_norm, conv_out_norm, w_out, ffn2_norm, ffn2_wg, ffn2_wu, ffn2_wd,
           ple_norm, ple_w_gate, ple_w_proj, final_norm):
    b, s, d = x.shape
    assert ffn1_wg.shape[0] == 1
    assert d == D_MODEL and s % (Q_ROWS * GRID_W) == 0 and s % PROJ_TM == 0
    t = b * s
    h = x.reshape(t, d)
    row = lambda v: v.reshape(1, -1)
    for i in range(1):
        h = _ffn(h, row(ffn1_norm[i]), ffn1_wg[i], ffn1_wu[i], ffn1_wd[i])
        qkv, y_conv = _inproj(h, row(mix_norm[i]), w_in[i], conv_w[i], row(conv_b[i]),
                              row(conv_out_norm[i]), s)
        y_attn = _attn(qkv, rpb[i], row(attn_out_norm[i]), b, s)
        h = _outproj(h, y_attn, y_conv, w_out[i])
        h = _ffn(h, row(ffn2_norm[i]), ffn2_wg[i], ffn2_wu[i], ffn2_wd[i])
        h = _ple(h, p[i].reshape(t, PLE_DIM), row(ple_norm[i]), ple_w_gate[i], ple_w_proj[i],
                 row(final_norm))
    return h.reshape(b, s, d)
```

```python
import functools

import numpy as np
import jax
import jax.numpy as jnp
from jax import lax
from jax.experimental import pallas as pl
from jax.experimental.pallas import tpu as pltpu

D_MODEL = 2048
GRID_W = 64
PLE_DIM = 256
D_ATTN = D_MODEL // 2
D_CONV = D_MODEL - D_ATTN
N_HEADS = 8
HEAD_DIM = D_ATTN // N_HEADS
GROUP_DIM = 128
CONV_W = 3
WIN_ROWS = 8
WIN_COLS = 16
RMS_EPS = 1e-6
NEG = -1e30
LOG2E = 1.4426950408889634
Q_SCALE = HEAD_DIM ** -0.5 * LOG2E

BF16 = jnp.bfloat16
F32 = jnp.float32

VMEM_LIMIT = 56 * 1024 * 1024
BIG_VMEM_LIMIT = 60 * 1024 * 1024
FFN_TM = 1024
FFN_TF = 512
PROJ_TM = 1024
PROJ_TN = 1024
CONV_TC = 256
CONV_SUBTILES = 2
HALO = 16
Q_ROWS = 8
K_ROWS = 16
K_CHUNK_ROWS = 4
OUT_TM = 512
PLE_TM = 512


def _rms(x, g):
    ms = jnp.mean(x * x, axis=-1, keepdims=True)
    return x * lax.rsqrt(ms + RMS_EPS) * g


def _params(sem, vmem_limit=VMEM_LIMIT):
    return pltpu.CompilerParams(dimension_semantics=sem, vmem_limit_bytes=vmem_limit)


def _ffn_kernel(x_hbm, g_ref, wg_ref, wu_ref, wd_ref, o_ref, xbuf, xn_ref, sem):
    i = pl.program_id(0)
    j = pl.program_id(1)
    tm = xbuf.shape[0]

    def x_copy(tile):
        rows = pl.ds(pl.multiple_of(tile * tm, tm), tm)
        return pltpu.make_async_copy(x_hbm.at[rows, :], xbuf, sem)

    def half_swiglu(xn):
        gate = jnp.dot(xn, wg_ref[...].astype(BF16), preferred_element_type=F32)
        up = jnp.dot(xn, wu_ref[...].astype(BF16), preferred_element_type=F32)
        act = (jax.nn.silu(gate) * up * 0.5).astype(BF16)
        return jnp.dot(act, wd_ref[...].astype(BF16), preferred_element_type=F32)

    @pl.when(j == 0)
    def _():
        @pl.when(i == 0)
        def _():
            x_copy(0).start()

        x_copy(i).wait()
        x = xbuf[...]
        xn = _rms(x, g_ref[...]).astype(BF16)
        xn_ref[...] = xn
        o_ref[...] = x + half_swiglu(xn)

    @pl.when((j == 1) & (i + 1 < pl.num_programs(0)))
    def _():
        x_copy(i + 1).start()

    @pl.when(j > 0)
    def _():
        o_ref[...] += half_swiglu(xn_ref[...])


def _ffn(x, g, wg, wu, wd):
    t, d = x.shape
    f = wg.shape[1]
    return pl.pallas_call(
        _ffn_kernel,
        grid=(t // FFN_TM, f // FFN_TF),
        in_specs=[
            pl.BlockSpec(memory_space=pl.ANY),
            pl.BlockSpec((1, d), lambda i, j: (0, 0)),
            pl.BlockSpec((d, FFN_TF), lambda i, j: (0, j)),
            pl.BlockSpec((d, FFN_TF), lambda i, j: (0, j)),
            pl.BlockSpec((FFN_TF, d), lambda i, j: (j, 0)),
        ],
        out_specs=pl.BlockSpec((FFN_TM, d), lambda i, j: (i, 0)),
        out_shape=jax.ShapeDtypeStruct((t, d), F32),
        scratch_shapes=[pltpu.VMEM((FFN_TM, d), F32),
                        pltpu.VMEM((FFN_TM, d), BF16),
                        pltpu.SemaphoreType.DMA(())],
        compiler_params=_params(("arbitrary", "arbitrary"), BIG_VMEM_LIMIT),
        name="ffn",
    )(x, g, wg, wu, wd)


N_QKV_STEPS = 3 * D_ATTN // PROJ_TN
N_CONV_STEPS = D_CONV // CONV_TC


def _inproj_kernel(xa_ref, xb_ref, xp_ref, xnx_ref, g_ref, wq_ref, wb_ref, wc_ref, wu_ref,
                   cw_ref, cb_ref, gn_ref, qkv_ref, conv_ref, xn_ref, cu_ref, *, seq):
    tm = xa_ref.shape[0] + xb_ref.shape[0]
    j = pl.program_id(1)

    def qkv_tile(xn):
        acc = jnp.dot(xn, wq_ref[...].astype(BF16), preferred_element_type=F32)
        col_scale = jnp.where(j < D_ATTN // PROJ_TN, Q_SCALE, 1.0)
        qkv_ref[...] = (acc * col_scale).astype(BF16)

    @pl.when(j == 0)
    def _():
        g = g_ref[...]
        xn_ref[0:HALO, :] = _rms(xp_ref[...], g).astype(BF16)
        xn_ref[HALO + tm:, :] = _rms(xnx_ref[...], g).astype(BF16)
        xn = jnp.concatenate([_rms(xa_ref[...], g), _rms(xb_ref[...], g)], axis=0).astype(BF16)
        xn_ref[HALO:HALO + tm, :] = xn
        qkv_tile(xn)

    @pl.when((j > 0) & (j < N_QKV_STEPS))
    def _():
        qkv_tile(xn_ref[HALO:HALO + tm, :])

    @pl.when(j >= N_QKV_STEPS)
    def _():
        wb = wb_ref[...].astype(BF16)
        wc = wc_ref[...].astype(BF16)
        wu = wu_ref[...].astype(BF16)
        cw = cw_ref[...]
        cb = cb_ref[...]
        gn = gn_ref[...]
        ts = tm // CONV_SUBTILES
        for sub in range(CONV_SUBTILES):
            r0 = sub * ts
            xs = xn_ref[r0:r0 + ts + 2 * HALO, :]
            gate_b = jnp.dot(xn_ref[HALO + r0:HALO + r0 + ts, :], wb, preferred_element_type=F32)
            gate_c = jnp.dot(xs, wc, preferred_element_type=F32)
            u = jnp.dot(xs, wu, preferred_element_type=F32)
            cu_ref[sub] = gate_c * u

            pos = (lax.broadcasted_iota(jnp.int32, (ts, 1), 0) + pl.program_id(0) * tm + r0) % seq
            prev = jnp.where(pos == 0, 0.0, cu_ref[sub, HALO - 1:HALO - 1 + ts, :])
            cur = cu_ref[sub, HALO:HALO + ts, :]
            nxt = jnp.where(pos == seq - 1, 0.0, cu_ref[sub, HALO + 1:HALO + 1 + ts, :])
            y = prev * cw[0:1, :]
            y = y + cur * cw[1:2, :]
            y = y + nxt * cw[2:3, :]
            y = gate_b * (y + cb)
            for c in range(0, y.shape[1], GROUP_DIM):
                sl = slice(c, c + GROUP_DIM)
                conv_ref[r0:r0 + ts, sl] = _rms(y[:, sl], gn[:, sl]).astype(BF16)


def _inproj(x, g, w_in, conv_w, conv_b, gn, seq):
    t, d = x.shape
    tm, tn, tc = PROJ_TM, PROJ_TN, CONV_TC
    hb = tm // HALO
    nhb = t // HALO
    col_b = 3 * D_ATTN // tc
    col_c = col_b + D_CONV // tc
    col_u = col_c + D_CONV // tc
    qstep = lambda j: jnp.minimum(j, N_QKV_STEPS - 1)
    cstep = lambda j: jnp.maximum(j - N_QKV_STEPS, 0)
    n_tiles = t // tm
    xtile = lambda i, j, after: jnp.minimum(i + jnp.where(j > after, 1, 0), n_tiles - 1)
    qwstep = lambda j: jnp.where(j < N_QKV_STEPS, j, 0)
    return pl.pallas_call(
        functools.partial(_inproj_kernel, seq=seq),
        grid=(n_tiles, N_QKV_STEPS + N_CONV_STEPS),
        in_specs=[
            pl.BlockSpec((tm // 2, d), lambda i, j: (2 * xtile(i, j, N_QKV_STEPS), 0)),
            pl.BlockSpec((tm // 2, d), lambda i, j: (2 * xtile(i, j, N_QKV_STEPS + 1) + 1, 0)),
            pl.BlockSpec((HALO, d), lambda i, j: (jnp.maximum(xtile(i, j, 0) * hb - 1, 0), 0)),
            pl.BlockSpec((HALO, d),
                         lambda i, j: (jnp.minimum((xtile(i, j, 0) + 1) * hb, nhb - 1), 0)),
            pl.BlockSpec((1, d), lambda i, j: (0, 0)),
            pl.BlockSpec((d, tn), lambda i, j: (0, qwstep(j))),
            pl.BlockSpec((d, tc), lambda i, j: (0, col_b + cstep(j))),
            pl.BlockSpec((d, tc), lambda i, j: (0, col_c + cstep(j))),
            pl.BlockSpec((d, tc), lambda i, j: (0, col_u + cstep(j))),
            pl.BlockSpec((CONV_W, tc), lambda i, j: (0, cstep(j))),
            pl.BlockSpec((1, tc), lambda i, j: (0, cstep(j))),
            pl.BlockSpec((1, tc), lambda i, j: (0, cstep(j))),
        ],
        out_specs=[pl.BlockSpec((tm, tn), lambda i, j: (i, qstep(j))),
                   pl.BlockSpec((tm, tc), lambda i, j: (i, cstep(j)))],
        out_shape=[jax.ShapeDtypeStruct((t, 3 * D_ATTN), BF16),
                   jax.ShapeDtypeStruct((t, D_CONV), BF16)],
        scratch_shapes=[pltpu.VMEM((tm + 2 * HALO, d), BF16),
                        pltpu.VMEM((CONV_SUBTILES, tm // CONV_SUBTILES + 2 * HALO, tc), F32)],
        compiler_params=_params(("parallel", "arbitrary"), BIG_VMEM_LIMIT),
        name="inproj",
    )(x, x, x, x, g, w_in, w_in, w_in, w_in, conv_w, conv_b, gn)


N_REL = 2 * WIN_ROWS - 1
PAIR_FULL = 0
PAIR_LEFT_NEG = N_REL - 1
PAIR_RIGHT_NEG = 2 * N_REL - 1
PAIR_NEG = 3 * N_REL - 1
N_ROW_PATTERNS = 3
N_PAIRS = K_ROWS // 2


def _pair_halves(block_id):
    if block_id < PAIR_LEFT_NEG:
        return block_id, block_id + 1
    if block_id < PAIR_RIGHT_NEG:
        return None, block_id - PAIR_LEFT_NEG
    if block_id < PAIR_NEG:
        return block_id - PAIR_RIGHT_NEG, None
    return None, None


def _pair_seeds(rpb, ids):
    nh, nr, nc = rpb.shape
    w = GRID_W
    scaled = jnp.concatenate([rpb * LOG2E, jnp.zeros((nh, 1, nc), F32)], axis=1)
    halves = [_pair_halves(b) for b in ids]
    left = jnp.stack([scaled[:, nr if l is None else l] for l, _ in halves], axis=1)
    right = jnp.stack([scaled[:, nr if r is None else r] for _, r in halves], axis=1)
    gap = jnp.zeros((nh, len(ids), w - 2 * WIN_COLS + 1), F32)
    seeds = jnp.concatenate(
        [left[..., WIN_COLS - 1:], gap, right, gap, left[..., :WIN_COLS - 1]], axis=-1)
    return seeds.reshape(nh * len(ids), 2 * w)


def _build_pair_blocks(seed_ref, pb_ref, ids):
    w = GRID_W
    q = lax.broadcasted_iota(jnp.int32, (w, 2 * w), 0)
    lane = lax.broadcasted_iota(jnp.int32, (w, 2 * w), 1)
    k = lane % w
    col_start = jnp.clip(q - WIN_COLS // 2, 0, w - WIN_COLS)
    in_win = (k >= col_start) & (k < col_start + WIN_COLS)
    masks = {
        (True, True): in_win,
        (False, True): in_win & (lane >= w),
        (True, False): in_win & (lane < w),
    }
    for h in range(N_HEADS):
        for n, block_id in enumerate(ids):
            left, right = _pair_halves(block_id)
            if left is None and right is None:
                pb_ref[h, n] = jnp.full((w, 2 * w), NEG, F32)
                continue
            row = h * len(ids) + n
            seed = jnp.broadcast_to(seed_ref[row:row + 1, :], (w, 2 * w))
            rotated = pltpu.roll(seed, 0, 1, stride=1, stride_axis=0)
            pb_ref[h, n] = jnp.where(masks[left is not None, right is not None], rotated, NEG)


def _pair_index_table(rows):
    tab = np.zeros((N_ROW_PATTERNS, Q_ROWS, N_PAIRS), np.int32)
    for pat, r0 in enumerate((0, Q_ROWS, rows - Q_ROWS)):
        k0 = min(max(r0 - WIN_ROWS // 2, 0), rows - K_ROWS)
        for i in range(Q_ROWS):
            r = r0 + i
            row_start = min(max(r - WIN_ROWS // 2, 0), rows - WIN_ROWS)
            for m in range(N_PAIRS):
                kr = k0 + 2 * m
                in0 = row_start <= kr < row_start + WIN_ROWS
                in1 = row_start <= kr + 1 < row_start + WIN_ROWS
                a = kr - r + WIN_ROWS - 1
                if in0 and in1:
                    tab[pat, i, m] = PAIR_FULL + a
                elif in1:
                    tab[pat, i, m] = PAIR_LEFT_NEG + a + 1
                elif in0:
                    tab[pat, i, m] = PAIR_RIGHT_NEG + a
                else:
                    tab[pat, i, m] = PAIR_NEG
    return tab


def _chunk_spans(tab):
    pairs_per_chunk = K_CHUNK_ROWS // 2
    spans = []
    for pat in range(tab.shape[0]):
        row = []
        for c in range(K_ROWS // K_CHUNK_ROWS):
            blk = tab[pat, :, c * pairs_per_chunk:(c + 1) * pairs_per_chunk]
            hit = np.nonzero((blk != PAIR_NEG).any(axis=1))[0]
            if hit.size == 0:
                row.append((0, 0))
            else:
                assert hit[-1] - hit[0] + 1 == hit.size
                row.append((int(hit[0]), int(hit[-1]) + 1))
        spans.append(row)
    return spans


def _attn_head(h, tab, spans, q_ref, ks, vs, pb_ref, gn_ref, o_ref):
    hs = slice(h * HEAD_DIM, (h + 1) * HEAD_DIM)
    pairs_per_chunk = K_CHUNK_ROWS // 2
    live = [c for c, (first, last) in enumerate(spans) if last > first]
    w = GRID_W
    scores = {}
    m = [None] * Q_ROWS
    for c in live:
        first, last = spans[c]
        s = lax.dot_general(q_ref[first * w:last * w, hs], ks[c][:, hs],
                            (((1,), (1,)), ((), ())), preferred_element_type=F32)
        for i in range(first, last):
            bias = jnp.concatenate([pb_ref[h, int(tab[i, c * pairs_per_chunk + k])]
                                    for k in range(pairs_per_chunk)], axis=1)
            si = s[(i - first) * w:(i - first + 1) * w] + bias
            scores[c, i] = si
            mi = jnp.max(si, axis=1, keepdims=True)
            m[i] = mi if m[i] is None else jnp.maximum(m[i], mi)
    l = [None] * Q_ROWS
    o = [None] * Q_ROWS
    for c in live:
        first, last = spans[c]
        ps = []
        for i in range(first, last):
            p = jnp.exp2(scores[c, i] - m[i])
            li = jnp.sum(p, axis=1, keepdims=True)
            l[i] = li if l[i] is None else l[i] + li
            ps.append(p.astype(BF16))
        oc = jnp.dot(jnp.concatenate(ps, axis=0), vs[c][:, hs], preferred_element_type=F32)
        for i in range(first, last):
            oi = oc[(i - first) * w:(i - first + 1) * w]
            o[i] = oi if o[i] is None else o[i] + oi
    gn = gn_ref[:, hs]
    for i in range(Q_ROWS):
        o_ref[i * w:(i + 1) * w, hs] = _rms(o[i] / l[i], gn).astype(BF16)


def _attn_kernel(q_ref, k0, k1, k2, k3, v0, v1, v2, v3, seed_ref, gn_ref, o_ref, pb_ref, *,
                 tab, spans, ids):
    rb = pl.program_id(1)

    @pl.when((pl.program_id(0) == 0) & (rb == 0))
    def _():
        _build_pair_blocks(seed_ref, pb_ref, ids)

    pat = jnp.where(rb == 0, 0, jnp.where(rb == pl.num_programs(1) - 1, 2, 1))
    for pat_id in range(N_ROW_PATTERNS):
        @pl.when(pat == pat_id)
        def _(pat_id=pat_id):
            for h in range(N_HEADS):
                _attn_head(h, tab[pat_id], spans[pat_id], q_ref, (k0, k1, k2, k3),
                           (v0, v1, v2, v3), pb_ref, gn_ref, o_ref)


def _attn(qkv, rpb, gn, batch, seq):
    rows = seq // GRID_W
    tq = Q_ROWS * GRID_W
    kc = K_CHUNK_ROWS * GRID_W
    n_rb = rows // Q_ROWS
    n_kc = K_ROWS // K_CHUNK_ROWS
    chunks_per_seq = seq // kc
    tab = _pair_index_table(rows)
    spans = _chunk_spans(tab)
    ids = [int(v) for v in np.unique(tab)]
    seeds = _pair_seeds(rpb, ids)
    tab = np.searchsorted(ids, tab)

    def q_map(b, rb):
        return (b * n_rb + rb, 0)

    def kv_map(col, c):
        def f(b, rb):
            first = jnp.clip(rb * (Q_ROWS // K_CHUNK_ROWS) - (WIN_ROWS // 2) // K_CHUNK_ROWS,
                             0, chunks_per_seq - n_kc)
            return (b * chunks_per_seq + first + c, col)
        return f

    k_specs = [pl.BlockSpec((kc, D_ATTN), kv_map(1, c)) for c in range(n_kc)]
    v_specs = [pl.BlockSpec((kc, D_ATTN), kv_map(2, c)) for c in range(n_kc)]
    return pl.pallas_call(
        functools.partial(_attn_kernel, tab=tab, spans=spans, ids=ids),
        grid=(batch, n_rb),
        in_specs=[pl.BlockSpec((tq, D_ATTN), q_map)] + k_specs + v_specs + [
            pl.BlockSpec(seeds.shape, lambda b, rb: (0, 0)),
            pl.BlockSpec((1, D_ATTN), lambda b, rb: (0, 0)),
        ],
        out_specs=pl.BlockSpec((tq, D_ATTN), q_map),
        out_shape=jax.ShapeDtypeStruct((batch * seq, D_ATTN), BF16),
        scratch_shapes=[pltpu.VMEM((N_HEADS, len(ids), GRID_W, 2 * GRID_W), F32)],
        compiler_params=_params(("arbitrary", "arbitrary")),
        name="attn",
    )(qkv, *([qkv] * (2 * n_kc)), seeds, gn)


def _outproj_kernel(xe_ref, xo_ref, a_ref, c_ref, wa_ref, wc_ref, o_ref):
    def body(x_ref):
        acc = jnp.dot(a_ref[...], wa_ref[...].astype(BF16), preferred_element_type=F32)
        acc = acc + jnp.dot(c_ref[...], wc_ref[...].astype(BF16), preferred_element_type=F32)
        o_ref[...] = x_ref[...] + acc

    odd = pl.program_id(0) % 2
    pl.when(odd == 0)(lambda: body(xe_ref))
    pl.when(odd == 1)(lambda: body(xo_ref))


def _outproj(x, a, c, w_out):
    t, d = x.shape
    tm = OUT_TM
    n = t // tm
    return pl.pallas_call(
        _outproj_kernel,
        grid=(n,),
        in_specs=[
            pl.BlockSpec((tm, d), lambda i: (jnp.minimum(i + i % 2, n - 2), 0)),
            pl.BlockSpec((tm, d), lambda i: (i + 1 - i % 2, 0)),
            pl.BlockSpec((tm, D_ATTN), lambda i: (i, 0)),
            pl.BlockSpec((tm, D_CONV), lambda i: (i, 0)),
            pl.BlockSpec((D_ATTN, d), lambda i: (0, 0), pipeline_mode=pl.Buffered(1)),
            pl.BlockSpec((D_CONV, d), lambda i: (1, 0), pipeline_mode=pl.Buffered(1)),
        ],
        out_specs=pl.BlockSpec((tm, d), lambda i: (i, 0)),
        out_shape=jax.ShapeDtypeStruct((t, d), F32),
        compiler_params=_params(("parallel",)),
        name="outproj",
    )(x, x, a, c, w_out, w_out)


def _ple_kernel(x_ref, p_ref, g_ref, wg_ref, wp_ref, gf_ref, o_ref):
    x = x_ref[...]
    xn = _rms(x, g_ref[...]).astype(BF16)
    gate = jax.nn.sigmoid(jnp.dot(xn, wg_ref[...].astype(BF16), preferred_element_type=F32))
    proj = jnp.dot(p_ref[...].astype(BF16), wp_ref[...].astype(BF16),
                   preferred_element_type=F32)
    o_ref[...] = _rms(x + gate * proj, gf_ref[...])


def _ple(x, p, g, w_gate, w_proj, gf):
    t, d = x.shape
    tm = PLE_TM
    return pl.pallas_call(
        _ple_kernel,
        grid=(t // tm,),
        in_specs=[
            pl.BlockSpec((tm, d), lambda i: (i, 0)),
            pl.BlockSpec((tm, PLE_DIM), lambda i: (i, 0)),
            pl.BlockSpec((1, d), lambda i: (0, 0)),
            pl.BlockSpec((d, d), lambda i: (0, 0), pipeline_mode=pl.Buffered(1)),
            pl.BlockSpec((PLE_DIM, d), lambda i: (0, 0), pipeline_mode=pl.Buffered(1)),
            pl.BlockSpec((1, d), lambda i: (0, 0)),
        ],
        out_specs=pl.BlockSpec((tm, d), lambda i: (i, 0)),
        out_shape=jax.ShapeDtypeStruct((t, d), F32),
        compiler_params=_params(("parallel",)),
        name="ple",
    )(x, p, g, w_gate, w_proj, gf)


def kernel(x, p, ffn1_norm, ffn1_wg, ffn1_wu, ffn1_wd, mix_norm, w_in, rpb, conv_w, conv_b,
           attn_out_norm, conv_out_norm, w_out, ffn2_norm, ffn2_wg, ffn2_wu, ffn2_wd,
           ple_norm, ple_w_gate, ple_w_proj, final_norm):
    b, s, d = x.shape
    assert ffn1_wg.shape[0] == 1
    assert d == D_MODEL and s % (Q_ROWS * GRID_W) == 0 and s % PROJ_TM == 0
    t = b * s
    h = x.reshape(t, d)
    row = lambda v: v.reshape(1, -1)
    for i in range(1):
        h = _ffn(h, row(ffn1_norm[i]), ffn1_wg[i], ffn1_wu[i], ffn1_wd[i])
        qkv, y_conv = _inproj(h, row(mix_norm[i]), w_in[i], conv_w[i], row(conv_b[i]),
                              row(conv_out_norm[i]), s)
        y_attn = _attn(qkv, rpb[i], row(attn_out_norm[i]), b, s)
        h = _outproj(h, y_attn, y_conv, w_out[i])
        h = _ffn(h, row(ffn2_norm[i]), ffn2_wg[i], ffn2_wu[i], ffn2_wd[i])
        h = _ple(h, p[i].reshape(t, PLE_DIM), row(ple_norm[i]), ple_w_gate[i], ple_w_proj[i],
                 row(final_norm))
    return h.reshape(b, s, d)
```

```python
import functools

import numpy as np
import jax
import jax.numpy as jnp
from jax import lax
from jax.experimental import pallas as pl
from jax.experimental.pallas import tpu as pltpu

D_MODEL = 2048
GRID_W = 64
PLE_DIM = 256
D_ATTN = D_MODEL // 2
D_CONV = D_MODEL - D_ATTN
N_HEADS = 8
HEAD_DIM = D_ATTN // N_HEADS
GROUP_DIM = 128
CONV_W = 3
WIN_ROWS = 8
WIN_COLS = 16
RMS_EPS = 1e-6
NEG = -1e30
LOG2E = 1.4426950408889634
Q_SCALE = HEAD_DIM ** -0.5 * LOG2E

BF16 = jnp.bfloat16
F32 = jnp.float32

VMEM_LIMIT = 56 * 1024 * 1024
BIG_VMEM_LIMIT = 60 * 1024 * 1024
FFN_TM = 1024
FFN_TF = 512
PROJ_TM = 1024
PROJ_TN = 1024
CONV_TC = 256
CONV_SUBTILES = 2
HALO = 16
Q_ROWS = 8
K_ROWS = 16
K_CHUNK_ROWS = 4
OUT_TM = 512
PLE_TM = 512


def _rms(x, g):
    ms = jnp.mean(x * x, axis=-1, keepdims=True)
    return x * lax.rsqrt(ms + RMS_EPS) * g


def _params(sem, vmem_limit=VMEM_LIMIT):
    return pltpu.CompilerParams(dimension_semantics=sem, vmem_limit_bytes=vmem_limit)


def _ffn_kernel(x_hbm, g_ref, wg_ref, wu_ref, wd_ref, o_ref, xbuf, xn_ref, sem):
    i = pl.program_id(0)
    j = pl.program_id(1)
    tm = xbuf.shape[0]

    def x_copy(tile):
        rows = pl.ds(pl.multiple_of(tile * tm, tm), tm)
        return pltpu.make_async_copy(x_hbm.at[rows, :], xbuf, sem)

    def half_swiglu(xn):
        gate = jnp.dot(xn, wg_ref[...].astype(BF16), preferred_element_type=F32)
        up = jnp.dot(xn, wu_ref[...].astype(BF16), preferred_element_type=F32)
        act = (jax.nn.silu(gate) * up * 0.5).astype(BF16)
        return jnp.dot(act, wd_ref[...].astype(BF16), preferred_element_type=F32)

    @pl.when(j == 0)
    def _():
        @pl.when(i == 0)
        def _():
            x_copy(0).start()

        x_copy(i).wait()
        x = xbuf[...]
        xn = _rms(x, g_ref[...]).astype(BF16)
        xn_ref[...] = xn
        o_ref[...] = x + half_swiglu(xn)

    @pl.when((j == 1) & (i + 1 < pl.num_programs(0)))
    def _():
        x_copy(i + 1).start()

    @pl.when(j > 0)
    def _():
        o_ref[...] += half_swiglu(xn_ref[...])


def _ffn(x, g, wg, wu, wd):
    t, d = x.shape
    f = wg.shape[1]
    return pl.pallas_call(
        _ffn_kernel,
        grid=(t // FFN_TM, f // FFN_TF),
        in_specs=[
            pl.BlockSpec(memory_space=pl.ANY),
            pl.BlockSpec((1, d), lambda i, j: (0, 0)),
            pl.BlockSpec((d, FFN_TF), lambda i, j: (0, j)),
            pl.BlockSpec((d, FFN_TF), lambda i, j: (0, j)),
            pl.BlockSpec((FFN_TF, d), lambda i, j: (j, 0)),
        ],
        out_specs=pl.BlockSpec((FFN_TM, d), lambda i, j: (i, 0)),
        out_shape=jax.ShapeDtypeStruct((t, d), F32),
        scratch_shapes=[pltpu.VMEM((FFN_TM, d), F32),
                        pltpu.VMEM((FFN_TM, d), BF16),
                        pltpu.SemaphoreType.DMA(())],
        compiler_params=_params(("arbitrary", "arbitrary"), BIG_VMEM_LIMIT),
        name="ffn",
    )(x, g, wg, wu, wd)


N_QKV_STEPS = 3 * D_ATTN // PROJ_TN
N_CONV_STEPS = D_CONV // CONV_TC


def _inproj_kernel(xa_ref, xb_ref, xp_ref, xnx_ref, g_ref, wq_ref, wb_ref, wc_ref, wu_ref,
                   cw_ref, cb_ref, gn_ref, qkv_ref, conv_ref, xn_ref, cu_ref, *, seq):
    tm = xa_ref.shape[0] + xb_ref.shape[0]
    j = pl.program_id(1)

    def qkv_tile(xn):
        acc = jnp.dot(xn, wq_ref[...].astype(BF16), preferred_element_type=F32)
        col_scale = jnp.where(j < D_ATTN // PROJ_TN, Q_SCALE, 1.0)
        qkv_ref[...] = (acc * col_scale).astype(BF16)

    @pl.when(j == 0)
    def _():
        g = g_ref[...]
        xn_ref[0:HALO, :] = _rms(xp_ref[...], g).astype(BF16)
        xn_ref[HALO + tm:, :] = _rms(xnx_ref[...], g).astype(BF16)
        xn = jnp.concatenate([_rms(xa_ref[...], g), _rms(xb_ref[...], g)], axis=0).astype(BF16)
        xn_ref[HALO:HALO + tm, :] = xn
        qkv_tile(xn)

    @pl.when((j > 0) & (j < N_QKV_STEPS))
    def _():
        qkv_tile(xn_ref[HALO:HALO + tm, :])

    @pl.when(j >= N_QKV_STEPS)
    def _():
        wb = wb_ref[...].astype(BF16)
        wc = wc_ref[...].astype(BF16)
        wu = wu_ref[...].astype(BF16)
        cw = cw_ref[...]
        cb = cb_ref[...]
        gn = gn_ref[...]
        ts = tm // CONV_SUBTILES
        for sub in range(CONV_SUBTILES):
            r0 = sub * ts
            xs = xn_ref[r0:r0 + ts + 2 * HALO, :]
            gate_b = jnp.dot(xn_ref[HALO + r0:HALO + r0 + ts, :], wb, preferred_element_type=F32)
            gate_c = jnp.dot(xs, wc, preferred_element_type=F32)
            u = jnp.dot(xs, wu, preferred_element_type=F32)
            cu_ref[sub] = gate_c * u

            pos = (lax.broadcasted_iota(jnp.int32, (ts, 1), 0) + pl.program_id(0) * tm + r0) % seq
            prev = jnp.where(pos == 0, 0.0, cu_ref[sub, HALO - 1:HALO - 1 + ts, :])
            cur = cu_ref[sub, HALO:HALO + ts, :]
            nxt = jnp.where(pos == seq - 1, 0.0, cu_ref[sub, HALO + 1:HALO + 1 + ts, :])
            y = prev * cw[0:1, :]
            y = y + cur * cw[1:2, :]
            y = y + nxt * cw[2:3, :]
            y = gate_b * (y + cb)
            for c in range(0, y.shape[1], GROUP_DIM):
                sl = slice(c, c + GROUP_DIM)
                conv_ref[r0:r0 + ts, sl] = _rms(y[:, sl], gn[:, sl]).astype(BF16)


def _inproj(x, g, w_in, conv_w, conv_b, gn, seq):
    t, d = x.shape
    tm, tn, tc = PROJ_TM, PROJ_TN, CONV_TC
    assert tn == D_ATTN
    hb = tm // HALO
    nhb = t // HALO
    col_b = 3 * D_ATTN // tc
    col_c = col_b + D_CONV // tc
    col_u = col_c + D_CONV // tc
    qstep = lambda j: jnp.minimum(j, N_QKV_STEPS - 1)
    cstep = lambda j: jnp.maximum(j - N_QKV_STEPS, 0)
    n_tiles = t // tm
    xtile = lambda i, j, after: jnp.minimum(i + jnp.where(j > after, 1, 0), n_tiles - 1)
    qwstep = lambda j: jnp.where(j < N_QKV_STEPS, j, 0)
    return pl.pallas_call(
        functools.partial(_inproj_kernel, seq=seq),
        grid=(n_tiles, N_QKV_STEPS + N_CONV_STEPS),
        in_specs=[
            pl.BlockSpec((tm // 2, d), lambda i, j: (2 * xtile(i, j, N_QKV_STEPS), 0)),
            pl.BlockSpec((tm // 2, d), lambda i, j: (2 * xtile(i, j, N_QKV_STEPS + 1) + 1, 0)),
            pl.BlockSpec((HALO, d), lambda i, j: (jnp.maximum(xtile(i, j, 0) * hb - 1, 0), 0)),
            pl.BlockSpec((HALO, d),
                         lambda i, j: (jnp.minimum((xtile(i, j, 0) + 1) * hb, nhb - 1), 0)),
            pl.BlockSpec((1, d), lambda i, j: (0, 0)),
            pl.BlockSpec((d, tn), lambda i, j: (0, qwstep(j))),
            pl.BlockSpec((d, tc), lambda i, j: (0, col_b + cstep(j))),
            pl.BlockSpec((d, tc), lambda i, j: (0, col_c + cstep(j))),
            pl.BlockSpec((d, tc), lambda i, j: (0, col_u + cstep(j))),
            pl.BlockSpec((CONV_W, tc), lambda i, j: (0, cstep(j))),
            pl.BlockSpec((1, tc), lambda i, j: (0, cstep(j))),
            pl.BlockSpec((1, tc), lambda i, j: (0, cstep(j))),
        ],
        out_specs=[pl.BlockSpec((None, tm, tn), lambda i, j: (qstep(j), i, 0)),
                   pl.BlockSpec((tm, tc), lambda i, j: (i, cstep(j)))],
        out_shape=[jax.ShapeDtypeStruct((N_QKV_STEPS, t, tn), BF16),
                   jax.ShapeDtypeStruct((t, D_CONV), BF16)],
        scratch_shapes=[pltpu.VMEM((tm + 2 * HALO, d), BF16),
                        pltpu.VMEM((CONV_SUBTILES, tm // CONV_SUBTILES + 2 * HALO, tc), F32)],
        compiler_params=_params(("parallel", "arbitrary"), BIG_VMEM_LIMIT),
        name="inproj",
    )(x, x, x, x, g, w_in, w_in, w_in, w_in, conv_w, conv_b, gn)


N_REL = 2 * WIN_ROWS - 1
PAIR_FULL = 0
PAIR_LEFT_NEG = N_REL - 1
PAIR_RIGHT_NEG = 2 * N_REL - 1
PAIR_NEG = 3 * N_REL - 1
N_ROW_PATTERNS = 3
N_PAIRS = K_ROWS // 2


def _pair_halves(block_id):
    if block_id < PAIR_LEFT_NEG:
        return block_id, block_id + 1
    if block_id < PAIR_RIGHT_NEG:
        return None, block_id - PAIR_LEFT_NEG
    if block_id < PAIR_NEG:
        return block_id - PAIR_RIGHT_NEG, None
    return None, None


def _pair_seeds(rpb, ids):
    nh, nr, nc = rpb.shape
    w = GRID_W
    scaled = jnp.concatenate([rpb * LOG2E, jnp.zeros((nh, 1, nc), F32)], axis=1)
    halves = [_pair_halves(b) for b in ids]
    left = jnp.stack([scaled[:, nr if l is None else l] for l, _ in halves], axis=1)
    right = jnp.stack([scaled[:, nr if r is None else r] for _, r in halves], axis=1)
    gap = jnp.zeros((nh, len(ids), w - 2 * WIN_COLS + 1), F32)
    seeds = jnp.concatenate(
        [left[..., WIN_COLS - 1:], gap, right, gap, left[..., :WIN_COLS - 1]], axis=-1)
    return seeds.reshape(nh * len(ids), 2 * w)


def _build_pair_blocks(seed_ref, pb_ref, ids):
    w = GRID_W
    q = lax.broadcasted_iota(jnp.int32, (w, 2 * w), 0)
    lane = lax.broadcasted_iota(jnp.int32, (w, 2 * w), 1)
    k = lane % w
    col_start = jnp.clip(q - WIN_COLS // 2, 0, w - WIN_COLS)
    in_win = (k >= col_start) & (k < col_start + WIN_COLS)
    masks = {
        (True, True): in_win,
        (False, True): in_win & (lane >= w),
        (True, False): in_win & (lane < w),
    }
    for h in range(N_HEADS):
        for n, block_id in enumerate(ids):
            left, right = _pair_halves(block_id)
            if left is None and right is None:
                pb_ref[h, n] = jnp.full((w, 2 * w), NEG, F32)
                continue
            row = h * len(ids) + n
            seed = jnp.broadcast_to(seed_ref[row:row + 1, :], (w, 2 * w))
            rotated = pltpu.roll(seed, 0, 1, stride=1, stride_axis=0)
            pb_ref[h, n] = jnp.where(masks[left is not None, right is not None], rotated, NEG)


def _pair_index_table(rows):
    tab = np.zeros((N_ROW_PATTERNS, Q_ROWS, N_PAIRS), np.int32)
    for pat, r0 in enumerate((0, Q_ROWS, rows - Q_ROWS)):
        k0 = min(max(r0 - WIN_ROWS // 2, 0), rows - K_ROWS)
        for i in range(Q_ROWS):
            r = r0 + i
            row_start = min(max(r - WIN_ROWS // 2, 0), rows - WIN_ROWS)
            for m in range(N_PAIRS):
                kr = k0 + 2 * m
                in0 = row_start <= kr < row_start + WIN_ROWS
                in1 = row_start <= kr + 1 < row_start + WIN_ROWS
                a = kr - r + WIN_ROWS - 1
                if in0 and in1:
                    tab[pat, i, m] = PAIR_FULL + a
                elif in1:
                    tab[pat, i, m] = PAIR_LEFT_NEG + a + 1
                elif in0:
                    tab[pat, i, m] = PAIR_RIGHT_NEG + a
                else:
                    tab[pat, i, m] = PAIR_NEG
    return tab


def _chunk_spans(tab):
    pairs_per_chunk = K_CHUNK_ROWS // 2
    spans = []
    for pat in range(tab.shape[0]):
        row = []
        for c in range(K_ROWS // K_CHUNK_ROWS):
            blk = tab[pat, :, c * pairs_per_chunk:(c + 1) * pairs_per_chunk]
            hit = np.nonzero((blk != PAIR_NEG).any(axis=1))[0]
            if hit.size == 0:
                row.append((0, 0))
            else:
                assert hit[-1] - hit[0] + 1 == hit.size
                row.append((int(hit[0]), int(hit[-1]) + 1))
        spans.append(row)
    return spans


def _attn_head(h, tab, spans, q_ref, ks, vs, pb_ref, gn_ref, o_ref):
    hs = slice(h * HEAD_DIM, (h + 1) * HEAD_DIM)
    pairs_per_chunk = K_CHUNK_ROWS // 2
    live = [c for c, (first, last) in enumerate(spans) if last > first]
    w = GRID_W
    scores = {}
    m = [None] * Q_ROWS
    for c in live:
        first, last = spans[c]
        s = lax.dot_general(q_ref[first * w:last * w, hs], ks[c][:, hs],
                            (((1,), (1,)), ((), ())), preferred_element_type=F32)
        for i in range(first, last):
            bias = jnp.concatenate([pb_ref[h, int(tab[i, c * pairs_per_chunk + k])]
                                    for k in range(pairs_per_chunk)], axis=1)
            si = s[(i - first) * w:(i - first + 1) * w] + bias
            scores[c, i] = si
            mi = jnp.max(si, axis=1, keepdims=True)
            m[i] = mi if m[i] is None else jnp.maximum(m[i], mi)
    l = [None] * Q_ROWS
    o = [None] * Q_ROWS
    for c in live:
        first, last = spans[c]
        ps = []
        for i in range(first, last):
            p = jnp.exp2(scores[c, i] - m[i])
            li = jnp.sum(p, axis=1, keepdims=True)
            l[i] = li if l[i] is None else l[i] + li
            ps.append(p.astype(BF16))
        oc = jnp.dot(jnp.concatenate(ps, axis=0), vs[c][:, hs], preferred_element_type=F32)
        for i in range(first, last):
            oi = oc[(i - first) * w:(i - first + 1) * w]
            o[i] = oi if o[i] is None else o[i] + oi
    gn = gn_ref[:, hs]
    for i in range(Q_ROWS):
        o_ref[i * w:(i + 1) * w, hs] = _rms(o[i] / l[i], gn).astype(BF16)


def _attn_kernel(q_ref, k0, k1, k2, k3, v0, v1, v2, v3, seed_ref, gn_ref, o_ref, pb_ref, *,
                 tab, spans, ids):
    rb = pl.program_id(1)

    @pl.when((pl.program_id(0) == 0) & (rb == 0))
    def _():
        _build_pair_blocks(seed_ref, pb_ref, ids)

    pat = jnp.where(rb == 0, 0, jnp.where(rb == pl.num_programs(1) - 1, 2, 1))
    for pat_id in range(N_ROW_PATTERNS):
        @pl.when(pat == pat_id)
        def _(pat_id=pat_id):
            for h in range(N_HEADS):
                _attn_head(h, tab[pat_id], spans[pat_id], q_ref, (k0, k1, k2, k3),
                           (v0, v1, v2, v3), pb_ref, gn_ref, o_ref)


def _attn(qkv, rpb, gn, batch, seq):
    rows = seq // GRID_W
    tq = Q_ROWS * GRID_W
    kc = K_CHUNK_ROWS * GRID_W
    n_rb = rows // Q_ROWS
    n_kc = K_ROWS // K_CHUNK_ROWS
    chunks_per_seq = seq // kc
    tab = _pair_index_table(rows)
    spans = _chunk_spans(tab)
    ids = [int(v) for v in np.unique(tab)]
    seeds = _pair_seeds(rpb, ids)
    tab = np.searchsorted(ids, tab)

    def q_map(b, rb):
        return (b * n_rb + rb, 0)

    def kv_map(plane, c):
        def f(b, rb):
            first = jnp.clip(rb * (Q_ROWS // K_CHUNK_ROWS) - (WIN_ROWS // 2) // K_CHUNK_ROWS,
                             0, chunks_per_seq - n_kc)
            return (plane, b * chunks_per_seq + first + c, 0)
        return f

    k_specs = [pl.BlockSpec((None, kc, D_ATTN), kv_map(1, c)) for c in range(n_kc)]
    v_specs = [pl.BlockSpec((None, kc, D_ATTN), kv_map(2, c)) for c in range(n_kc)]
    return pl.pallas_call(
        functools.partial(_attn_kernel, tab=tab, spans=spans, ids=ids),
        grid=(batch, n_rb),
        in_specs=[pl.BlockSpec((None, tq, D_ATTN), lambda b, rb: (0, b * n_rb + rb, 0))]
        + k_specs + v_specs + [
            pl.BlockSpec(seeds.shape, lambda b, rb: (0, 0)),
            pl.BlockSpec((1, D_ATTN), lambda b, rb: (0, 0)),
        ],
        out_specs=pl.BlockSpec((tq, D_ATTN), q_map),
        out_shape=jax.ShapeDtypeStruct((batch * seq, D_ATTN), BF16),
        scratch_shapes=[pltpu.VMEM((N_HEADS, len(ids), GRID_W, 2 * GRID_W), F32)],
        compiler_params=_params(("arbitrary", "arbitrary")),
        name="attn",
    )(qkv, *([qkv] * (2 * n_kc)), seeds, gn)


def _outproj_kernel(x_ref, a_ref, c_ref, wa_ref, wc_ref, o_ref):
    acc = jnp.dot(a_ref[...], wa_ref[...].astype(BF16), preferred_element_type=F32)
    acc = acc + jnp.dot(c_ref[...], wc_ref[...].astype(BF16), preferred_element_type=F32)
    o_ref[...] = x_ref[...] + acc


def _outproj(x, a, c, w_out):
    t, d = x.shape
    tm = OUT_TM
    return pl.pallas_call(
        _outproj_kernel,
        grid=(t // tm,),
        in_specs=[
            pl.BlockSpec((tm, d), lambda i: (i, 0)),
            pl.BlockSpec((tm, D_ATTN), lambda i: (i, 0)),
            pl.BlockSpec((tm, D_CONV), lambda i: (i, 0)),
            pl.BlockSpec((D_ATTN, d), lambda i: (0, 0), pipeline_mode=pl.Buffered(1)),
            pl.BlockSpec((D_CONV, d), lambda i: (1, 0), pipeline_mode=pl.Buffered(1)),
        ],
        out_specs=pl.BlockSpec((tm, d), lambda i: (i, 0)),
        out_shape=jax.ShapeDtypeStruct((t, d), F32),
        compiler_params=_params(("parallel",)),
        name="outproj",
    )(x, a, c, w_out, w_out)


def _ple_kernel(x_ref, p_ref, g_ref, wg_ref, wp_ref, gf_ref, o_ref):
    x = x_ref[...]
    xn = _rms(x, g_ref[...]).astype(BF16)
    gate = jax.nn.sigmoid(jnp.dot(xn, wg_ref[...].astype(BF16), preferred_element_type=F32))
    proj = jnp.dot(p_ref[...].astype(BF16), wp_ref[...].astype(BF16),
                   preferred_element_type=F32)
    o_ref[...] = _rms(x + gate * proj, gf_ref[...])


def _ple(x, p, g, w_gate, w_proj, gf):
    t, d = x.shape
    tm = PLE_TM
    return pl.pallas_call(
        _ple_kernel,
        grid=(t // tm,),
        in_specs=[
            pl.BlockSpec((tm, d), lambda i: (i, 0)),
            pl.BlockSpec((tm, PLE_DIM), lambda i: (i, 0)),
            pl.BlockSpec((1, d), lambda i: (0, 0)),
            pl.BlockSpec((d, d), lambda i: (0, 0), pipeline_mode=pl.Buffered(1)),
            pl.BlockSpec((PLE_DIM, d), lambda i: (0, 0), pipeline_mode=pl.Buffered(1)),
            pl.BlockSpec((1, d), lambda i: (0, 0)),
        ],
        out_specs=pl.BlockSpec((tm, d), lambda i: (i, 0)),
        out_shape=jax.ShapeDtypeStruct((t, d), F32),
        compiler_params=_params(("parallel",)),
        name="ple",
    )(x, p, g, w_gate, w_proj, gf)


def kernel(x, p, ffn1_norm, ffn1_wg, ffn1_wu, ffn1_wd, mix_norm, w_in, rpb, conv_w, conv_b,
           attn_out_norm, conv_out_norm, w_out, ffn2_norm, ffn2_wg, ffn2_wu, ffn2_wd,
           ple_norm, ple_w_gate, ple_w_proj, final_norm):
    b, s, d = x.shape
    assert ffn1_wg.shape[0] == 1
    assert d == D_MODEL and s % (Q_ROWS * GRID_W) == 0 and s % PROJ_TM == 0
    t = b * s
    h = x.reshape(t, d)
    row = lambda v: v.reshape(1, -1)
    for i in range(1):
        h = _ffn(h, row(ffn1_norm[i]), ffn1_wg[i], ffn1_wu[i], ffn1_wd[i])
        qkv, y_conv = _inproj(h, row(mix_norm[i]), w_in[i], conv_w[i], row(conv_b[i]),
                              row(conv_out_norm[i]), s)
        y_attn = _attn(qkv, rpb[i], row(attn_out_norm[i]), b, s)
        h = _outproj(h, y_attn, y_conv, w_out[i])
        h = _ffn(h, row(ffn2_norm[i]), ffn2_wg[i], ffn2_wu[i], ffn2_wd[i])
        h = _ple(h, p[i].reshape(t, PLE_DIM), row(ple_norm[i]), ple_w_gate[i], ple_w_proj[i],
                 row(final_norm))
    return h.reshape(b, s, d)
```

```python
import functools

import numpy as np
import jax
import jax.numpy as jnp
from jax import lax
from jax.experimental import pallas as pl
from jax.experimental.pallas import tpu as pltpu

D_MODEL = 2048
GRID_W = 64
PLE_DIM = 256
D_ATTN = D_MODEL // 2
D_CONV = D_MODEL - D_ATTN
N_HEADS = 8
HEAD_DIM = D_ATTN // N_HEADS
GROUP_DIM = 128
CONV_W = 3
WIN_ROWS = 8
WIN_COLS = 16
RMS_EPS = 1e-6
NEG = -1e30
LOG2E = 1.4426950408889634
Q_SCALE = HEAD_DIM ** -0.5 * LOG2E

BF16 = jnp.bfloat16
F32 = jnp.float32

VMEM_LIMIT = 56 * 1024 * 1024
BIG_VMEM_LIMIT = 60 * 1024 * 1024
FFN_TM = 1024
FFN_TF = 512
PROJ_TM = 1024
PROJ_TN = 1024
CONV_TC = 256
CONV_SUBTILES = 2
HALO = 16
Q_ROWS = 8
K_ROWS = 16
K_CHUNK_ROWS = 4
HEADS_PER_GROUP = 2
OUT_TM = 512
PLE_TM = 512


def _rms(x, g):
    ms = jnp.mean(x * x, axis=-1, keepdims=True)
    return x * lax.rsqrt(ms + RMS_EPS) * g


def _params(sem, vmem_limit=VMEM_LIMIT):
    return pltpu.CompilerParams(dimension_semantics=sem, vmem_limit_bytes=vmem_limit)


def _ffn_kernel(x_hbm, g_ref, wg_ref, wu_ref, wd_ref, o_ref, xbuf, xn_ref, sem):
    i = pl.program_id(0)
    j = pl.program_id(1)
    tm = xbuf.shape[0]

    def x_copy(tile):
        rows = pl.ds(pl.multiple_of(tile * tm, tm), tm)
        return pltpu.make_async_copy(x_hbm.at[rows, :], xbuf, sem)

    def half_swiglu(xn):
        gate = jnp.dot(xn, wg_ref[...].astype(BF16), preferred_element_type=F32)
        up = jnp.dot(xn, wu_ref[...].astype(BF16), preferred_element_type=F32)
        act = (jax.nn.silu(gate) * up * 0.5).astype(BF16)
        return jnp.dot(act, wd_ref[...].astype(BF16), preferred_element_type=F32)

    @pl.when(j == 0)
    def _():
        @pl.when(i == 0)
        def _():
            x_copy(0).start()

        x_copy(i).wait()
        x = xbuf[...]
        xn = _rms(x, g_ref[...]).astype(BF16)
        xn_ref[...] = xn
        o_ref[...] = x + half_swiglu(xn)

    @pl.when((j == 1) & (i + 1 < pl.num_programs(0)))
    def _():
        x_copy(i + 1).start()

    @pl.when(j > 0)
    def _():
        o_ref[...] += half_swiglu(xn_ref[...])


def _ffn(x, g, wg, wu, wd):
    t, d = x.shape
    f = wg.shape[1]
    return pl.pallas_call(
        _ffn_kernel,
        grid=(t // FFN_TM, f // FFN_TF),
        in_specs=[
            pl.BlockSpec(memory_space=pl.ANY),
            pl.BlockSpec((1, d), lambda i, j: (0, 0)),
            pl.BlockSpec((d, FFN_TF), lambda i, j: (0, j)),
            pl.BlockSpec((d, FFN_TF), lambda i, j: (0, j)),
            pl.BlockSpec((FFN_TF, d), lambda i, j: (j, 0)),
        ],
        out_specs=pl.BlockSpec((FFN_TM, d), lambda i, j: (i, 0)),
        out_shape=jax.ShapeDtypeStruct((t, d), F32),
        scratch_shapes=[pltpu.VMEM((FFN_TM, d), F32),
                        pltpu.VMEM((FFN_TM, d), BF16),
                        pltpu.SemaphoreType.DMA(())],
        compiler_params=_params(("arbitrary", "arbitrary"), BIG_VMEM_LIMIT),
        name="ffn",
    )(x, g, wg, wu, wd)


N_QKV_STEPS = 3 * D_ATTN // PROJ_TN
N_CONV_STEPS = D_CONV // CONV_TC


def _inproj_kernel(xa_ref, xb_ref, xp_ref, xnx_ref, g_ref, wq_ref, wb_ref, wc_ref, wu_ref,
                   cw_ref, cb_ref, gn_ref, qkv_ref, conv_ref, xn_ref, cu_ref, *, seq):
    tm = xa_ref.shape[0] + xb_ref.shape[0]
    j = pl.program_id(1)

    def qkv_tile(xn):
        acc = jnp.dot(xn, wq_ref[...].astype(BF16), preferred_element_type=F32)
        col_scale = jnp.where(j < D_ATTN // PROJ_TN, Q_SCALE, 1.0)
        qkv_ref[...] = (acc * col_scale).astype(BF16)

    @pl.when(j == 0)
    def _():
        g = g_ref[...]
        xn_ref[0:HALO, :] = _rms(xp_ref[...], g).astype(BF16)
        xn_ref[HALO + tm:, :] = _rms(xnx_ref[...], g).astype(BF16)
        xn = jnp.concatenate([_rms(xa_ref[...], g), _rms(xb_ref[...], g)], axis=0).astype(BF16)
        xn_ref[HALO:HALO + tm, :] = xn
        qkv_tile(xn)

    @pl.when((j > 0) & (j < N_QKV_STEPS))
    def _():
        qkv_tile(xn_ref[HALO:HALO + tm, :])

    @pl.when(j >= N_QKV_STEPS)
    def _():
        wb = wb_ref[...].astype(BF16)
        wc = wc_ref[...].astype(BF16)
        wu = wu_ref[...].astype(BF16)
        cw = cw_ref[...]
        cb = cb_ref[...]
        gn = gn_ref[...]
        ts = tm // CONV_SUBTILES
        for sub in range(CONV_SUBTILES):
            r0 = sub * ts
            xs = xn_ref[r0:r0 + ts + 2 * HALO, :]
            gate_b = jnp.dot(xn_ref[HALO + r0:HALO + r0 + ts, :], wb, preferred_element_type=F32)
            gate_c = jnp.dot(xs, wc, preferred_element_type=F32)
            u = jnp.dot(xs, wu, preferred_element_type=F32)
            cu_ref[sub] = gate_c * u

            pos = (lax.broadcasted_iota(jnp.int32, (ts, 1), 0) + pl.program_id(0) * tm + r0) % seq
            prev = jnp.where(pos == 0, 0.0, cu_ref[sub, HALO - 1:HALO - 1 + ts, :])
            cur = cu_ref[sub, HALO:HALO + ts, :]
            nxt = jnp.where(pos == seq - 1, 0.0, cu_ref[sub, HALO + 1:HALO + 1 + ts, :])
            y = prev * cw[0:1, :]
            y = y + cur * cw[1:2, :]
            y = y + nxt * cw[2:3, :]
            y = gate_b * (y + cb)
            for c in range(0, y.shape[1], GROUP_DIM):
                sl = slice(c, c + GROUP_DIM)
                conv_ref[r0:r0 + ts, sl] = _rms(y[:, sl], gn[:, sl]).astype(BF16)


def _inproj(x, g, w_in, conv_w, conv_b, gn, seq):
    t, d = x.shape
    tm, tn, tc = PROJ_TM, PROJ_TN, CONV_TC
    assert tn == D_ATTN
    hb = tm // HALO
    nhb = t // HALO
    col_b = 3 * D_ATTN // tc
    col_c = col_b + D_CONV // tc
    col_u = col_c + D_CONV // tc
    qstep = lambda j: jnp.minimum(j, N_QKV_STEPS - 1)
    cstep = lambda j: jnp.maximum(j - N_QKV_STEPS, 0)
    n_tiles = t // tm
    xtile = lambda i, j, after: jnp.minimum(i + jnp.where(j > after, 1, 0), n_tiles - 1)
    qwstep = lambda j: jnp.where(j < N_QKV_STEPS, j, 0)
    return pl.pallas_call(
        functools.partial(_inproj_kernel, seq=seq),
        grid=(n_tiles, N_QKV_STEPS + N_CONV_STEPS),
        in_specs=[
            pl.BlockSpec((tm // 2, d), lambda i, j: (2 * xtile(i, j, N_QKV_STEPS), 0)),
            pl.BlockSpec((tm // 2, d), lambda i, j: (2 * xtile(i, j, N_QKV_STEPS + 1) + 1, 0)),
            pl.BlockSpec((HALO, d), lambda i, j: (jnp.maximum(xtile(i, j, 0) * hb - 1, 0), 0)),
            pl.BlockSpec((HALO, d),
                         lambda i, j: (jnp.minimum((xtile(i, j, 0) + 1) * hb, nhb - 1), 0)),
            pl.BlockSpec((1, d), lambda i, j: (0, 0)),
            pl.BlockSpec((d, tn), lambda i, j: (0, qwstep(j))),
            pl.BlockSpec((d, tc), lambda i, j: (0, col_b + cstep(j))),
            pl.BlockSpec((d, tc), lambda i, j: (0, col_c + cstep(j))),
            pl.BlockSpec((d, tc), lambda i, j: (0, col_u + cstep(j))),
            pl.BlockSpec((CONV_W, tc), lambda i, j: (0, cstep(j))),
            pl.BlockSpec((1, tc), lambda i, j: (0, cstep(j))),
            pl.BlockSpec((1, tc), lambda i, j: (0, cstep(j))),
        ],
        out_specs=[pl.BlockSpec((None, tm, tn), lambda i, j: (qstep(j), i, 0)),
                   pl.BlockSpec((tm, tc), lambda i, j: (i, cstep(j)))],
        out_shape=[jax.ShapeDtypeStruct((N_QKV_STEPS, t, tn), BF16),
                   jax.ShapeDtypeStruct((t, D_CONV), BF16)],
        scratch_shapes=[pltpu.VMEM((tm + 2 * HALO, d), BF16),
                        pltpu.VMEM((CONV_SUBTILES, tm // CONV_SUBTILES + 2 * HALO, tc), F32)],
        compiler_params=_params(("parallel", "arbitrary"), BIG_VMEM_LIMIT),
        name="inproj",
    )(x, x, x, x, g, w_in, w_in, w_in, w_in, conv_w, conv_b, gn)


N_REL = 2 * WIN_ROWS - 1
PAIR_FULL = 0
PAIR_LEFT_NEG = N_REL - 1
PAIR_RIGHT_NEG = 2 * N_REL - 1
PAIR_NEG = 3 * N_REL - 1
N_ROW_PATTERNS = 3
N_PAIRS = K_ROWS // 2


def _pair_halves(block_id):
    if block_id < PAIR_LEFT_NEG:
        return block_id, block_id + 1
    if block_id < PAIR_RIGHT_NEG:
        return None, block_id - PAIR_LEFT_NEG
    if block_id < PAIR_NEG:
        return block_id - PAIR_RIGHT_NEG, None
    return None, None


def _pair_seeds(rpb, ids):
    nh, nr, nc = rpb.shape
    w = GRID_W
    scaled = jnp.concatenate([rpb * LOG2E, jnp.zeros((nh, 1, nc), F32)], axis=1)
    halves = [_pair_halves(b) for b in ids]
    left = jnp.stack([scaled[:, nr if l is None else l] for l, _ in halves], axis=1)
    right = jnp.stack([scaled[:, nr if r is None else r] for _, r in halves], axis=1)
    gap = jnp.zeros((nh, len(ids), w - 2 * WIN_COLS + 1), F32)
    seeds = jnp.concatenate(
        [left[..., WIN_COLS - 1:], gap, right, gap, left[..., :WIN_COLS - 1]], axis=-1)
    return seeds.reshape(nh * len(ids), 2 * w)


def _build_pair_blocks(seed_ref, pb_ref, ids):
    w = GRID_W
    q = lax.broadcasted_iota(jnp.int32, (w, 2 * w), 0)
    lane = lax.broadcasted_iota(jnp.int32, (w, 2 * w), 1)
    k = lane % w
    col_start = jnp.clip(q - WIN_COLS // 2, 0, w - WIN_COLS)
    in_win = (k >= col_start) & (k < col_start + WIN_COLS)
    masks = {
        (True, True): in_win,
        (False, True): in_win & (lane >= w),
        (True, False): in_win & (lane < w),
    }
    for h in range(N_HEADS):
        for n, block_id in enumerate(ids):
            left, right = _pair_halves(block_id)
            if left is None and right is None:
                pb_ref[h, n] = jnp.full((w, 2 * w), NEG, F32)
                continue
            row = h * len(ids) + n
            seed = jnp.broadcast_to(seed_ref[row:row + 1, :], (w, 2 * w))
            rotated = pltpu.roll(seed, 0, 1, stride=1, stride_axis=0)
            pb_ref[h, n] = jnp.where(masks[left is not None, right is not None], rotated, NEG)


def _pair_index_table(rows):
    tab = np.zeros((N_ROW_PATTERNS, Q_ROWS, N_PAIRS), np.int32)
    for pat, r0 in enumerate((0, Q_ROWS, rows - Q_ROWS)):
        k0 = min(max(r0 - WIN_ROWS // 2, 0), rows - K_ROWS)
        for i in range(Q_ROWS):
            r = r0 + i
            row_start = min(max(r - WIN_ROWS // 2, 0), rows - WIN_ROWS)
            for m in range(N_PAIRS):
                kr = k0 + 2 * m
                in0 = row_start <= kr < row_start + WIN_ROWS
                in1 = row_start <= kr + 1 < row_start + WIN_ROWS
                a = kr - r + WIN_ROWS - 1
                if in0 and in1:
                    tab[pat, i, m] = PAIR_FULL + a
                elif in1:
                    tab[pat, i, m] = PAIR_LEFT_NEG + a + 1
                elif in0:
                    tab[pat, i, m] = PAIR_RIGHT_NEG + a
                else:
                    tab[pat, i, m] = PAIR_NEG
    return tab


def _chunk_spans(tab):
    pairs_per_chunk = K_CHUNK_ROWS // 2
    spans = []
    for pat in range(tab.shape[0]):
        row = []
        for c in range(K_ROWS // K_CHUNK_ROWS):
            blk = tab[pat, :, c * pairs_per_chunk:(c + 1) * pairs_per_chunk]
            hit = np.nonzero((blk != PAIR_NEG).any(axis=1))[0]
            if hit.size == 0:
                row.append((0, 0))
            else:
                assert hit[-1] - hit[0] + 1 == hit.size
                row.append((int(hit[0]), int(hit[-1]) + 1))
        spans.append(row)
    return spans


def _attn_head(h, tab, spans, q_ref, ks, vs, pb_ref, gn_ref, o_ref):
    hs = pl.ds(pl.multiple_of(h * HEAD_DIM, HEAD_DIM), HEAD_DIM)
    pairs_per_chunk = K_CHUNK_ROWS // 2
    live = [c for c, (first, last) in enumerate(spans) if last > first]
    w = GRID_W
    scores = {}
    m = [None] * Q_ROWS
    for c in live:
        first, last = spans[c]
        s = lax.dot_general(q_ref[first * w:last * w, hs], ks[c][:, hs],
                            (((1,), (1,)), ((), ())), preferred_element_type=F32)
        for i in range(first, last):
            bias = jnp.concatenate([pb_ref[h, int(tab[i, c * pairs_per_chunk + k])]
                                    for k in range(pairs_per_chunk)], axis=1)
            si = s[(i - first) * w:(i - first + 1) * w] + bias
            scores[c, i] = si
            mi = jnp.max(si, axis=1, keepdims=True)
            m[i] = mi if m[i] is None else jnp.maximum(m[i], mi)
    l = [None] * Q_ROWS
    o = [None] * Q_ROWS
    for c in live:
        first, last = spans[c]
        ps = []
        for i in range(first, last):
            p = jnp.exp2(scores[c, i] - m[i])
            li = jnp.sum(p, axis=1, keepdims=True)
            l[i] = li if l[i] is None else l[i] + li
            ps.append(p.astype(BF16))
        oc = jnp.dot(jnp.concatenate(ps, axis=0), vs[c][:, hs], preferred_element_type=F32)
        for i in range(first, last):
            oi = oc[(i - first) * w:(i - first + 1) * w]
            o[i] = oi if o[i] is None else o[i] + oi
    gn = gn_ref[:, hs]
    for i in range(Q_ROWS):
        o_ref[i * w:(i + 1) * w, hs] = _rms(o[i] / l[i], gn).astype(BF16)


def _attn_kernel(q_ref, k0, k1, k2, k3, v0, v1, v2, v3, seed_ref, gn_ref, o_ref, pb_ref, *,
                 tab, spans, ids):
    rb = pl.program_id(1)

    @pl.when((pl.program_id(0) == 0) & (rb == 0))
    def _():
        _build_pair_blocks(seed_ref, pb_ref, ids)

    pat = jnp.where(rb == 0, 0, jnp.where(rb == pl.num_programs(1) - 1, 2, 1))
    for pat_id in range(N_ROW_PATTERNS):
        @pl.when(pat == pat_id)
        def _(pat_id=pat_id):
            def head_group(g, carry):
                for hh in range(HEADS_PER_GROUP):
                    _attn_head(g * HEADS_PER_GROUP + hh, tab[pat_id], spans[pat_id], q_ref,
                               (k0, k1, k2, k3), (v0, v1, v2, v3), pb_ref, gn_ref, o_ref)
                return carry

            lax.fori_loop(0, N_HEADS // HEADS_PER_GROUP, head_group, 0)


def _attn(qkv, rpb, gn, batch, seq):
    rows = seq // GRID_W
    tq = Q_ROWS * GRID_W
    kc = K_CHUNK_ROWS * GRID_W
    n_rb = rows // Q_ROWS
    n_kc = K_ROWS // K_CHUNK_ROWS
    chunks_per_seq = seq // kc
    tab = _pair_index_table(rows)
    spans = _chunk_spans(tab)
    ids = [int(v) for v in np.unique(tab)]
    seeds = _pair_seeds(rpb, ids)
    tab = np.searchsorted(ids, tab)

    def q_map(b, rb):
        return (b * n_rb + rb, 0)

    def kv_map(plane, c):
        def f(b, rb):
            first = jnp.clip(rb * (Q_ROWS // K_CHUNK_ROWS) - (WIN_ROWS // 2) // K_CHUNK_ROWS,
                             0, chunks_per_seq - n_kc)
            return (plane, b * chunks_per_seq + first + c, 0)
        return f

    k_specs = [pl.BlockSpec((None, kc, D_ATTN), kv_map(1, c)) for c in range(n_kc)]
    v_specs = [pl.BlockSpec((None, kc, D_ATTN), kv_map(2, c)) for c in range(n_kc)]
    return pl.pallas_call(
        functools.partial(_attn_kernel, tab=tab, spans=spans, ids=ids),
        grid=(batch, n_rb),
        in_specs=[pl.BlockSpec((None, tq, D_ATTN), lambda b, rb: (0, b * n_rb + rb, 0))]
        + k_specs + v_specs + [
            pl.BlockSpec(seeds.shape, lambda b, rb: (0, 0)),
            pl.BlockSpec((1, D_ATTN), lambda b, rb: (0, 0)),
        ],
        out_specs=pl.BlockSpec((tq, D_ATTN), q_map),
        out_shape=jax.ShapeDtypeStruct((batch * seq, D_ATTN), BF16),
        scratch_shapes=[pltpu.VMEM((N_HEADS, len(ids), GRID_W, 2 * GRID_W), F32)],
        compiler_params=_params(("arbitrary", "arbitrary")),
        name="attn",
    )(qkv, *([qkv] * (2 * n_kc)), seeds, gn)


def _outproj_kernel(x_ref, a_ref, c_ref, wa_ref, wc_ref, o_ref):
    acc = jnp.dot(a_ref[...], wa_ref[...].astype(BF16), preferred_element_type=F32)
    acc = acc + jnp.dot(c_ref[...], wc_ref[...].astype(BF16), preferred_element_type=F32)
    o_ref[...] = x_ref[...] + acc


def _outproj(x, a, c, w_out):
    t, d = x.shape
    tm = OUT_TM
    return pl.pallas_call(
        _outproj_kernel,
        grid=(t // tm,),
        in_specs=[
            pl.BlockSpec((tm, d), lambda i: (i, 0)),
            pl.BlockSpec((tm, D_ATTN), lambda i: (i, 0)),
            pl.BlockSpec((tm, D_CONV), lambda i: (i, 0)),
            pl.BlockSpec((D_ATTN, d), lambda i: (0, 0), pipeline_mode=pl.Buffered(1)),
            pl.BlockSpec((D_CONV, d), lambda i: (1, 0), pipeline_mode=pl.Buffered(1)),
        ],
        out_specs=pl.BlockSpec((tm, d), lambda i: (i, 0)),
        out_shape=jax.ShapeDtypeStruct((t, d), F32),
        compiler_params=_params(("parallel",)),
        name="outproj",
    )(x, a, c, w_out, w_out)


def _ple_kernel(x_ref, p_ref, g_ref, wg_ref, wp_ref, gf_ref, o_ref):
    x = x_ref[...]
    xn = _rms(x, g_ref[...]).astype(BF16)
    gate = jax.nn.sigmoid(jnp.dot(xn, wg_ref[...].astype(BF16), preferred_element_type=F32))
    proj = jnp.dot(p_ref[...].astype(BF16), wp_ref[...].astype(BF16),
                   preferred_element_type=F32)
    o_ref[...] = _rms(x + gate * proj, gf_ref[...])


def _ple(x, p, g, w_gate, w_proj, gf):
    t, d = x.shape
    tm = PLE_TM
    return pl.pallas_call(
        _ple_kernel,
        grid=(t // tm,),
        in_specs=[
            pl.BlockSpec((tm, d), lambda i: (i, 0)),
            pl.BlockSpec((tm, PLE_DIM), lambda i: (i, 0)),
            pl.BlockSpec((1, d), lambda i: (0, 0)),
            pl.BlockSpec((d, d), lambda i: (0, 0), pipeline_mode=pl.Buffered(1)),
            pl.BlockSpec((PLE_DIM, d), lambda i: (0, 0), pipeline_mode=pl.Buffered(1)),
            pl.BlockSpec((1, d), lambda i: (0, 0)),
        ],
        out_specs=pl.BlockSpec((tm, d), lambda i: (i, 0)),
        out_shape=jax.ShapeDtypeStruct((t, d), F32),
        compiler_params=_params(("parallel",)),
        name="ple",
    )(x, p, g, w_gate, w_proj, gf)


def kernel(x, p, ffn1_norm, ffn1_wg, ffn1_wu, ffn1_wd, mix_norm, w_in, rpb, conv_w, conv_b,
           attn_out_norm, conv_out_norm, w_out, ffn2_norm, ffn2_wg, ffn2_wu, ffn2_wd,
           ple_norm, ple_w_gate, ple_w_proj, final_norm):
    b, s, d = x.shape
    assert ffn1_wg.shape[0] == 1
    assert d == D_MODEL and s % (Q_ROWS * GRID_W) == 0 and s % PROJ_TM == 0
    t = b * s
    h = x.reshape(t, d)
    row = lambda v: v.reshape(1, -1)
    for i in range(1):
        h = _ffn(h, row(ffn1_norm[i]), ffn1_wg[i], ffn1_wu[i], ffn1_wd[i])
        qkv, y_conv = _inproj(h, row(mix_norm[i]), w_in[i], conv_w[i], row(conv_b[i]),
                              row(conv_out_norm[i]), s)
        y_attn = _attn(qkv, rpb[i], row(attn_out_norm[i]), b, s)
        h = _outproj(h, y_attn, y_conv, w_out[i])
        h = _ffn(h, row(ffn2_norm[i]), ffn2_wg[i], ffn2_wu[i], ffn2_wd[i])
        h = _ple(h, p[i].reshape(t, PLE_DIM), row(ple_norm[i]), ple_w_gate[i], ple_w_proj[i],
                 row(final_norm))
    return h.reshape(b, s, d)
```

```python
import functools

import numpy as np
import jax
import jax.numpy as jnp
from jax import lax
from jax.experimental import pallas as pl
from jax.experimental.pallas import tpu as pltpu

D_MODEL = 2048
GRID_W = 64
PLE_DIM = 256
D_ATTN = D_MODEL // 2
D_CONV = D_MODEL - D_ATTN
N_HEADS = 8
HEAD_DIM = D_ATTN // N_HEADS
GROUP_DIM = 128
CONV_W = 3
WIN_ROWS = 8
WIN_COLS = 16
RMS_EPS = 1e-6
NEG = -1e30
LOG2E = 1.4426950408889634
Q_SCALE = HEAD_DIM ** -0.5 * LOG2E

BF16 = jnp.bfloat16
F32 = jnp.float32

VMEM_LIMIT = 56 * 1024 * 1024
BIG_VMEM_LIMIT = 60 * 1024 * 1024
FFN_TM = 1024
FFN_TF = 512
PROJ_TM = 1024
PROJ_TN = 1024
CONV_TC = 256
CONV_SUBTILES = 2
HALO = 16
Q_ROWS = 8
K_ROWS = 16
K_CHUNK_ROWS = 4
OUT_TM = 512
PLE_TM = 512


def _rms(x, g):
    ms = jnp.mean(x * x, axis=-1, keepdims=True)
    return x * lax.rsqrt(ms + RMS_EPS) * g


def _params(sem, vmem_limit=VMEM_LIMIT):
    return pltpu.CompilerParams(dimension_semantics=sem, vmem_limit_bytes=vmem_limit)


def _ffn_kernel(x_hbm, g_ref, wg_ref, wu_ref, wd_ref, o_ref, xbuf, xn_ref, sem):
    i = pl.program_id(0)
    j = pl.program_id(1)
    tm = xbuf.shape[0]

    def x_copy(tile):
        rows = pl.ds(pl.multiple_of(tile * tm, tm), tm)
        return pltpu.make_async_copy(x_hbm.at[rows, :], xbuf, sem)

    def half_swiglu(xn):
        gate = jnp.dot(xn, wg_ref[...].astype(BF16), preferred_element_type=F32)
        up = jnp.dot(xn, wu_ref[...].astype(BF16), preferred_element_type=F32)
        act = (jax.nn.silu(gate) * up * 0.5).astype(BF16)
        return jnp.dot(act, wd_ref[...].astype(BF16), preferred_element_type=F32)

    @pl.when(j == 0)
    def _():
        @pl.when(i == 0)
        def _():
            x_copy(0).start()

        x_copy(i).wait()
        x = xbuf[...]
        xn = _rms(x, g_ref[...]).astype(BF16)
        xn_ref[...] = xn
        o_ref[...] = x + half_swiglu(xn)

    @pl.when((j == 1) & (i + 1 < pl.num_programs(0)))
    def _():
        x_copy(i + 1).start()

    @pl.when(j > 0)
    def _():
        o_ref[...] += half_swiglu(xn_ref[...])


def _ffn(x, g, wg, wu, wd):
    t, d = x.shape
    f = wg.shape[1]
    return pl.pallas_call(
        _ffn_kernel,
        grid=(t // FFN_TM, f // FFN_TF),
        in_specs=[
            pl.BlockSpec(memory_space=pl.ANY),
            pl.BlockSpec((1, d), lambda i, j: (0, 0)),
            pl.BlockSpec((d, FFN_TF), lambda i, j: (0, j)),
            pl.BlockSpec((d, FFN_TF), lambda i, j: (0, j)),
            pl.BlockSpec((FFN_TF, d), lambda i, j: (j, 0)),
        ],
        out_specs=pl.BlockSpec((FFN_TM, d), lambda i, j: (i, 0)),
        out_shape=jax.ShapeDtypeStruct((t, d), F32),
        scratch_shapes=[pltpu.VMEM((FFN_TM, d), F32),
                        pltpu.VMEM((FFN_TM, d), BF16),
                        pltpu.SemaphoreType.DMA(())],
        compiler_params=_params(("arbitrary", "arbitrary"), BIG_VMEM_LIMIT),
        name="ffn",
    )(x, g, wg, wu, wd)


N_QKV_STEPS = 3 * D_ATTN // PROJ_TN
N_CONV_STEPS = D_CONV // CONV_TC


def _inproj_kernel(xa_ref, xb_ref, xp_ref, xnx_ref, g_ref, wq_ref, wb_ref, wc_ref, wu_ref,
                   cw_ref, cb_ref, gn_ref, qkv_ref, conv_ref, xn_ref, cu_ref, *, seq):
    tm = xa_ref.shape[0] + xb_ref.shape[0]
    j = pl.program_id(1)

    def qkv_tile(xn):
        acc = jnp.dot(xn, wq_ref[...].astype(BF16), preferred_element_type=F32)
        col_scale = jnp.where(j < D_ATTN // PROJ_TN, Q_SCALE, 1.0)
        qkv_ref[...] = (acc * col_scale).astype(BF16)

    @pl.when(j == 0)
    def _():
        g = g_ref[...]
        xn_ref[0:HALO, :] = _rms(xp_ref[...], g).astype(BF16)
        xn_ref[HALO + tm:, :] = _rms(xnx_ref[...], g).astype(BF16)
        xn = jnp.concatenate([_rms(xa_ref[...], g), _rms(xb_ref[...], g)], axis=0).astype(BF16)
        xn_ref[HALO:HALO + tm, :] = xn
        qkv_tile(xn)

    @pl.when((j > 0) & (j < N_QKV_STEPS))
    def _():
        qkv_tile(xn_ref[HALO:HALO + tm, :])

    @pl.when(j >= N_QKV_STEPS)
    def _():
        wb = wb_ref[...].astype(BF16)
        wc = wc_ref[...].astype(BF16)
        wu = wu_ref[...].astype(BF16)
        cw = cw_ref[...]
        cb = cb_ref[...]
        gn = gn_ref[...]
        ts = tm // CONV_SUBTILES
        for sub in range(CONV_SUBTILES):
            r0 = sub * ts
            xs = xn_ref[r0:r0 + ts + 2 * HALO, :]
            gate_b = jnp.dot(xn_ref[HALO + r0:HALO + r0 + ts, :], wb, preferred_element_type=F32)
            gate_c = jnp.dot(xs, wc, preferred_element_type=F32)
            u = jnp.dot(xs, wu, preferred_element_type=F32)
            cu_ref[sub] = gate_c * u

            pos = (lax.broadcasted_iota(jnp.int32, (ts, 1), 0) + pl.program_id(0) * tm + r0) % seq
            prev = jnp.where(pos == 0, 0.0, cu_ref[sub, HALO - 1:HALO - 1 + ts, :])
            cur = cu_ref[sub, HALO:HALO + ts, :]
            nxt = jnp.where(pos == seq - 1, 0.0, cu_ref[sub, HALO + 1:HALO + 1 + ts, :])
            y = prev * cw[0:1, :]
            y = y + cur * cw[1:2, :]
            y = y + nxt * cw[2:3, :]
            y = gate_b * (y + cb)
            for c in range(0, y.shape[1], GROUP_DIM):
                sl = slice(c, c + GROUP_DIM)
                conv_ref[r0:r0 + ts, sl] = _rms(y[:, sl], gn[:, sl]).astype(BF16)


def _inproj(x, g, w_in, conv_w, conv_b, gn, seq):
    t, d = x.shape
    tm, tn, tc = PROJ_TM, PROJ_TN, CONV_TC
    hb = tm // HALO
    nhb = t // HALO
    col_b = 3 * D_ATTN // tc
    col_c = col_b + D_CONV // tc
    col_u = col_c + D_CONV // tc
    qstep = lambda j: jnp.minimum(j, N_QKV_STEPS - 1)
    cstep = lambda j: jnp.maximum(j - N_QKV_STEPS, 0)
    n_tiles = t // tm
    xtile = lambda i, j, after: jnp.minimum(i + jnp.where(j > after, 1, 0), n_tiles - 1)
    qwstep = lambda j: jnp.where(j < N_QKV_STEPS, j, 0)
    return pl.pallas_call(
        functools.partial(_inproj_kernel, seq=seq),
        grid=(n_tiles, N_QKV_STEPS + N_CONV_STEPS),
        in_specs=[
            pl.BlockSpec((tm // 2, d), lambda i, j: (2 * xtile(i, j, N_QKV_STEPS), 0)),
            pl.BlockSpec((tm // 2, d), lambda i, j: (2 * xtile(i, j, N_QKV_STEPS + 1) + 1, 0)),
            pl.BlockSpec((HALO, d), lambda i, j: (jnp.maximum(xtile(i, j, 0) * hb - 1, 0), 0)),
            pl.BlockSpec((HALO, d),
                         lambda i, j: (jnp.minimum((xtile(i, j, 0) + 1) * hb, nhb - 1), 0)),
            pl.BlockSpec((1, d), lambda i, j: (0, 0)),
            pl.BlockSpec((d, tn), lambda i, j: (0, qwstep(j))),
            pl.BlockSpec((d, tc), lambda i, j: (0, col_b + cstep(j))),
            pl.BlockSpec((d, tc), lambda i, j: (0, col_c + cstep(j))),
            pl.BlockSpec((d, tc), lambda i, j: (0, col_u + cstep(j))),
            pl.BlockSpec((CONV_W, tc), lambda i, j: (0, cstep(j))),
            pl.BlockSpec((1, tc), lambda i, j: (0, cstep(j))),
            pl.BlockSpec((1, tc), lambda i, j: (0, cstep(j))),
        ],
        out_specs=[pl.BlockSpec((tm, tn), lambda i, j: (i, qstep(j))),
                   pl.BlockSpec((tm, tc), lambda i, j: (i, cstep(j)))],
        out_shape=[jax.ShapeDtypeStruct((t, 3 * D_ATTN), BF16),
                   jax.ShapeDtypeStruct((t, D_CONV), BF16)],
        scratch_shapes=[pltpu.VMEM((tm + 2 * HALO, d), BF16),
                        pltpu.VMEM((CONV_SUBTILES, tm // CONV_SUBTILES + 2 * HALO, tc), F32)],
        compiler_params=_params(("parallel", "arbitrary"), BIG_VMEM_LIMIT),
        name="inproj",
    )(x, x, x, x, g, w_in, w_in, w_in, w_in, conv_w, conv_b, gn)


N_REL = 2 * WIN_ROWS - 1
PAIR_FULL = 0
PAIR_LEFT_NEG = N_REL - 1
PAIR_RIGHT_NEG = 2 * N_REL - 1
PAIR_NEG = 3 * N_REL - 1
N_ROW_PATTERNS = 3
N_PAIRS = K_ROWS // 2


def _pair_halves(block_id):
    if block_id < PAIR_LEFT_NEG:
        return block_id, block_id + 1
    if block_id < PAIR_RIGHT_NEG:
        return None, block_id - PAIR_LEFT_NEG
    if block_id < PAIR_NEG:
        return block_id - PAIR_RIGHT_NEG, None
    return None, None


def _pair_seeds(rpb, ids):
    nh, nr, nc = rpb.shape
    w = GRID_W
    scaled = jnp.concatenate([rpb * LOG2E, jnp.zeros((nh, 1, nc), F32)], axis=1)
    halves = [_pair_halves(b) for b in ids]
    left = jnp.stack([scaled[:, nr if l is None else l] for l, _ in halves], axis=1)
    right = jnp.stack([scaled[:, nr if r is None else r] for _, r in halves], axis=1)
    gap = jnp.zeros((nh, len(ids), w - 2 * WIN_COLS + 1), F32)
    seeds = jnp.concatenate(
        [left[..., WIN_COLS - 1:], gap, right, gap, left[..., :WIN_COLS - 1]], axis=-1)
    return seeds.reshape(nh * len(ids), 2 * w)


def _build_pair_blocks(seed_ref, pb_ref, ids):
    w = GRID_W
    q = lax.broadcasted_iota(jnp.int32, (w, 2 * w), 0)
    lane = lax.broadcasted_iota(jnp.int32, (w, 2 * w), 1)
    k = lane % w
    col_start = jnp.clip(q - WIN_COLS // 2, 0, w - WIN_COLS)
    in_win = (k >= col_start) & (k < col_start + WIN_COLS)
    masks = {
        (True, True): in_win,
        (False, True): in_win & (lane >= w),
        (True, False): in_win & (lane < w),
    }
    for h in range(N_HEADS):
        for n, block_id in enumerate(ids):
            left, right = _pair_halves(block_id)
            if left is None and right is None:
                pb_ref[h, n] = jnp.full((w, 2 * w), NEG, F32)
                continue
            row = h * len(ids) + n
            seed = jnp.broadcast_to(seed_ref[row:row + 1, :], (w, 2 * w))
            rotated = pltpu.roll(seed, 0, 1, stride=1, stride_axis=0)
            pb_ref[h, n] = jnp.where(masks[left is not None, right is not None], rotated, NEG)


def _pair_index_table(rows):
    tab = np.zeros((N_ROW_PATTERNS, Q_ROWS, N_PAIRS), np.int32)
    for pat, r0 in enumerate((0, Q_ROWS, rows - Q_ROWS)):
        k0 = min(max(r0 - WIN_ROWS // 2, 0), rows - K_ROWS)
        for i in range(Q_ROWS):
            r = r0 + i
            row_start = min(max(r - WIN_ROWS // 2, 0), rows - WIN_ROWS)
            for m in range(N_PAIRS):
                kr = k0 + 2 * m
                in0 = row_start <= kr < row_start + WIN_ROWS
                in1 = row_start <= kr + 1 < row_start + WIN_ROWS
                a = kr - r + WIN_ROWS - 1
                if in0 and in1:
                    tab[pat, i, m] = PAIR_FULL + a
                elif in1:
                    tab[pat, i, m] = PAIR_LEFT_NEG + a + 1
                elif in0:
                    tab[pat, i, m] = PAIR_RIGHT_NEG + a
                else:
                    tab[pat, i, m] = PAIR_NEG
    return tab


def _chunk_spans(tab):
    pairs_per_chunk = K_CHUNK_ROWS // 2
    spans = []
    for pat in range(tab.shape[0]):
        row = []
        for c in range(K_ROWS // K_CHUNK_ROWS):
            blk = tab[pat, :, c * pairs_per_chunk:(c + 1) * pairs_per_chunk]
            hit = np.nonzero((blk != PAIR_NEG).any(axis=1))[0]
            if hit.size == 0:
                row.append((0, 0))
            else:
                assert hit[-1] - hit[0] + 1 == hit.size
                row.append((int(hit[0]), int(hit[-1]) + 1))
        spans.append(row)
    return spans


def _attn_head(h, tab, spans, q_ref, ks, vs, pb_ref, gn_ref, o_ref):
    hs = slice(h * HEAD_DIM, (h + 1) * HEAD_DIM)
    pairs_per_chunk = K_CHUNK_ROWS // 2
    live = [c for c, (first, last) in enumerate(spans) if last > first]
    w = GRID_W
    scores = {}
    m = [None] * Q_ROWS
    for c in live:
        first, last = spans[c]
        s = lax.dot_general(q_ref[first * w:last * w, hs], ks[c][:, hs],
                            (((1,), (1,)), ((), ())), preferred_element_type=F32)
        for i in range(first, last):
            bias = jnp.concatenate([pb_ref[h, int(tab[i, c * pairs_per_chunk + k])]
                                    for k in range(pairs_per_chunk)], axis=1)
            si = s[(i - first) * w:(i - first + 1) * w] + bias
            scores[c, i] = si
            mi = jnp.max(si, axis=1, keepdims=True)
            m[i] = mi if m[i] is None else jnp.maximum(m[i], mi)
    l = [None] * Q_ROWS
    o = [None] * Q_ROWS
    for c in live:
        first, last = spans[c]
        ps = []
        for i in range(first, last):
            p = jnp.exp2(scores[c, i] - m[i])
            li = jnp.sum(p, axis=1, keepdims=True)
            l[i] = li if l[i] is None else l[i] + li
            ps.append(p.astype(BF16))
        oc = jnp.dot(jnp.concatenate(ps, axis=0), vs[c][:, hs], preferred_element_type=F32)
        for i in range(first, last):
            oi = oc[(i - first) * w:(i - first + 1) * w]
            o[i] = oi if o[i] is None else o[i] + oi
    gn = gn_ref[:, hs]
    for i in range(Q_ROWS):
        o_ref[i * w:(i + 1) * w, hs] = _rms(o[i] / l[i], gn).astype(BF16)


def _attn_kernel(q_ref, k0, k1, k2, k3, v0, v1, v2, v3, seed_ref, gn_ref, o_ref, pb_ref, *,
                 tab, spans, ids):
    rb = pl.program_id(1)

    @pl.when((pl.program_id(0) == 0) & (rb == 0))
    def _():
        _build_pair_blocks(seed_ref, pb_ref, ids)

    pat = jnp.where(rb == 0, 0, jnp.where(rb == pl.num_programs(1) - 1, 2, 1))
    for pat_id in range(N_ROW_PATTERNS):
        @pl.when(pat == pat_id)
        def _(pat_id=pat_id):
            for h in range(N_HEADS):
                _attn_head(h, tab[pat_id], spans[pat_id], q_ref, (k0, k1, k2, k3),
                           (v0, v1, v2, v3), pb_ref, gn_ref, o_ref)


def _attn(qkv, rpb, gn, batch, seq):
    rows = seq // GRID_W
    tq = Q_ROWS * GRID_W
    kc = K_CHUNK_ROWS * GRID_W
    n_rb = rows // Q_ROWS
    n_kc = K_ROWS // K_CHUNK_ROWS
    chunks_per_seq = seq // kc
    tab = _pair_index_table(rows)
    spans = _chunk_spans(tab)
    ids = [int(v) for v in np.unique(tab)]
    seeds = _pair_seeds(rpb, ids)
    tab = np.searchsorted(ids, tab)

    def q_map(b, rb):
        return (b * n_rb + rb, 0)

    def kv_map(col, c):
        def f(b, rb):
            first = jnp.clip(rb * (Q_ROWS // K_CHUNK_ROWS) - (WIN_ROWS // 2) // K_CHUNK_ROWS,
                             0, chunks_per_seq - n_kc)
            return (b * chunks_per_seq + first + c, col)
        return f

    k_specs = [pl.BlockSpec((kc, D_ATTN), kv_map(1, c)) for c in range(n_kc)]
    v_specs = [pl.BlockSpec((kc, D_ATTN), kv_map(2, c)) for c in range(n_kc)]
    return pl.pallas_call(
        functools.partial(_attn_kernel, tab=tab, spans=spans, ids=ids),
        grid=(batch, n_rb),
        in_specs=[pl.BlockSpec((tq, D_ATTN), q_map)] + k_specs + v_specs + [
            pl.BlockSpec(seeds.shape, lambda b, rb: (0, 0)),
            pl.BlockSpec((1, D_ATTN), lambda b, rb: (0, 0)),
        ],
        out_specs=pl.BlockSpec((tq, D_ATTN), q_map),
        out_shape=jax.ShapeDtypeStruct((batch * seq, D_ATTN), BF16),
        scratch_shapes=[pltpu.VMEM((N_HEADS, len(ids), GRID_W, 2 * GRID_W), F32)],
        compiler_params=_params(("arbitrary", "arbitrary")),
        name="attn",
    )(qkv, *([qkv] * (2 * n_kc)), seeds, gn)


def _outproj_kernel(xe_ref, xo_ref, a_ref, c_ref, wa_ref, wc_ref, o_ref):
    def body(x_ref):
        acc = jnp.dot(a_ref[...], wa_ref[...].astype(BF16), preferred_element_type=F32)
        acc = acc + jnp.dot(c_ref[...], wc_ref[...].astype(BF16), preferred_element_type=F32)
        o_ref[...] = x_ref[...] + acc

    odd = pl.program_id(0) % 2
    pl.when(odd == 0)(lambda: body(xe_ref))
    pl.when(odd == 1)(lambda: body(xo_ref))


def _outproj(x, a, c, w_out):
    t, d = x.shape
    tm = OUT_TM
    n = t // tm
    return pl.pallas_call(
        _outproj_kernel,
        grid=(n,),
        in_specs=[
            pl.BlockSpec((tm, d), lambda i: (jnp.minimum(i + i % 2, n - 2), 0)),
            pl.BlockSpec((tm, d), lambda i: (i + 1 - i % 2, 0)),
            pl.BlockSpec((tm, D_ATTN), lambda i: (i, 0)),
            pl.BlockSpec((tm, D_CONV), lambda i: (i, 0)),
            pl.BlockSpec((D_ATTN, d), lambda i: (0, 0), pipeline_mode=pl.Buffered(1)),
            pl.BlockSpec((D_CONV, d), lambda i: (1, 0), pipeline_mode=pl.Buffered(1)),
        ],
        out_specs=pl.BlockSpec((tm, d), lambda i: (i, 0)),
        out_shape=jax.ShapeDtypeStruct((t, d), F32),
        compiler_params=_params(("parallel",)),
        name="outproj",
    )(x, x, a, c, w_out, w_out)


def _ple_kernel(xe_ref, xo_ref, p_ref, g_ref, wg_ref, wp_ref, gf_ref, o_ref):
    def body(x_ref):
        x = x_ref[...]
        xn = _rms(x, g_ref[...]).astype(BF16)
        gate = jax.nn.sigmoid(
            jnp.dot(xn, wg_ref[...].astype(BF16), preferred_element_type=F32))
        proj = jnp.dot(p_ref[...].astype(BF16), wp_ref[...].astype(BF16),
                       preferred_element_type=F32)
        o_ref[...] = _rms(x + gate * proj, gf_ref[...])

    odd = pl.program_id(0) % 2
    pl.when(odd == 0)(lambda: body(xe_ref))
    pl.when(odd == 1)(lambda: body(xo_ref))


def _ple(x, p, g, w_gate, w_proj, gf):
    t, d = x.shape
    tm = PLE_TM
    n = t // tm
    return pl.pallas_call(
        _ple_kernel,
        grid=(n,),
        in_specs=[
            pl.BlockSpec((tm, d), lambda i: (jnp.minimum(i + i % 2, n - 2), 0)),
            pl.BlockSpec((tm, d), lambda i: (i + 1 - i % 2, 0)),
            pl.BlockSpec((tm, PLE_DIM), lambda i: (i, 0)),
            pl.BlockSpec((1, d), lambda i: (0, 0)),
            pl.BlockSpec((d, d), lambda i: (0, 0), pipeline_mode=pl.Buffered(1)),
            pl.BlockSpec((PLE_DIM, d), lambda i: (0, 0), pipeline_mode=pl.Buffered(1)),
            pl.BlockSpec((1, d), lambda i: (0, 0)),
        ],
        out_specs=pl.BlockSpec((tm, d), lambda i: (i, 0)),
        out_shape=jax.ShapeDtypeStruct((t, d), F32),
        compiler_params=_params(("parallel",)),
        name="ple",
    )(x, x, p, g, w_gate, w_proj, gf)


def kernel(x, p, ffn1_norm, ffn1_wg, ffn1_wu, ffn1_wd, mix_norm, w_in, rpb, conv_w, conv_b,
           attn_out_norm, conv_out_norm, w_out, ffn2_norm, ffn2_wg, ffn2_wu, ffn2_wd,
           ple_norm, ple_w_gate, ple_w_proj, final_norm):
    b, s, d = x.shape
    assert ffn1_wg.shape[0] == 1
    assert d == D_MODEL and s % (Q_ROWS * GRID_W) == 0 and s % PROJ_TM == 0
    t = b * s
    h = x.reshape(t, d)
    row = lambda v: v.reshape(1, -1)
    for i in range(1):
        h = _ffn(h, row(ffn1_norm[i]), ffn1_wg[i], ffn1_wu[i], ffn1_wd[i])
        qkv, y_conv = _inproj(h, row(mix_norm[i]), w_in[i], conv_w[i], row(conv_b[i]),
                              row(conv_out_norm[i]), s)
        y_attn = _attn(qkv, rpb[i], row(attn_out_norm[i]), b, s)
        h = _outproj(h, y_attn, y_conv, w_out[i])
        h = _ffn(h, row(ffn2_norm[i]), ffn2_wg[i], ffn2_wu[i], ffn2_wd[i])
        h = _ple(h, p[i].reshape(t, PLE_DIM), row(ple_norm[i]), ple_w_gate[i], ple_w_proj[i],
                 row(final_norm))
    return h.reshape(b, s, d)
```

```python
import functools

import numpy as np
import jax
import jax.numpy as jnp
from jax import lax
from jax.experimental import pallas as pl
from jax.experimental.pallas import tpu as pltpu

D_MODEL = 2048
GRID_W = 64
PLE_DIM = 256
D_ATTN = D_MODEL // 2
D_CONV = D_MODEL - D_ATTN
N_HEADS = 8
HEAD_DIM = D_ATTN // N_HEADS
GROUP_DIM = 128
CONV_W = 3
WIN_ROWS = 8
WIN_COLS = 16
RMS_EPS = 1e-6
NEG = -1e30
LOG2E = 1.4426950408889634
Q_SCALE = HEAD_DIM ** -0.5 * LOG2E

BF16 = jnp.bfloat16
F32 = jnp.float32

VMEM_LIMIT = 56 * 1024 * 1024
BIG_VMEM_LIMIT = 60 * 1024 * 1024
ATTN_VMEM_LIMIT = 24 * 1024 * 1024
FFN_TM = 1024
FFN_TF = 512
PROJ_TM = 1024
PROJ_TN = 1024
CONV_TC = 256
CONV_SUBTILES = 2
HALO = 16
Q_ROWS = 8
K_ROWS = 16
K_CHUNK_ROWS = 4
OUT_TM = 512
PLE_TM = 512


def _rms(x, g):
    ms = jnp.mean(x * x, axis=-1, keepdims=True)
    return x * lax.rsqrt(ms + RMS_EPS) * g


def _params(sem, vmem_limit=VMEM_LIMIT):
    return pltpu.CompilerParams(dimension_semantics=sem, vmem_limit_bytes=vmem_limit)


def _ffn_kernel(x_hbm, g_ref, wg_ref, wu_ref, wd_ref, o_ref, xbuf, xn_ref, sem):
    i = pl.program_id(0)
    j = pl.program_id(1)
    tm = xbuf.shape[0]

    def x_copy(tile):
        rows = pl.ds(pl.multiple_of(tile * tm, tm), tm)
        return pltpu.make_async_copy(x_hbm.at[rows, :], xbuf, sem)

    def half_swiglu(xn):
        gate = jnp.dot(xn, wg_ref[...].astype(BF16), preferred_element_type=F32)
        up = jnp.dot(xn, wu_ref[...].astype(BF16), preferred_element_type=F32)
        act = (jax.nn.silu(gate) * up * 0.5).astype(BF16)
        return jnp.dot(act, wd_ref[...].astype(BF16), preferred_element_type=F32)

    @pl.when(j == 0)
    def _():
        @pl.when(i == 0)
        def _():
            x_copy(0).start()

        x_copy(i).wait()
        x = xbuf[...]
        xn = _rms(x, g_ref[...]).astype(BF16)
        xn_ref[...] = xn
        o_ref[...] = x + half_swiglu(xn)

    @pl.when((j == 1) & (i + 1 < pl.num_programs(0)))
    def _():
        x_copy(i + 1).start()

    @pl.when(j > 0)
    def _():
        o_ref[...] += half_swiglu(xn_ref[...])


def _ffn(x, g, wg, wu, wd):
    t, d = x.shape
    f = wg.shape[1]
    return pl.pallas_call(
        _ffn_kernel,
        grid=(t // FFN_TM, f // FFN_TF),
        in_specs=[
            pl.BlockSpec(memory_space=pl.ANY),
            pl.BlockSpec((1, d), lambda i, j: (0, 0)),
            pl.BlockSpec((d, FFN_TF), lambda i, j: (0, j)),
            pl.BlockSpec((d, FFN_TF), lambda i, j: (0, j)),
            pl.BlockSpec((FFN_TF, d), lambda i, j: (j, 0)),
        ],
        out_specs=pl.BlockSpec((FFN_TM, d), lambda i, j: (i, 0)),
        out_shape=jax.ShapeDtypeStruct((t, d), F32),
        scratch_shapes=[pltpu.VMEM((FFN_TM, d), F32),
                        pltpu.VMEM((FFN_TM, d), BF16),
                        pltpu.SemaphoreType.DMA(())],
        compiler_params=_params(("arbitrary", "arbitrary"), BIG_VMEM_LIMIT),
        name="ffn",
    )(x, g, wg, wu, wd)


N_QKV_STEPS = 3 * D_ATTN // PROJ_TN
N_CONV_STEPS = D_CONV // CONV_TC


def _inproj_kernel(xa_ref, xb_ref, xp_ref, xnx_ref, g_ref, wq_ref, wb_ref, wc_ref, wu_ref,
                   cw_ref, cb_ref, gn_ref, qkv_ref, conv_ref, xn_ref, cu_ref, *, seq):
    tm = xa_ref.shape[0] + xb_ref.shape[0]
    j = pl.program_id(1)

    def qkv_tile(xn):
        acc = jnp.dot(xn, wq_ref[...].astype(BF16), preferred_element_type=F32)
        col_scale = jnp.where(j < D_ATTN // PROJ_TN, Q_SCALE, 1.0)
        qkv_ref[...] = (acc * col_scale).astype(BF16)

    @pl.when(j == 0)
    def _():
        g = g_ref[...]
        xn_ref[0:HALO, :] = _rms(xp_ref[...], g).astype(BF16)
        xn_ref[HALO + tm:, :] = _rms(xnx_ref[...], g).astype(BF16)
        xn = jnp.concatenate([_rms(xa_ref[...], g), _rms(xb_ref[...], g)], axis=0).astype(BF16)
        xn_ref[HALO:HALO + tm, :] = xn
        qkv_tile(xn)

    @pl.when((j > 0) & (j < N_QKV_STEPS))
    def _():
        qkv_tile(xn_ref[HALO:HALO + tm, :])

    @pl.when(j >= N_QKV_STEPS)
    def _():
        wb = wb_ref[...].astype(BF16)
        wc = wc_ref[...].astype(BF16)
        wu = wu_ref[...].astype(BF16)
        cw = cw_ref[...]
        cb = cb_ref[...]
        gn = gn_ref[...]
        ts = tm // CONV_SUBTILES
        for sub in range(CONV_SUBTILES):
            r0 = sub * ts
            xs = xn_ref[r0:r0 + ts + 2 * HALO, :]
            gate_b = jnp.dot(xn_ref[HALO + r0:HALO + r0 + ts, :], wb, preferred_element_type=F32)
            gate_c = jnp.dot(xs, wc, preferred_element_type=F32)
            u = jnp.dot(xs, wu, preferred_element_type=F32)
            cu_ref[sub] = gate_c * u

            pos = (lax.broadcasted_iota(jnp.int32, (ts, 1), 0) + pl.program_id(0) * tm + r0) % seq
            prev = jnp.where(pos == 0, 0.0, cu_ref[sub, HALO - 1:HALO - 1 + ts, :])
            cur = cu_ref[sub, HALO:HALO + ts, :]
            nxt = jnp.where(pos == seq - 1, 0.0, cu_ref[sub, HALO + 1:HALO + 1 + ts, :])
            y = prev * cw[0:1, :]
            y = y + cur * cw[1:2, :]
            y = y + nxt * cw[2:3, :]
            y = gate_b * (y + cb)
            for c in range(0, y.shape[1], GROUP_DIM):
                sl = slice(c, c + GROUP_DIM)
                conv_ref[r0:r0 + ts, sl] = _rms(y[:, sl], gn[:, sl]).astype(BF16)


def _inproj(x, g, w_in, conv_w, conv_b, gn, seq):
    t, d = x.shape
    tm, tn, tc = PROJ_TM, PROJ_TN, CONV_TC
    hb = tm // HALO
    nhb = t // HALO
    col_b = 3 * D_ATTN // tc
    col_c = col_b + D_CONV // tc
    col_u = col_c + D_CONV // tc
    qstep = lambda j: jnp.minimum(j, N_QKV_STEPS - 1)
    cstep = lambda j: jnp.maximum(j - N_QKV_STEPS, 0)
    n_tiles = t // tm
    xtile = lambda i, j, after: jnp.minimum(i + jnp.where(j > after, 1, 0), n_tiles - 1)
    qwstep = lambda j: jnp.where(j < N_QKV_STEPS, j, 0)
    return pl.pallas_call(
        functools.partial(_inproj_kernel, seq=seq),
        grid=(n_tiles, N_QKV_STEPS + N_CONV_STEPS),
        in_specs=[
            pl.BlockSpec((tm // 2, d), lambda i, j: (2 * xtile(i, j, N_QKV_STEPS), 0)),
            pl.BlockSpec((tm // 2, d), lambda i, j: (2 * xtile(i, j, N_QKV_STEPS + 1) + 1, 0)),
            pl.BlockSpec((HALO, d), lambda i, j: (jnp.maximum(xtile(i, j, 0) * hb - 1, 0), 0)),
            pl.BlockSpec((HALO, d),
                         lambda i, j: (jnp.minimum((xtile(i, j, 0) + 1) * hb, nhb - 1), 0)),
            pl.BlockSpec((1, d), lambda i, j: (0, 0)),
            pl.BlockSpec((d, tn), lambda i, j: (0, qwstep(j))),
            pl.BlockSpec((d, tc), lambda i, j: (0, col_b + cstep(j))),
            pl.BlockSpec((d, tc), lambda i, j: (0, col_c + cstep(j))),
            pl.BlockSpec((d, tc), lambda i, j: (0, col_u + cstep(j))),
            pl.BlockSpec((CONV_W, tc), lambda i, j: (0, cstep(j))),
            pl.BlockSpec((1, tc), lambda i, j: (0, cstep(j))),
            pl.BlockSpec((1, tc), lambda i, j: (0, cstep(j))),
        ],
        out_specs=[pl.BlockSpec((tm, tn), lambda i, j: (i, qstep(j))),
                   pl.BlockSpec((tm, tc), lambda i, j: (i, cstep(j)))],
        out_shape=[jax.ShapeDtypeStruct((t, 3 * D_ATTN), BF16),
                   jax.ShapeDtypeStruct((t, D_CONV), BF16)],
        scratch_shapes=[pltpu.VMEM((tm + 2 * HALO, d), BF16),
                        pltpu.VMEM((CONV_SUBTILES, tm // CONV_SUBTILES + 2 * HALO, tc), F32)],
        compiler_params=_params(("parallel", "arbitrary"), BIG_VMEM_LIMIT),
        name="inproj",
    )(x, x, x, x, g, w_in, w_in, w_in, w_in, conv_w, conv_b, gn)


N_REL = 2 * WIN_ROWS - 1
PAIR_FULL = 0
PAIR_LEFT_NEG = N_REL - 1
PAIR_RIGHT_NEG = 2 * N_REL - 1
PAIR_NEG = 3 * N_REL - 1
N_ROW_PATTERNS = 3
N_PAIRS = K_ROWS // 2


def _pair_halves(block_id):
    if block_id < PAIR_LEFT_NEG:
        return block_id, block_id + 1
    if block_id < PAIR_RIGHT_NEG:
        return None, block_id - PAIR_LEFT_NEG
    if block_id < PAIR_NEG:
        return block_id - PAIR_RIGHT_NEG, None
    return None, None


def _pair_seeds(rpb, ids):
    nh, nr, nc = rpb.shape
    w = GRID_W
    scaled = jnp.concatenate([rpb * LOG2E, jnp.zeros((nh, 1, nc), F32)], axis=1)
    halves = [_pair_halves(b) for b in ids]
    left = jnp.stack([scaled[:, nr if l is None else l] for l, _ in halves], axis=1)
    right = jnp.stack([scaled[:, nr if r is None else r] for _, r in halves], axis=1)
    gap = jnp.zeros((nh, len(ids), w - 2 * WIN_COLS + 1), F32)
    seeds = jnp.concatenate(
        [left[..., WIN_COLS - 1:], gap, right, gap, left[..., :WIN_COLS - 1]], axis=-1)
    return seeds.reshape(nh * len(ids), 2 * w)


def _build_pair_blocks(seed_ref, pb_ref, ids):
    w = GRID_W
    q = lax.broadcasted_iota(jnp.int32, (w, 2 * w), 0)
    lane = lax.broadcasted_iota(jnp.int32, (w, 2 * w), 1)
    k = lane % w
    col_start = jnp.clip(q - WIN_COLS // 2, 0, w - WIN_COLS)
    in_win = (k >= col_start) & (k < col_start + WIN_COLS)
    masks = {
        (True, True): in_win,
        (False, True): in_win & (lane >= w),
        (True, False): in_win & (lane < w),
    }
    for h in range(N_HEADS):
        for n, block_id in enumerate(ids):
            left, right = _pair_halves(block_id)
            if left is None and right is None:
                pb_ref[h, n] = jnp.full((w, 2 * w), NEG, F32)
                continue
            row = h * len(ids) + n
            seed = jnp.broadcast_to(seed_ref[row:row + 1, :], (w, 2 * w))
            rotated = pltpu.roll(seed, 0, 1, stride=1, stride_axis=0)
            pb_ref[h, n] = jnp.where(masks[left is not None, right is not None], rotated, NEG)


def _pair_index_table(rows):
    tab = np.zeros((N_ROW_PATTERNS, Q_ROWS, N_PAIRS), np.int32)
    for pat, r0 in enumerate((0, Q_ROWS, rows - Q_ROWS)):
        k0 = min(max(r0 - WIN_ROWS // 2, 0), rows - K_ROWS)
        for i in range(Q_ROWS):
            r = r0 + i
            row_start = min(max(r - WIN_ROWS // 2, 0), rows - WIN_ROWS)
            for m in range(N_PAIRS):
                kr = k0 + 2 * m
                in0 = row_start <= kr < row_start + WIN_ROWS
                in1 = row_start <= kr + 1 < row_start + WIN_ROWS
                a = kr - r + WIN_ROWS - 1
                if in0 and in1:
                    tab[pat, i, m] = PAIR_FULL + a
                elif in1:
                    tab[pat, i, m] = PAIR_LEFT_NEG + a + 1
                elif in0:
                    tab[pat, i, m] = PAIR_RIGHT_NEG + a
                else:
                    tab[pat, i, m] = PAIR_NEG
    return tab


def _chunk_spans(tab):
    pairs_per_chunk = K_CHUNK_ROWS // 2
    spans = []
    for pat in range(tab.shape[0]):
        row = []
        for c in range(K_ROWS // K_CHUNK_ROWS):
            blk = tab[pat, :, c * pairs_per_chunk:(c + 1) * pairs_per_chunk]
            hit = np.nonzero((blk != PAIR_NEG).any(axis=1))[0]
            if hit.size == 0:
                row.append((0, 0))
            else:
                assert hit[-1] - hit[0] + 1 == hit.size
                row.append((int(hit[0]), int(hit[-1]) + 1))
        spans.append(row)
    return spans


def _attn_head(h, tab, spans, q_ref, ks, vs, pb_ref, gn_ref, o_ref):
    hs = slice(h * HEAD_DIM, (h + 1) * HEAD_DIM)
    pairs_per_chunk = K_CHUNK_ROWS // 2
    live = [c for c, (first, last) in enumerate(spans) if last > first]
    w = GRID_W
    scores = {}
    m = [None] * Q_ROWS
    for c in live:
        first, last = spans[c]
        s = lax.dot_general(q_ref[first * w:last * w, hs], ks[c][:, hs],
                            (((1,), (1,)), ((), ())), preferred_element_type=F32)
        for i in range(first, last):
            bias = jnp.concatenate([pb_ref[h, int(tab[i, c * pairs_per_chunk + k])]
                                    for k in range(pairs_per_chunk)], axis=1)
            si = s[(i - first) * w:(i - first + 1) * w] + bias
            scores[c, i] = si
            mi = jnp.max(si, axis=1, keepdims=True)
            m[i] = mi if m[i] is None else jnp.maximum(m[i], mi)
    l = [None] * Q_ROWS
    o = [None] * Q_ROWS
    for c in live:
        first, last = spans[c]
        ps = []
        for i in range(first, last):
            p = jnp.exp2(scores[c, i] - m[i])
            li = jnp.sum(p, axis=1, keepdims=True)
            l[i] = li if l[i] is None else l[i] + li
            ps.append(p.astype(BF16))
        oc = jnp.dot(jnp.concatenate(ps, axis=0), vs[c][:, hs], preferred_element_type=F32)
        for i in range(first, last):
            oi = oc[(i - first) * w:(i - first + 1) * w]
            o[i] = oi if o[i] is None else o[i] + oi
    gn = gn_ref[:, hs]
    for i in range(Q_ROWS):
        o_ref[i * w:(i + 1) * w, hs] = _rms(o[i] / l[i], gn).astype(BF16)


def _attn_kernel(q_ref, k0, k1, k2, k3, v0, v1, v2, v3, seed_ref, gn_ref, o_ref, pb_ref, *,
                 tab, spans, ids):
    rb = pl.program_id(1)

    @pl.when((pl.program_id(0) == 0) & (rb == 0))
    def _():
        _build_pair_blocks(seed_ref, pb_ref, ids)

    pat = jnp.where(rb == 0, 0, jnp.where(rb == pl.num_programs(1) - 1, 2, 1))
    for pat_id in range(N_ROW_PATTERNS):
        @pl.when(pat == pat_id)
        def _(pat_id=pat_id):
            for h in range(N_HEADS):
                _attn_head(h, tab[pat_id], spans[pat_id], q_ref, (k0, k1, k2, k3),
                           (v0, v1, v2, v3), pb_ref, gn_ref, o_ref)


def _attn(qkv, rpb, gn, batch, seq):
    rows = seq // GRID_W
    tq = Q_ROWS * GRID_W
    kc = K_CHUNK_ROWS * GRID_W
    n_rb = rows // Q_ROWS
    n_kc = K_ROWS // K_CHUNK_ROWS
    chunks_per_seq = seq // kc
    tab = _pair_index_table(rows)
    spans = _chunk_spans(tab)
    ids = [int(v) for v in np.unique(tab)]
    seeds = _pair_seeds(rpb, ids)
    tab = np.searchsorted(ids, tab)

    def q_map(b, rb):
        return (b * n_rb + rb, 0)

    def kv_map(col, c):
        def f(b, rb):
            first = jnp.clip(rb * (Q_ROWS // K_CHUNK_ROWS) - (WIN_ROWS // 2) // K_CHUNK_ROWS,
                             0, chunks_per_seq - n_kc)
            return (b * chunks_per_seq + first + c, col)
        return f

    k_specs = [pl.BlockSpec((kc, D_ATTN), kv_map(1, c)) for c in range(n_kc)]
    v_specs = [pl.BlockSpec((kc, D_ATTN), kv_map(2, c)) for c in range(n_kc)]
    return pl.pallas_call(
        functools.partial(_attn_kernel, tab=tab, spans=spans, ids=ids),
        grid=(batch, n_rb),
        in_specs=[pl.BlockSpec((tq, D_ATTN), q_map)] + k_specs + v_specs + [
            pl.BlockSpec(seeds.shape, lambda b, rb: (0, 0)),
            pl.BlockSpec((1, D_ATTN), lambda b, rb: (0, 0)),
        ],
        out_specs=pl.BlockSpec((tq, D_ATTN), q_map),
        out_shape=jax.ShapeDtypeStruct((batch * seq, D_ATTN), BF16),
        scratch_shapes=[pltpu.VMEM((N_HEADS, len(ids), GRID_W, 2 * GRID_W), F32)],
        compiler_params=_params(("arbitrary", "arbitrary"), ATTN_VMEM_LIMIT),
        name="attn",
    )(qkv, *([qkv] * (2 * n_kc)), seeds, gn)


def _outproj_kernel(xe_ref, xo_ref, a_ref, c_ref, wa_ref, wc_ref, o_ref):
    def body(x_ref):
        acc = jnp.dot(a_ref[...], wa_ref[...].astype(BF16), preferred_element_type=F32)
        acc = acc + jnp.dot(c_ref[...], wc_ref[...].astype(BF16), preferred_element_type=F32)
        o_ref[...] = x_ref[...] + acc

    odd = pl.program_id(0) % 2
    pl.when(odd == 0)(lambda: body(xe_ref))
    pl.when(odd == 1)(lambda: body(xo_ref))


def _outproj(x, a, c, w_out):
    t, d = x.shape
    tm = OUT_TM
    n = t // tm
    return pl.pallas_call(
        _outproj_kernel,
        grid=(n,),
        in_specs=[
            pl.BlockSpec((tm, d), lambda i: (jnp.minimum(i + i % 2, n - 2), 0)),
            pl.BlockSpec((tm, d), lambda i: (i + 1 - i % 2, 0)),
            pl.BlockSpec((tm, D_ATTN), lambda i: (i, 0)),
            pl.BlockSpec((tm, D_CONV), lambda i: (i, 0)),
            pl.BlockSpec((D_ATTN, d), lambda i: (0, 0), pipeline_mode=pl.Buffered(1)),
            pl.BlockSpec((D_CONV, d), lambda i: (1, 0), pipeline_mode=pl.Buffered(1)),
        ],
        out_specs=pl.BlockSpec((tm, d), lambda i: (i, 0)),
        out_shape=jax.ShapeDtypeStruct((t, d), F32),
        compiler_params=_params(("parallel",)),
        name="outproj",
    )(x, x, a, c, w_out, w_out)


def _ple_kernel(x_ref, p_ref, g_ref, wg_ref, wp_ref, gf_ref, o_ref):
    x = x_ref[...]
    xn = _rms(x, g_ref[...]).astype(BF16)
    gate = jax.nn.sigmoid(jnp.dot(xn, wg_ref[...].astype(BF16), preferred_element_type=F32))
    proj = jnp.dot(p_ref[...].astype(BF16), wp_ref[...].astype(BF16),
                   preferred_element_type=F32)
    o_ref[...] = _rms(x + gate * proj, gf_ref[...])


def _ple(x, p, g, w_gate, w_proj, gf):
    t, d = x.shape
    tm = PLE_TM
    return pl.pallas_call(
        _ple_kernel,
        grid=(t // tm,),
        in_specs=[
            pl.BlockSpec((tm, d), lambda i: (i, 0)),
            pl.BlockSpec((tm, PLE_DIM), lambda i: (i, 0)),
            pl.BlockSpec((1, d), lambda i: (0, 0)),
            pl.BlockSpec((d, d), lambda i: (0, 0), pipeline_mode=pl.Buffered(1)),
            pl.BlockSpec((PLE_DIM, d), lambda i: (0, 0), pipeline_mode=pl.Buffered(1)),
            pl.BlockSpec((1, d), lambda i: (0, 0)),
        ],
        out_specs=pl.BlockSpec((tm, d), lambda i: (i, 0)),
        out_shape=jax.ShapeDtypeStruct((t, d), F32),
        compiler_params=_params(("parallel",)),
        name="ple",
    )(x, p, g, w_gate, w_proj, gf)


def kernel(x, p, ffn1_norm, ffn1_wg, ffn1_wu, ffn1_wd, mix_norm, w_in, rpb, conv_w, conv_b,
           attn_out_norm, conv_out_norm, w_out, ffn2_norm, ffn2_wg, ffn2_wu, ffn2_wd,
           ple_norm, ple_w_gate, ple_w_proj, final_norm):
    b, s, d = x.shape
    assert ffn1_wg.shape[0] == 1
    assert d == D_MODEL and s % (Q_ROWS * GRID_W) == 0 and s % PROJ_TM == 0
    t = b * s
    h = x.reshape(t, d)
    row = lambda v: v.reshape(1, -1)
    for i in range(1):
        h = _ffn(h, row(ffn1_norm[i]), ffn1_wg[i], ffn1_wu[i], ffn1_wd[i])
        qkv, y_conv = _inproj(h, row(mix_norm[i]), w_in[i], conv_w[i], row(conv_b[i]),
                              row(conv_out_norm[i]), s)
        y_attn = _attn(qkv, rpb[i], row(attn_out_norm[i]), b, s)
        h = _outproj(h, y_attn, y_conv, w_out[i])
        h = _ffn(h, row(ffn2_norm[i]), ffn2_wg[i], ffn2_wu[i], ffn2_wd[i])
        h = _ple(h, p[i].reshape(t, PLE_DIM), row(ple_norm[i]), ple_w_gate[i], ple_w_proj[i],
                 row(final_norm))
    return h.reshape(b, s, d)
```

```python
import functools

import numpy as np
import jax
import jax.numpy as jnp
from jax import lax
from jax.experimental import pallas as pl
from jax.experimental.pallas import tpu as pltpu

D_MODEL = 2048
GRID_W = 64
PLE_DIM = 256
D_ATTN = D_MODEL // 2
D_CONV = D_MODEL - D_ATTN
N_HEADS = 8
HEAD_DIM = D_ATTN // N_HEADS
GROUP_DIM = 128
CONV_W = 3
WIN_ROWS = 8
WIN_COLS = 16
RMS_EPS = 1e-6
NEG = -1e30
LOG2E = 1.4426950408889634
Q_SCALE = HEAD_DIM ** -0.5 * LOG2E

BF16 = jnp.bfloat16
F32 = jnp.float32

VMEM_LIMIT = 56 * 1024 * 1024
BIG_VMEM_LIMIT = 60 * 1024 * 1024
FFN_TM = 1024
FFN_TF = 512
PROJ_TM = 1024
PROJ_TN = 1024
CONV_TC = 256
CONV_SUBTILES = 2
X_PARTS = 4
HALO = 16
Q_ROWS = 8
K_ROWS = 16
K_CHUNK_ROWS = 4
OUT_TM = 512
PLE_TM = 512


def _rms(x, g):
    ms = jnp.mean(x * x, axis=-1, keepdims=True)
    return x * lax.rsqrt(ms + RMS_EPS) * g


def _params(sem, vmem_limit=VMEM_LIMIT):
    return pltpu.CompilerParams(dimension_semantics=sem, vmem_limit_bytes=vmem_limit)


def _ffn_kernel(x_hbm, g_ref, wg_ref, wu_ref, wd_ref, o_ref, xbuf, xn_ref, sem):
    i = pl.program_id(0)
    j = pl.program_id(1)
    tm = xbuf.shape[0]

    def x_copy(tile):
        rows = pl.ds(pl.multiple_of(tile * tm, tm), tm)
        return pltpu.make_async_copy(x_hbm.at[rows, :], xbuf, sem)

    def half_swiglu(xn):
        gate = jnp.dot(xn, wg_ref[...].astype(BF16), preferred_element_type=F32)
        up = jnp.dot(xn, wu_ref[...].astype(BF16), preferred_element_type=F32)
        act = (jax.nn.silu(gate) * up * 0.5).astype(BF16)
        return jnp.dot(act, wd_ref[...].astype(BF16), preferred_element_type=F32)

    @pl.when(j == 0)
    def _():
        @pl.when(i == 0)
        def _():
            x_copy(0).start()

        x_copy(i).wait()
        x = xbuf[...]
        xn = _rms(x, g_ref[...]).astype(BF16)
        xn_ref[...] = xn
        o_ref[...] = x + half_swiglu(xn)

    @pl.when((j == 1) & (i + 1 < pl.num_programs(0)))
    def _():
        x_copy(i + 1).start()

    @pl.when(j > 0)
    def _():
        o_ref[...] += half_swiglu(xn_ref[...])


def _ffn(x, g, wg, wu, wd):
    t, d = x.shape
    f = wg.shape[1]
    return pl.pallas_call(
        _ffn_kernel,
        grid=(t // FFN_TM, f // FFN_TF),
        in_specs=[
            pl.BlockSpec(memory_space=pl.ANY),
            pl.BlockSpec((1, d), lambda i, j: (0, 0)),
            pl.BlockSpec((d, FFN_TF), lambda i, j: (0, j)),
            pl.BlockSpec((d, FFN_TF), lambda i, j: (0, j)),
            pl.BlockSpec((FFN_TF, d), lambda i, j: (j, 0)),
        ],
        out_specs=pl.BlockSpec((FFN_TM, d), lambda i, j: (i, 0)),
        out_shape=jax.ShapeDtypeStruct((t, d), F32),
        scratch_shapes=[pltpu.VMEM((FFN_TM, d), F32),
                        pltpu.VMEM((FFN_TM, d), BF16),
                        pltpu.SemaphoreType.DMA(())],
        compiler_params=_params(("arbitrary", "arbitrary"), BIG_VMEM_LIMIT),
        name="ffn",
    )(x, g, wg, wu, wd)


N_QKV_STEPS = 3 * D_ATTN // PROJ_TN
N_CONV_STEPS = D_CONV // CONV_TC


def _inproj_kernel(x0_ref, x1_ref, x2_ref, x3_ref, xp_ref, xnx_ref, g_ref, wq_ref, wb_ref,
                   wc_ref, wu_ref, cw_ref, cb_ref, gn_ref, qkv_ref, conv_ref, xn_ref, cu_ref, *,
                   seq):
    x_parts = (x0_ref, x1_ref, x2_ref, x3_ref)
    tm = sum(r.shape[0] for r in x_parts)
    j = pl.program_id(1)

    def qkv_tile(xn):
        acc = jnp.dot(xn, wq_ref[...].astype(BF16), preferred_element_type=F32)
        col_scale = jnp.where(j < D_ATTN // PROJ_TN, Q_SCALE, 1.0)
        qkv_ref[...] = (acc * col_scale).astype(BF16)

    @pl.when(j == 0)
    def _():
        g = g_ref[...]
        xn_ref[0:HALO, :] = _rms(xp_ref[...], g).astype(BF16)
        xn_ref[HALO + tm:, :] = _rms(xnx_ref[...], g).astype(BF16)
        xn = jnp.concatenate([_rms(r[...], g) for r in x_parts], axis=0).astype(BF16)
        xn_ref[HALO:HALO + tm, :] = xn
        qkv_tile(xn)

    @pl.when((j > 0) & (j < N_QKV_STEPS))
    def _():
        qkv_tile(xn_ref[HALO:HALO + tm, :])

    @pl.when(j >= N_QKV_STEPS)
    def _():
        wb = wb_ref[...].astype(BF16)
        wc = wc_ref[...].astype(BF16)
        wu = wu_ref[...].astype(BF16)
        cw = cw_ref[...]
        cb = cb_ref[...]
        gn = gn_ref[...]
        ts = tm // CONV_SUBTILES
        for sub in range(CONV_SUBTILES):
            r0 = sub * ts
            xs = xn_ref[r0:r0 + ts + 2 * HALO, :]
            gate_b = jnp.dot(xn_ref[HALO + r0:HALO + r0 + ts, :], wb, preferred_element_type=F32)
            gate_c = jnp.dot(xs, wc, preferred_element_type=F32)
            u = jnp.dot(xs, wu, preferred_element_type=F32)
            cu_ref[sub] = gate_c * u

            pos = (lax.broadcasted_iota(jnp.int32, (ts, 1), 0) + pl.program_id(0) * tm + r0) % seq
            prev = jnp.where(pos == 0, 0.0, cu_ref[sub, HALO - 1:HALO - 1 + ts, :])
            cur = cu_ref[sub, HALO:HALO + ts, :]
            nxt = jnp.where(pos == seq - 1, 0.0, cu_ref[sub, HALO + 1:HALO + 1 + ts, :])
            y = prev * cw[0:1, :]
            y = y + cur * cw[1:2, :]
            y = y + nxt * cw[2:3, :]
            y = gate_b * (y + cb)
            for c in range(0, y.shape[1], GROUP_DIM):
                sl = slice(c, c + GROUP_DIM)
                conv_ref[r0:r0 + ts, sl] = _rms(y[:, sl], gn[:, sl]).astype(BF16)


def _inproj(x, g, w_in, conv_w, conv_b, gn, seq):
    t, d = x.shape
    tm, tn, tc = PROJ_TM, PROJ_TN, CONV_TC
    hb = tm // HALO
    nhb = t // HALO
    col_b = 3 * D_ATTN // tc
    col_c = col_b + D_CONV // tc
    col_u = col_c + D_CONV // tc
    qstep = lambda j: jnp.minimum(j, N_QKV_STEPS - 1)
    cstep = lambda j: jnp.maximum(j - N_QKV_STEPS, 0)
    n_tiles = t // tm
    xtile = lambda i, j, after: jnp.minimum(i + jnp.where(j > after, 1, 0), n_tiles - 1)
    qwstep = lambda j: jnp.where(j < N_QKV_STEPS, j, 0)
    return pl.pallas_call(
        functools.partial(_inproj_kernel, seq=seq),
        grid=(n_tiles, N_QKV_STEPS + N_CONV_STEPS),
        in_specs=[
            *[pl.BlockSpec((tm // X_PARTS, d),
                           lambda i, j, k=k: (X_PARTS * xtile(i, j, N_QKV_STEPS + k) + k, 0))
              for k in range(X_PARTS)],
            pl.BlockSpec((HALO, d), lambda i, j: (jnp.maximum(xtile(i, j, 0) * hb - 1, 0), 0)),
            pl.BlockSpec((HALO, d),
                         lambda i, j: (jnp.minimum((xtile(i, j, 0) + 1) * hb, nhb - 1), 0)),
            pl.BlockSpec((1, d), lambda i, j: (0, 0)),
            pl.BlockSpec((d, tn), lambda i, j: (0, qwstep(j))),
            pl.BlockSpec((d, tc), lambda i, j: (0, col_b + cstep(j))),
            pl.BlockSpec((d, tc), lambda i, j: (0, col_c + cstep(j))),
            pl.BlockSpec((d, tc), lambda i, j: (0, col_u + cstep(j))),
            pl.BlockSpec((CONV_W, tc), lambda i, j: (0, cstep(j))),
            pl.BlockSpec((1, tc), lambda i, j: (0, cstep(j))),
            pl.BlockSpec((1, tc), lambda i, j: (0, cstep(j))),
        ],
        out_specs=[pl.BlockSpec((tm, tn), lambda i, j: (i, qstep(j))),
                   pl.BlockSpec((tm, tc), lambda i, j: (i, cstep(j)))],
        out_shape=[jax.ShapeDtypeStruct((t, 3 * D_ATTN), BF16),
                   jax.ShapeDtypeStruct((t, D_CONV), BF16)],
        scratch_shapes=[pltpu.VMEM((tm + 2 * HALO, d), BF16),
                        pltpu.VMEM((CONV_SUBTILES, tm // CONV_SUBTILES + 2 * HALO, tc), F32)],
        compiler_params=_params(("parallel", "arbitrary"), BIG_VMEM_LIMIT),
        name="inproj",
    )(*([x] * X_PARTS), x, x, g, w_in, w_in, w_in, w_in, conv_w, conv_b, gn)


N_REL = 2 * WIN_ROWS - 1
PAIR_FULL = 0
PAIR_LEFT_NEG = N_REL - 1
PAIR_RIGHT_NEG = 2 * N_REL - 1
PAIR_NEG = 3 * N_REL - 1
N_ROW_PATTERNS = 3
N_PAIRS = K_ROWS // 2


def _pair_halves(block_id):
    if block_id < PAIR_LEFT_NEG:
        return block_id, block_id + 1
    if block_id < PAIR_RIGHT_NEG:
        return None, block_id - PAIR_LEFT_NEG
    if block_id < PAIR_NEG:
        return block_id - PAIR_RIGHT_NEG, None
    return None, None


def _pair_seeds(rpb, ids):
    nh, nr, nc = rpb.shape
    w = GRID_W
    scaled = jnp.concatenate([rpb * LOG2E, jnp.zeros((nh, 1, nc), F32)], axis=1)
    halves = [_pair_halves(b) for b in ids]
    left = jnp.stack([scaled[:, nr if l is None else l] for l, _ in halves], axis=1)
    right = jnp.stack([scaled[:, nr if r is None else r] for _, r in halves], axis=1)
    gap = jnp.zeros((nh, len(ids), w - 2 * WIN_COLS + 1), F32)
    seeds = jnp.concatenate(
        [left[..., WIN_COLS - 1:], gap, right, gap, left[..., :WIN_COLS - 1]], axis=-1)
    return seeds.reshape(nh * len(ids), 2 * w)


def _build_pair_blocks(seed_ref, pb_ref, ids):
    w = GRID_W
    q = lax.broadcasted_iota(jnp.int32, (w, 2 * w), 0)
    lane = lax.broadcasted_iota(jnp.int32, (w, 2 * w), 1)
    k = lane % w
    col_start = jnp.clip(q - WIN_COLS // 2, 0, w - WIN_COLS)
    in_win = (k >= col_start) & (k < col_start + WIN_COLS)
    masks = {
        (True, True): in_win,
        (False, True): in_win & (lane >= w),
        (True, False): in_win & (lane < w),
    }
    for h in range(N_HEADS):
        for n, block_id in enumerate(ids):
            left, right = _pair_halves(block_id)
            if left is None and right is None:
                pb_ref[h, n] = jnp.full((w, 2 * w), NEG, F32)
                continue
            row = h * len(ids) + n
            seed = jnp.broadcast_to(seed_ref[row:row + 1, :], (w, 2 * w))
            rotated = pltpu.roll(seed, 0, 1, stride=1, stride_axis=0)
            pb_ref[h, n] = jnp.where(masks[left is not None, right is not None], rotated, NEG)


def _pair_index_table(rows):
    tab = np.zeros((N_ROW_PATTERNS, Q_ROWS, N_PAIRS), np.int32)
    for pat, r0 in enumerate((0, Q_ROWS, rows - Q_ROWS)):
        k0 = min(max(r0 - WIN_ROWS // 2, 0), rows - K_ROWS)
        for i in range(Q_ROWS):
            r = r0 + i
            row_start = min(max(r - WIN_ROWS // 2, 0), rows - WIN_ROWS)
            for m in range(N_PAIRS):
                kr = k0 + 2 * m
                in0 = row_start <= kr < row_start + WIN_ROWS
                in1 = row_start <= kr + 1 < row_start + WIN_ROWS
                a = kr - r + WIN_ROWS - 1
                if in0 and in1:
                    tab[pat, i, m] = PAIR_FULL + a
                elif in1:
                    tab[pat, i, m] = PAIR_LEFT_NEG + a + 1
                elif in0:
                    tab[pat, i, m] = PAIR_RIGHT_NEG + a
                else:
                    tab[pat, i, m] = PAIR_NEG
    return tab


def _chunk_spans(tab):
    pairs_per_chunk = K_CHUNK_ROWS // 2
    spans = []
    for pat in range(tab.shape[0]):
        row = []
        for c in range(K_ROWS // K_CHUNK_ROWS):
            blk = tab[pat, :, c * pairs_per_chunk:(c + 1) * pairs_per_chunk]
            hit = np.nonzero((blk != PAIR_NEG).any(axis=1))[0]
            if hit.size == 0:
                row.append((0, 0))
            else:
                assert hit[-1] - hit[0] + 1 == hit.size
                row.append((int(hit[0]), int(hit[-1]) + 1))
        spans.append(row)
    return spans


def _attn_head(h, tab, spans, q_ref, ks, vs, pb_ref, gn_ref, o_ref):
    hs = slice(h * HEAD_DIM, (h + 1) * HEAD_DIM)
    pairs_per_chunk = K_CHUNK_ROWS // 2
    live = [c for c, (first, last) in enumerate(spans) if last > first]
    w = GRID_W
    scores = {}
    m = [None] * Q_ROWS
    for c in live:
        first, last = spans[c]
        s = lax.dot_general(q_ref[first * w:last * w, hs], ks[c][:, hs],
                            (((1,), (1,)), ((), ())), preferred_element_type=F32)
        for i in range(first, last):
            bias = jnp.concatenate([pb_ref[h, int(tab[i, c * pairs_per_chunk + k])]
                                    for k in range(pairs_per_chunk)], axis=1)
            si = s[(i - first) * w:(i - first + 1) * w] + bias
            scores[c, i] = si
            mi = jnp.max(si, axis=1, keepdims=True)
            m[i] = mi if m[i] is None else jnp.maximum(m[i], mi)
    l = [None] * Q_ROWS
    o = [None] * Q_ROWS
    for c in live:
        first, last = spans[c]
        ps = []
        for i in range(first, last):
            p = jnp.exp2(scores[c, i] - m[i])
            li = jnp.sum(p, axis=1, keepdims=True)
            l[i] = li if l[i] is None else l[i] + li
            ps.append(p.astype(BF16))
        oc = jnp.dot(jnp.concatenate(ps, axis=0), vs[c][:, hs], preferred_element_type=F32)
        for i in range(first, last):
            oi = oc[(i - first) * w:(i - first + 1) * w]
            o[i] = oi if o[i] is None else o[i] + oi
    gn = gn_ref[:, hs]
    for i in range(Q_ROWS):
        o_ref[i * w:(i + 1) * w, hs] = _rms(o[i] / l[i], gn).astype(BF16)


def _attn_kernel(q_ref, k0, k1, k2, k3, v0, v1, v2, v3, seed_ref, gn_ref, o_ref, pb_ref, *,
                 tab, spans, ids):
    rb = pl.program_id(1)

    @pl.when((pl.program_id(0) == 0) & (rb == 0))
    def _():
        _build_pair_blocks(seed_ref, pb_ref, ids)

    pat = jnp.where(rb == 0, 0, jnp.where(rb == pl.num_programs(1) - 1, 2, 1))
    for pat_id in range(N_ROW_PATTERNS):
        @pl.when(pat == pat_id)
        def _(pat_id=pat_id):
            for h in range(N_HEADS):
                _attn_head(h, tab[pat_id], spans[pat_id], q_ref, (k0, k1, k2, k3),
                           (v0, v1, v2, v3), pb_ref, gn_ref, o_ref)


def _attn(qkv, rpb, gn, batch, seq):
    rows = seq // GRID_W
    tq = Q_ROWS * GRID_W
    kc = K_CHUNK_ROWS * GRID_W
    n_rb = rows // Q_ROWS
    n_kc = K_ROWS // K_CHUNK_ROWS
    chunks_per_seq = seq // kc
    tab = _pair_index_table(rows)
    spans = _chunk_spans(tab)
    ids = [int(v) for v in np.unique(tab)]
    seeds = _pair_seeds(rpb, ids)
    tab = np.searchsorted(ids, tab)

    def q_map(b, rb):
        return (b * n_rb + rb, 0)

    def kv_map(col, c):
        def f(b, rb):
            first = jnp.clip(rb * (Q_ROWS // K_CHUNK_ROWS) - (WIN_ROWS // 2) // K_CHUNK_ROWS,
                             0, chunks_per_seq - n_kc)
            return (b * chunks_per_seq + first + c, col)
        return f

    k_specs = [pl.BlockSpec((kc, D_ATTN), kv_map(1, c)) for c in range(n_kc)]
    v_specs = [pl.BlockSpec((kc, D_ATTN), kv_map(2, c)) for c in range(n_kc)]
    return pl.pallas_call(
        functools.partial(_attn_kernel, tab=tab, spans=spans, ids=ids),
        grid=(batch, n_rb),
        in_specs=[pl.BlockSpec((tq, D_ATTN), q_map)] + k_specs + v_specs + [
            pl.BlockSpec(seeds.shape, lambda b, rb: (0, 0)),
            pl.BlockSpec((1, D_ATTN), lambda b, rb: (0, 0)),
        ],
        out_specs=pl.BlockSpec((tq, D_ATTN), q_map),
        out_shape=jax.ShapeDtypeStruct((batch * seq, D_ATTN), BF16),
        scratch_shapes=[pltpu.VMEM((N_HEADS, len(ids), GRID_W, 2 * GRID_W), F32)],
        compiler_params=_params(("arbitrary", "arbitrary")),
        name="attn",
    )(qkv, *([qkv] * (2 * n_kc)), seeds, gn)


def _outproj_kernel(xe_ref, xo_ref, a_ref, c_ref, wa_ref, wc_ref, o_ref):
    def body(x_ref):
        acc = jnp.dot(a_ref[...], wa_ref[...].astype(BF16), preferred_element_type=F32)
        acc = acc + jnp.dot(c_ref[...], wc_ref[...].astype(BF16), preferred_element_type=F32)
        o_ref[...] = x_ref[...] + acc

    odd = pl.program_id(0) % 2
    pl.when(odd == 0)(lambda: body(xe_ref))
    pl.when(odd == 1)(lambda: body(xo_ref))


def _outproj(x, a, c, w_out):
    t, d = x.shape
    tm = OUT_TM
    n = t // tm
    return pl.pallas_call(
        _outproj_kernel,
        grid=(n,),
        in_specs=[
            pl.BlockSpec((tm, d), lambda i: (jnp.minimum(i + i % 2, n - 2), 0)),
            pl.BlockSpec((tm, d), lambda i: (i + 1 - i % 2, 0)),
            pl.BlockSpec((tm, D_ATTN), lambda i: (i, 0)),
            pl.BlockSpec((tm, D_CONV), lambda i: (i, 0)),
            pl.BlockSpec((D_ATTN, d), lambda i: (0, 0), pipeline_mode=pl.Buffered(1)),
            pl.BlockSpec((D_CONV, d), lambda i: (1, 0), pipeline_mode=pl.Buffered(1)),
        ],
        out_specs=pl.BlockSpec((tm, d), lambda i: (i, 0)),
        out_shape=jax.ShapeDtypeStruct((t, d), F32),
        compiler_params=_params(("parallel",)),
        name="outproj",
    )(x, x, a, c, w_out, w_out)


def _ple_kernel(x_ref, p_ref, g_ref, wg_ref, wp_ref, gf_ref, o_ref):
    x = x_ref[...]
    xn = _rms(x, g_ref[...]).astype(BF16)
    gate = jax.nn.sigmoid(jnp.dot(xn, wg_ref[...].astype(BF16), preferred_element_type=F32))
    proj = jnp.dot(p_ref[...].astype(BF16), wp_ref[...].astype(BF16),
                   preferred_element_type=F32)
    o_ref[...] = _rms(x + gate * proj, gf_ref[...])


def _ple(x, p, g, w_gate, w_proj, gf):
    t, d = x.shape
    tm = PLE_TM
    return pl.pallas_call(
        _ple_kernel,
        grid=(t // tm,),
        in_specs=[
            pl.BlockSpec((tm, d), lambda i: (i, 0)),
            pl.BlockSpec((tm, PLE_DIM), lambda i: (i, 0)),
            pl.BlockSpec((1, d), lambda i: (0, 0)),
            pl.BlockSpec((d, d), lambda i: (0, 0), pipeline_mode=pl.Buffered(1)),
            pl.BlockSpec((PLE_DIM, d), lambda i: (0, 0), pipeline_mode=pl.Buffered(1)),
            pl.BlockSpec((1, d), lambda i: (0, 0)),
        ],
        out_specs=pl.BlockSpec((tm, d), lambda i: (i, 0)),
        out_shape=jax.ShapeDtypeStruct((t, d), F32),
        compiler_params=_params(("parallel",)),
        name="ple",
    )(x, p, g, w_gate, w_proj, gf)


def kernel(x, p, ffn1_norm, ffn1_wg, ffn1_wu, ffn1_wd, mix_norm, w_in, rpb, conv_w, conv_b,
           attn_out_norm, conv_out_norm, w_out, ffn2_norm, ffn2_wg, ffn2_wu, ffn2_wd,
           ple_norm, ple_w_gate, ple_w_proj, final_norm):
    b, s, d = x.shape
    assert ffn1_wg.shape[0] == 1
    assert d == D_MODEL and s % (Q_ROWS * GRID_W) == 0 and s % PROJ_TM == 0
    t = b * s
    h = x.reshape(t, d)
    row = lambda v: v.reshape(1, -1)
    for i in range(1):
        h = _ffn(h, row(ffn1_norm[i]), ffn1_wg[i], ffn1_wu[i], ffn1_wd[i])
        qkv, y_conv = _inproj(h, row(mix_norm[i]), w_in[i], conv_w[i], row(conv_b[i]),
                              row(conv_out_norm[i]), s)
        y_attn = _attn(qkv, rpb[i], row(attn_out_norm[i]), b, s)
        h = _outproj(h, y_attn, y_conv, w_out[i])
        h = _ffn(h, row(ffn2_norm[i]), ffn2_wg[i], ffn2_wu[i], ffn2_wd[i])
        h = _ple(h, p[i].reshape(t, PLE_DIM), row(ple_norm[i]), ple_w_gate[i], ple_w_proj[i],
                 row(final_norm))
    return h.reshape(b, s, d)
```

```python
import functools

import numpy as np
import jax
import jax.numpy as jnp
from jax import lax
from jax.experimental import pallas as pl
from jax.experimental.pallas import tpu as pltpu

D_MODEL = 2048
GRID_W = 64
PLE_DIM = 256
D_ATTN = D_MODEL // 2
D_CONV = D_MODEL - D_ATTN
N_HEADS = 8
HEAD_DIM = D_ATTN // N_HEADS
GROUP_DIM = 128
CONV_W = 3
WIN_ROWS = 8
WIN_COLS = 16
RMS_EPS = 1e-6
NEG = -1e30
LOG2E = 1.4426950408889634
Q_SCALE = HEAD_DIM ** -0.5 * LOG2E

BF16 = jnp.bfloat16
F32 = jnp.float32

VMEM_LIMIT = 56 * 1024 * 1024
BIG_VMEM_LIMIT = 60 * 1024 * 1024
FFN_TM = 1024
FFN_TF = 512
PROJ_TM = 1024
PROJ_TN = 1024
CONV_TC = 256
CONV_SUBTILES = 2
X_PARTS = 4
HALO = 16
Q_ROWS = 8
K_ROWS = 16
K_CHUNK_ROWS = 4
OUT_TM = 512
PLE_TM = 512


def _rms(x, g):
    ms = jnp.mean(x * x, axis=-1, keepdims=True)
    return x * lax.rsqrt(ms + RMS_EPS) * g


def _params(sem, vmem_limit=VMEM_LIMIT):
    return pltpu.CompilerParams(dimension_semantics=sem, vmem_limit_bytes=vmem_limit)


def _ffn_kernel(x_hbm, g_ref, wg_ref, wu_ref, wd_ref, o_ref, xbuf, xn_ref, sem):
    i = pl.program_id(0)
    j = pl.program_id(1)
    tm = xbuf.shape[0]

    def x_copy(tile):
        rows = pl.ds(pl.multiple_of(tile * tm, tm), tm)
        return pltpu.make_async_copy(x_hbm.at[rows, :], xbuf, sem)

    def half_swiglu(xn):
        gate = jnp.dot(xn, wg_ref[...].astype(BF16), preferred_element_type=F32)
        up = jnp.dot(xn, wu_ref[...].astype(BF16), preferred_element_type=F32)
        act = (jax.nn.silu(gate) * up * 0.5).astype(BF16)
        return jnp.dot(act, wd_ref[...].astype(BF16), preferred_element_type=F32)

    @pl.when(j == 0)
    def _():
        @pl.when(i == 0)
        def _():
            x_copy(0).start()

        x_copy(i).wait()
        x = xbuf[...]
        xn = _rms(x, g_ref[...]).astype(BF16)
        xn_ref[...] = xn
        o_ref[...] = x + half_swiglu(xn)

    @pl.when((j == 1) & (i + 1 < pl.num_programs(0)))
    def _():
        x_copy(i + 1).start()

    @pl.when(j > 0)
    def _():
        o_ref[...] += half_swiglu(xn_ref[...])


def _ffn(x, g, wg, wu, wd):
    t, d = x.shape
    f = wg.shape[1]
    return pl.pallas_call(
        _ffn_kernel,
        grid=(t // FFN_TM, f // FFN_TF),
        in_specs=[
            pl.BlockSpec(memory_space=pl.ANY),
            pl.BlockSpec((1, d), lambda i, j: (0, 0)),
            pl.BlockSpec((d, FFN_TF), lambda i, j: (0, j)),
            pl.BlockSpec((d, FFN_TF), lambda i, j: (0, j)),
            pl.BlockSpec((FFN_TF, d), lambda i, j: (j, 0)),
        ],
        out_specs=pl.BlockSpec((FFN_TM, d), lambda i, j: (i, 0)),
        out_shape=jax.ShapeDtypeStruct((t, d), F32),
        scratch_shapes=[pltpu.VMEM((FFN_TM, d), F32),
                        pltpu.VMEM((FFN_TM, d), BF16),
                        pltpu.SemaphoreType.DMA(())],
        compiler_params=_params(("arbitrary", "arbitrary"), BIG_VMEM_LIMIT),
        name="ffn",
    )(x, g, wg, wu, wd)


N_QKV_STEPS = 3 * D_ATTN // PROJ_TN
N_CONV_STEPS = D_CONV // CONV_TC


def _inproj_kernel(x0_ref, x1_ref, x2_ref, x3_ref, xp_ref, xnx_ref, g_ref, wq_ref, wb_ref,
                   wc_ref, wu_ref, cw_ref, cb_ref, gn_ref, qkv_ref, conv_ref, xn_ref, cu_ref, *,
                   seq):
    x_parts = (x0_ref, x1_ref, x2_ref, x3_ref)
    tm = sum(r.shape[0] for r in x_parts)
    j = pl.program_id(1)

    def qkv_tile(xn):
        acc = jnp.dot(xn, wq_ref[...].astype(BF16), preferred_element_type=F32)
        col_scale = jnp.where(j < D_ATTN // PROJ_TN, Q_SCALE, 1.0)
        qkv_ref[...] = (acc * col_scale).astype(BF16)

    @pl.when(j == 0)
    def _():
        g = g_ref[...]
        xn_ref[0:HALO, :] = _rms(xp_ref[...], g).astype(BF16)
        xn_ref[HALO + tm:, :] = _rms(xnx_ref[...], g).astype(BF16)
        xn = jnp.concatenate([_rms(r[...], g) for r in x_parts], axis=0).astype(BF16)
        xn_ref[HALO:HALO + tm, :] = xn
        qkv_tile(xn)

    @pl.when((j > 0) & (j < N_QKV_STEPS))
    def _():
        qkv_tile(xn_ref[HALO:HALO + tm, :])

    @pl.when(j >= N_QKV_STEPS)
    def _():
        wb = wb_ref[...].astype(BF16)
        wc = wc_ref[...].astype(BF16)
        wu = wu_ref[...].astype(BF16)
        cw = cw_ref[...]
        cb = cb_ref[...]
        gn = gn_ref[...]
        ts = tm // CONV_SUBTILES
        for sub in range(CONV_SUBTILES):
            r0 = sub * ts
            xs = xn_ref[r0:r0 + ts + 2 * HALO, :]
            gate_b = jnp.dot(xn_ref[HALO + r0:HALO + r0 + ts, :], wb, preferred_element_type=F32)
            gate_c = jnp.dot(xs, wc, preferred_element_type=F32)
            u = jnp.dot(xs, wu, preferred_element_type=F32)
            cu_ref[sub] = gate_c * u

            start = pl.program_id(0) * tm + r0
            before = cu_ref[sub, HALO - 1:HALO, :]
            cu_ref[sub, HALO - 1:HALO, :] = jnp.where(start % seq == 0, 0.0, before)
            after = cu_ref[sub, HALO + ts:HALO + ts + 1, :]
            cu_ref[sub, HALO + ts:HALO + ts + 1, :] = jnp.where((start + ts) % seq == 0, 0.0, after)
            prev = cu_ref[sub, HALO - 1:HALO - 1 + ts, :]
            cur = cu_ref[sub, HALO:HALO + ts, :]
            nxt = cu_ref[sub, HALO + 1:HALO + 1 + ts, :]
            y = prev * cw[0:1, :]
            y = y + cur * cw[1:2, :]
            y = y + nxt * cw[2:3, :]
            y = gate_b * (y + cb)
            for c in range(0, y.shape[1], GROUP_DIM):
                sl = slice(c, c + GROUP_DIM)
                conv_ref[r0:r0 + ts, sl] = _rms(y[:, sl], gn[:, sl]).astype(BF16)


def _inproj(x, g, w_in, conv_w, conv_b, gn, seq):
    t, d = x.shape
    tm, tn, tc = PROJ_TM, PROJ_TN, CONV_TC
    assert seq % (tm // CONV_SUBTILES) == 0
    hb = tm // HALO
    nhb = t // HALO
    col_b = 3 * D_ATTN // tc
    col_c = col_b + D_CONV // tc
    col_u = col_c + D_CONV // tc
    qstep = lambda j: jnp.minimum(j, N_QKV_STEPS - 1)
    cstep = lambda j: jnp.maximum(j - N_QKV_STEPS, 0)
    n_tiles = t // tm
    xtile = lambda i, j, after: jnp.minimum(i + jnp.where(j > after, 1, 0), n_tiles - 1)
    qwstep = lambda j: jnp.where(j < N_QKV_STEPS, j, 0)
    return pl.pallas_call(
        functools.partial(_inproj_kernel, seq=seq),
        grid=(n_tiles, N_QKV_STEPS + N_CONV_STEPS),
        in_specs=[
            *[pl.BlockSpec((tm // X_PARTS, d),
                           lambda i, j, k=k: (X_PARTS * xtile(i, j, N_QKV_STEPS + k) + k, 0))
              for k in range(X_PARTS)],
            pl.BlockSpec((HALO, d), lambda i, j: (jnp.maximum(xtile(i, j, 0) * hb - 1, 0), 0)),
            pl.BlockSpec((HALO, d),
                         lambda i, j: (jnp.minimum((xtile(i, j, 0) + 1) * hb, nhb - 1), 0)),
            pl.BlockSpec((1, d), lambda i, j: (0, 0)),
            pl.BlockSpec((d, tn), lambda i, j: (0, qwstep(j))),
            pl.BlockSpec((d, tc), lambda i, j: (0, col_b + cstep(j))),
            pl.BlockSpec((d, tc), lambda i, j: (0, col_c + cstep(j))),
            pl.BlockSpec((d, tc), lambda i, j: (0, col_u + cstep(j))),
            pl.BlockSpec((CONV_W, tc), lambda i, j: (0, cstep(j))),
            pl.BlockSpec((1, tc), lambda i, j: (0, cstep(j))),
            pl.BlockSpec((1, tc), lambda i, j: (0, cstep(j))),
        ],
        out_specs=[pl.BlockSpec((tm, tn), lambda i, j: (i, qstep(j))),
                   pl.BlockSpec((tm, tc), lambda i, j: (i, cstep(j)))],
        out_shape=[jax.ShapeDtypeStruct((t, 3 * D_ATTN), BF16),
                   jax.ShapeDtypeStruct((t, D_CONV), BF16)],
        scratch_shapes=[pltpu.VMEM((tm + 2 * HALO, d), BF16),
                        pltpu.VMEM((CONV_SUBTILES, tm // CONV_SUBTILES + 2 * HALO, tc), F32)],
        compiler_params=_params(("parallel", "arbitrary"), BIG_VMEM_LIMIT),
        name="inproj",
    )(*([x] * X_PARTS), x, x, g, w_in, w_in, w_in, w_in, conv_w, conv_b, gn)


N_REL = 2 * WIN_ROWS - 1
PAIR_FULL = 0
PAIR_LEFT_NEG = N_REL - 1
PAIR_RIGHT_NEG = 2 * N_REL - 1
PAIR_NEG = 3 * N_REL - 1
N_ROW_PATTERNS = 3
N_PAIRS = K_ROWS // 2


def _pair_halves(block_id):
    if block_id < PAIR_LEFT_NEG:
        return block_id, block_id + 1
    if block_id < PAIR_RIGHT_NEG:
        return None, block_id - PAIR_LEFT_NEG
    if block_id < PAIR_NEG:
        return block_id - PAIR_RIGHT_NEG, None
    return None, None


def _pair_seeds(rpb, ids):
    nh, nr, nc = rpb.shape
    w = GRID_W
    scaled = jnp.concatenate([rpb * LOG2E, jnp.zeros((nh, 1, nc), F32)], axis=1)
    halves = [_pair_halves(b) for b in ids]
    left = jnp.stack([scaled[:, nr if l is None else l] for l, _ in halves], axis=1)
    right = jnp.stack([scaled[:, nr if r is None else r] for _, r in halves], axis=1)
    gap = jnp.zeros((nh, len(ids), w - 2 * WIN_COLS + 1), F32)
    seeds = jnp.concatenate(
        [left[..., WIN_COLS - 1:], gap, right, gap, left[..., :WIN_COLS - 1]], axis=-1)
    return seeds.reshape(nh * len(ids), 2 * w)


def _build_pair_blocks(seed_ref, pb_ref, ids):
    w = GRID_W
    q = lax.broadcasted_iota(jnp.int32, (w, 2 * w), 0)
    lane = lax.broadcasted_iota(jnp.int32, (w, 2 * w), 1)
    k = lane % w
    col_start = jnp.clip(q - WIN_COLS // 2, 0, w - WIN_COLS)
    in_win = (k >= col_start) & (k < col_start + WIN_COLS)
    masks = {
        (True, True): in_win,
        (False, True): in_win & (lane >= w),
        (True, False): in_win & (lane < w),
    }
    for h in range(N_HEADS):
        for n, block_id in enumerate(ids):
            left, right = _pair_halves(block_id)
            if left is None and right is None:
                pb_ref[h, n] = jnp.full((w, 2 * w), NEG, F32)
                continue
            row = h * len(ids) + n
            seed = jnp.broadcast_to(seed_ref[row:row + 1, :], (w, 2 * w))
            rotated = pltpu.roll(seed, 0, 1, stride=1, stride_axis=0)
            pb_ref[h, n] = jnp.where(masks[left is not None, right is not None], rotated, NEG)


def _pair_index_table(rows):
    tab = np.zeros((N_ROW_PATTERNS, Q_ROWS, N_PAIRS), np.int32)
    for pat, r0 in enumerate((0, Q_ROWS, rows - Q_ROWS)):
        k0 = min(max(r0 - WIN_ROWS // 2, 0), rows - K_ROWS)
        for i in range(Q_ROWS):
            r = r0 + i
            row_start = min(max(r - WIN_ROWS // 2, 0), rows - WIN_ROWS)
            for m in range(N_PAIRS):
                kr = k0 + 2 * m
                in0 = row_start <= kr < row_start + WIN_ROWS
                in1 = row_start <= kr + 1 < row_start + WIN_ROWS
                a = kr - r + WIN_ROWS - 1
                if in0 and in1:
                    tab[pat, i, m] = PAIR_FULL + a
                elif in1:
                    tab[pat, i, m] = PAIR_LEFT_NEG + a + 1
                elif in0:
                    tab[pat, i, m] = PAIR_RIGHT_NEG + a
                else:
                    tab[pat, i, m] = PAIR_NEG
    return tab


def _chunk_spans(tab):
    pairs_per_chunk = K_CHUNK_ROWS // 2
    spans = []
    for pat in range(tab.shape[0]):
        row = []
        for c in range(K_ROWS // K_CHUNK_ROWS):
            blk = tab[pat, :, c * pairs_per_chunk:(c + 1) * pairs_per_chunk]
            hit = np.nonzero((blk != PAIR_NEG).any(axis=1))[0]
            if hit.size == 0:
                row.append((0, 0))
            else:
                assert hit[-1] - hit[0] + 1 == hit.size
                row.append((int(hit[0]), int(hit[-1]) + 1))
        spans.append(row)
    return spans


def _attn_head(h, tab, spans, q_ref, ks, vs, pb_ref, gn_ref, o_ref):
    hs = slice(h * HEAD_DIM, (h + 1) * HEAD_DIM)
    pairs_per_chunk = K_CHUNK_ROWS // 2
    live = [c for c, (first, last) in enumerate(spans) if last > first]
    w = GRID_W
    scores = {}
    m = [None] * Q_ROWS
    for c in live:
        first, last = spans[c]
        s = lax.dot_general(q_ref[first * w:last * w, hs], ks[c][:, hs],
                            (((1,), (1,)), ((), ())), preferred_element_type=F32)
        for i in range(first, last):
            bias = jnp.concatenate([pb_ref[h, int(tab[i, c * pairs_per_chunk + k])]
                                    for k in range(pairs_per_chunk)], axis=1)
            si = s[(i - first) * w:(i - first + 1) * w] + bias
            scores[c, i] = si
            mi = jnp.max(si, axis=1, keepdims=True)
            m[i] = mi if m[i] is None else jnp.maximum(m[i], mi)
    l = [None] * Q_ROWS
    o = [None] * Q_ROWS
    for c in live:
        first, last = spans[c]
        ps = []
        for i in range(first, last):
            p = jnp.exp2(scores[c, i] - m[i])
            li = jnp.sum(p, axis=1, keepdims=True)
            l[i] = li if l[i] is None else l[i] + li
            ps.append(p.astype(BF16))
        oc = jnp.dot(jnp.concatenate(ps, axis=0), vs[c][:, hs], preferred_element_type=F32)
        for i in range(first, last):
            oi = oc[(i - first) * w:(i - first + 1) * w]
            o[i] = oi if o[i] is None else o[i] + oi
    gn = gn_ref[:, hs]
    for i in range(Q_ROWS):
        o_ref[i * w:(i + 1) * w, hs] = _rms(o[i] / l[i], gn).astype(BF16)


def _attn_kernel(q_ref, k0, k1, k2, k3, v0, v1, v2, v3, seed_ref, gn_ref, o_ref, pb_ref, *,
                 tab, spans, ids):
    rb = pl.program_id(1)

    @pl.when((pl.program_id(0) == 0) & (rb == 0))
    def _():
        _build_pair_blocks(seed_ref, pb_ref, ids)

    pat = jnp.where(rb == 0, 0, jnp.where(rb == pl.num_programs(1) - 1, 2, 1))
    for pat_id in range(N_ROW_PATTERNS):
        @pl.when(pat == pat_id)
        def _(pat_id=pat_id):
            for h in range(N_HEADS):
                _attn_head(h, tab[pat_id], spans[pat_id], q_ref, (k0, k1, k2, k3),
                           (v0, v1, v2, v3), pb_ref, gn_ref, o_ref)


def _attn(qkv, rpb, gn, batch, seq):
    rows = seq // GRID_W
    tq = Q_ROWS * GRID_W
    kc = K_CHUNK_ROWS * GRID_W
    n_rb = rows // Q_ROWS
    n_kc = K_ROWS // K_CHUNK_ROWS
    chunks_per_seq = seq // kc
    tab = _pair_index_table(rows)
    spans = _chunk_spans(tab)
    ids = [int(v) for v in np.unique(tab)]
    seeds = _pair_seeds(rpb, ids)
    tab = np.searchsorted(ids, tab)

    def q_map(b, rb):
        return (b * n_rb + rb, 0)

    def kv_map(col, c):
        def f(b, rb):
            first = jnp.clip(rb * (Q_ROWS // K_CHUNK_ROWS) - (WIN_ROWS // 2) // K_CHUNK_ROWS,
                             0, chunks_per_seq - n_kc)
            return (b * chunks_per_seq + first + c, col)
        return f

    k_specs = [pl.BlockSpec((kc, D_ATTN), kv_map(1, c)) for c in range(n_kc)]
    v_specs = [pl.BlockSpec((kc, D_ATTN), kv_map(2, c)) for c in range(n_kc)]
    return pl.pallas_call(
        functools.partial(_attn_kernel, tab=tab, spans=spans, ids=ids),
        grid=(batch, n_rb),
        in_specs=[pl.BlockSpec((tq, D_ATTN), q_map)] + k_specs + v_specs + [
            pl.BlockSpec(seeds.shape, lambda b, rb: (0, 0)),
            pl.BlockSpec((1, D_ATTN), lambda b, rb: (0, 0)),
        ],
        out_specs=pl.BlockSpec((tq, D_ATTN), q_map),
        out_shape=jax.ShapeDtypeStruct((batch * seq, D_ATTN), BF16),
        scratch_shapes=[pltpu.VMEM((N_HEADS, len(ids), GRID_W, 2 * GRID_W), F32)],
        compiler_params=_params(("arbitrary", "arbitrary")),
        name="attn",
    )(qkv, *([qkv] * (2 * n_kc)), seeds, gn)


def _outproj_kernel(xe_ref, xo_ref, a_ref, c_ref, wa_ref, wc_ref, o_ref):
    def body(x_ref):
        acc = jnp.dot(a_ref[...], wa_ref[...].astype(BF16), preferred_element_type=F32)
        acc = acc + jnp.dot(c_ref[...], wc_ref[...].astype(BF16), preferred_element_type=F32)
        o_ref[...] = x_ref[...] + acc

    odd = pl.program_id(0) % 2
    pl.when(odd == 0)(lambda: body(xe_ref))
    pl.when(odd == 1)(lambda: body(xo_ref))


def _outproj(x, a, c, w_out):
    t, d = x.shape
    tm = OUT_TM
    n = t // tm
    return pl.pallas_call(
        _outproj_kernel,
        grid=(n,),
        in_specs=[
            pl.BlockSpec((tm, d), lambda i: (jnp.minimum(i + i % 2, n - 2), 0)),
            pl.BlockSpec((tm, d), lambda i: (i + 1 - i % 2, 0)),
            pl.BlockSpec((tm, D_ATTN), lambda i: (i, 0)),
            pl.BlockSpec((tm, D_CONV), lambda i: (i, 0)),
            pl.BlockSpec((D_ATTN, d), lambda i: (0, 0), pipeline_mode=pl.Buffered(1)),
            pl.BlockSpec((D_CONV, d), lambda i: (1, 0), pipeline_mode=pl.Buffered(1)),
        ],
        out_specs=pl.BlockSpec((tm, d), lambda i: (i, 0)),
        out_shape=jax.ShapeDtypeStruct((t, d), F32),
        compiler_params=_params(("parallel",)),
        name="outproj",
    )(x, x, a, c, w_out, w_out)


def _ple_kernel(x_ref, p_ref, g_ref, wg_ref, wp_ref, gf_ref, o_ref):
    x = x_ref[...]
    xn = _rms(x, g_ref[...]).astype(BF16)
    gate = jax.nn.sigmoid(jnp.dot(xn, wg_ref[...].astype(BF16), preferred_element_type=F32))
    proj = jnp.dot(p_ref[...].astype(BF16), wp_ref[...].astype(BF16),
                   preferred_element_type=F32)
    o_ref[...] = _rms(x + gate * proj, gf_ref[...])


def _ple(x, p, g, w_gate, w_proj, gf):
    t, d = x.shape
    tm = PLE_TM
    return pl.pallas_call(
        _ple_kernel,
        grid=(t // tm,),
        in_specs=[
            pl.BlockSpec((tm, d), lambda i: (i, 0)),
            pl.BlockSpec((tm, PLE_DIM), lambda i: (i, 0)),
            pl.BlockSpec((1, d), lambda i: (0, 0)),
            pl.BlockSpec((d, d), lambda i: (0, 0), pipeline_mode=pl.Buffered(1)),
            pl.BlockSpec((PLE_DIM, d), lambda i: (0, 0), pipeline_mode=pl.Buffered(1)),
            pl.BlockSpec((1, d), lambda i: (0, 0)),
        ],
        out_specs=pl.BlockSpec((tm, d), lambda i: (i, 0)),
        out_shape=jax.ShapeDtypeStruct((t, d), F32),
        compiler_params=_params(("parallel",)),
        name="ple",
    )(x, p, g, w_gate, w_proj, gf)


def kernel(x, p, ffn1_norm, ffn1_wg, ffn1_wu, ffn1_wd, mix_norm, w_in, rpb, conv_w, conv_b,
           attn_out_norm, conv_out_norm, w_out, ffn2_norm, ffn2_wg, ffn2_wu, ffn2_wd,
           ple_norm, ple_w_gate, ple_w_proj, final_norm):
    b, s, d = x.shape
    assert ffn1_wg.shape[0] == 1
    assert d == D_MODEL and s % (Q_ROWS * GRID_W) == 0 and s % PROJ_TM == 0
    t = b * s
    h = x.reshape(t, d)
    row = lambda v: v.reshape(1, -1)
    for i in range(1):
        h = _ffn(h, row(ffn1_norm[i]), ffn1_wg[i], ffn1_wu[i], ffn1_wd[i])
        qkv, y_conv = _inproj(h, row(mix_norm[i]), w_in[i], conv_w[i], row(conv_b[i]),
                              row(conv_out_norm[i]), s)
        y_attn = _attn(qkv, rpb[i], row(attn_out_norm[i]), b, s)
        h = _outproj(h, y_attn, y_conv, w_out[i])
        h = _ffn(h, row(ffn2_norm[i]), ffn2_wg[i], ffn2_wu[i], ffn2_wd[i])
        h = _ple(h, p[i].reshape(t, PLE_DIM), row(ple_norm[i]), ple_w_gate[i], ple_w_proj[i],
                 row(final_norm))
    return h.reshape(b, s, d)
```
